```python
import math
import jax, jax.numpy as jnp
from jax import lax
import numpy as np

D_MODEL = 1024
BATCH = 4
SEQ = 4096
DEPTH = 4

N_MIXERS = 2
N_DIFF = (DEPTH + 1) // 2
N_GLA = DEPTH // 2

DIFF_HEADS = 8
DIFF_HEAD_DIM = 64
DIFF_V_DIM = 2 * DIFF_HEAD_DIM
DIFF_WIDTH = DIFF_HEADS * 2 * DIFF_HEAD_DIM
ROPE_THETA = 10000.0
Q_BLOCK = 128
LAMBDA_INIT_STD = 0.1

GLA_HEADS = 4
GLA_KEY_DIM = D_MODEL // 2
GLA_VAL_DIM = D_MODEL
GLA_DK = GLA_KEY_DIM // GLA_HEADS
GLA_DV = GLA_VAL_DIM // GLA_HEADS
GLA_GATE_RANK = 16
GLA_GATE_TEMP = 16.0
GLA_CHUNK = 64

D_FF = 2816
CONV_WIDTH = 3

NORM_EPS = 1e-6

kernel_name = 'hybrid_diffattn_gla_convffn'


def rms_norm(x, gain):
    xf = x.astype(jnp.float32)
    y = xf * lax.rsqrt(jnp.mean(xf * xf, axis=-1, keepdims=True) + NORM_EPS)
    return (y * gain.astype(jnp.float32)).astype(x.dtype)


def rope_tables(seq, dim):
    inv = 1.0 / (ROPE_THETA ** (jnp.arange(0, dim, 2, dtype=jnp.float32) / dim))
    ang = jnp.arange(seq, dtype=jnp.float32)[:, None] * inv[None, :]
    return jnp.cos(ang), jnp.sin(ang)


def apply_rope(t, cos, sin):
    t1, t2 = jnp.split(t, 2, axis=-1)
    c = cos[None, :, None, None, :]
    s = sin[None, :, None, None, :]
    return jnp.concatenate([t1 * c - t2 * s, t2 * c + t1 * s], axis=-1)


def diff_attention(h, w_qkv, w_o, lam_vecs, subln_gain, lambda_init, cos, sin):
    B, S, _ = h.shape
    f32 = jnp.float32
    q, k, v = jnp.split(h @ w_qkv, 3, axis=-1)
    q = q.reshape(B, S, DIFF_HEADS, 2, DIFF_HEAD_DIM).astype(f32)
    k = k.reshape(B, S, DIFF_HEADS, 2, DIFF_HEAD_DIM).astype(f32)
    v = v.reshape(B, S, DIFF_HEADS, DIFF_V_DIM).astype(f32)
    q = apply_rope(q, cos, sin) * (DIFF_HEAD_DIM ** -0.5)
    k = apply_rope(k, cos, sin)
    lv = lam_vecs.astype(f32)
    lam = jnp.exp(jnp.sum(lv[0] * lv[1])) - jnp.exp(jnp.sum(lv[2] * lv[3])) + lambda_init
    nb = S // Q_BLOCK
    q_blocks = q.reshape(B, nb, Q_BLOCK, DIFF_HEADS, 2, DIFF_HEAD_DIM).transpose(1, 0, 2, 3, 4, 5)
    k_pos = jnp.arange(S)

    def block(args):
        qb, start = args
        s = jnp.einsum('bqhmd,bkhmd->bhmqk', qb, k)
        q_pos = start + jnp.arange(Q_BLOCK)
        mask = k_pos[None, :] <= q_pos[:, None]
        p = jax.nn.softmax(jnp.where(mask, s, -jnp.inf), axis=-1)
        a = p[:, :, 0] - lam * p[:, :, 1]
        return jnp.einsum('bhqk,bkhe->bqhe', a, v)

    starts = jnp.arange(nb) * Q_BLOCK
    o = lax.map(block, (q_blocks, starts))
    o = o.transpose(1, 0, 2, 3, 4).reshape(B, S, DIFF_HEADS, DIFF_V_DIM)
    o = rms_norm(o, subln_gain) * (1.0 - lambda_init)
    return o.reshape(B, S, DIFF_WIDTH).astype(h.dtype) @ w_o


def gla(h, w_in, w_a1, w_a2, b_a, norm_gain, w_o):
    B, S, _ = h.shape
    f32 = jnp.float32
    C = GLA_CHUNK
    nc = S // C
    q, k, v, r = jnp.split(h @ w_in, [GLA_KEY_DIM, 2 * GLA_KEY_DIM, 2 * GLA_KEY_DIM + GLA_VAL_DIM], axis=-1)
    g = jax.nn.log_sigmoid(((h @ w_a1) @ w_a2 + b_a).astype(f32)) / GLA_GATE_TEMP

    def to_chunks(t, dh):
        return t.astype(f32).reshape(B, nc, C, GLA_HEADS, dh).transpose(1, 0, 3, 2, 4)

    qc = to_chunks(q, GLA_DK) * (GLA_DK ** -0.5)
    kc = to_chunks(k, GLA_DK)
    vc = to_chunks(v, GLA_DV)
    gc = to_chunks(g, GLA_DK)
    causal = jnp.tril(jnp.ones((C, C), dtype=bool))[None, None, :, :, None]

    def step(state, inp):
        qi, ki, vi, gi = inp
        b = jnp.cumsum(gi, axis=2)
        o_inter = jnp.einsum('bhcd,bhde->bhce', qi * jnp.exp(b), state)
        rel = b[:, :, :, None, :] - b[:, :, None, :, :]
        decay = jnp.exp(jnp.where(causal, rel, -jnp.inf))
        attn = jnp.einsum('bhid,bhjd,bhijd->bhij', qi, ki, decay)
        o = o_inter + jnp.einsum('bhij,bhje->bhie', attn, vi)
        b_last = b[:, :, -1:, :]
        state = jnp.exp(b_last[:, :, 0, :])[..., None] * state + jnp.einsum(
            'bhcd,bhce->bhde', ki * jnp.exp(b_last - b), vi)
        return state, o

    s0 = jnp.zeros((B, GLA_HEADS, GLA_DK, GLA_DV), f32)
    _, o = lax.scan(step, s0, (qc, kc, vc, gc))
    o = o.transpose(1, 0, 3, 2, 4).reshape(B, S, GLA_HEADS, GLA_DV)
    o = rms_norm(o, norm_gain) * jax.nn.silu(r.astype(f32).reshape(B, S, GLA_HEADS, GLA_DV))
    return o.reshape(B, S, GLA_VAL_DIM).astype(h.dtype) @ w_o


def conv_ffn(h, w_up, conv_w, conv_b, w_down):
    u = h @ w_up
    u = lax.conv_general_dilated(
        u, conv_w[:, None, :], window_strides=(1,), padding=((CONV_WIDTH - 1, 0),),
        dimension_numbers=('NWC', 'WIO', 'NWC'), feature_group_count=2 * D_FF) + conv_b
    gate, up = jnp.split(u, 2, axis=-1)
    return (jax.nn.silu(gate) * up) @ w_down


def setup_inputs(seed: int = 0) -> dict:
    key = jax.random.key(seed)
    ks = jax.random.split(key, 20)
    f32 = jnp.float32

    def w(k, shape, fan_in):
        return jax.random.normal(k, shape, f32) * (fan_in ** -0.5)

    def gain(k, shape):
        return 1.0 + 0.01 * jax.random.normal(k, shape, f32)

    return {
        'x': jax.random.normal(ks[0], (BATCH, SEQ, D_MODEL), f32),
        'norm_mix': gain(ks[1], (DEPTH, D_MODEL)),
        'norm_ffn': gain(ks[2], (DEPTH, D_MODEL)),
        'norm_final': gain(ks[3], (D_MODEL,)),
        'diff_w_qkv': w(ks[4], (N_DIFF, D_MODEL, 3 * DIFF_WIDTH), D_MODEL),
        'diff_w_o': w(ks[5], (N_DIFF, DIFF_WIDTH, D_MODEL), DIFF_WIDTH),
        'diff_lambda': LAMBDA_INIT_STD * jax.random.normal(ks[6], (N_DIFF, 4, DIFF_HEAD_DIM), f32),
        'diff_subln': gain(ks[7], (N_DIFF, DIFF_V_DIM)),
        'gla_w_in': w(ks[8], (N_GLA, D_MODEL, 2 * GLA_KEY_DIM + 2 * GLA_VAL_DIM), D_MODEL),
        'gla_w_a1': w(ks[9], (N_GLA, D_MODEL, GLA_GATE_RANK), D_MODEL),
        'gla_w_a2': w(ks[10], (N_GLA, GLA_GATE_RANK, GLA_KEY_DIM), GLA_GATE_RANK),
        'gla_b_a': 0.1 * jax.random.normal(ks[11], (N_GLA, GLA_KEY_DIM), f32),
        'gla_norm': gain(ks[12], (N_GLA, GLA_DV)),
        'gla_w_o': w(ks[13], (N_GLA, GLA_VAL_DIM, D_MODEL), GLA_VAL_DIM),
        'ffn_w_up': w(ks[14], (DEPTH, D_MODEL, 2 * D_FF), D_MODEL),
        'ffn_conv_w': w(ks[15], (DEPTH, CONV_WIDTH, 2 * D_FF), CONV_WIDTH),
        'ffn_conv_b': 0.01 * jax.random.normal(ks[16], (DEPTH, 2 * D_FF), f32),
        'ffn_w_down': w(ks[17], (DEPTH, D_FF, D_MODEL), D_FF),
    }


def reference(x, norm_mix, norm_ffn, norm_final, diff_w_qkv, diff_w_o, diff_lambda, diff_subln,
              gla_w_in, gla_w_a1, gla_w_a2, gla_b_a, gla_norm, gla_w_o,
              ffn_w_up, ffn_conv_w, ffn_conv_b, ffn_w_down):
    cos, sin = rope_tables(x.shape[1], DIFF_HEAD_DIM)
    h = x
    for layer in range(DEPTH):
        j = layer // N_MIXERS
        hn = rms_norm(h, norm_mix[layer])
        if layer % N_MIXERS == 0:
            lambda_init = 0.8 - 0.6 * math.exp(-0.3 * layer)
            h = h + diff_attention(hn, diff_w_qkv[j], diff_w_o[j], diff_lambda[j], diff_subln[j],
                                   lambda_init, cos, sin)
        else:
            h = h + gla(hn, gla_w_in[j], gla_w_a1[j], gla_w_a2[j], gla_b_a[j], gla_norm[j], gla_w_o[j])
        h = h + conv_ffn(rms_norm(h, norm_ffn[layer]), ffn_w_up[layer], ffn_conv_w[layer],
                         ffn_conv_b[layer], ffn_w_down[layer])
    return rms_norm(h, norm_final)
```

```python
import functools
import math

import jax
import jax.numpy as jnp
from jax import lax
from jax.experimental import pallas as pl
from jax.experimental.pallas import tpu as pltpu

F32 = jnp.float32
BF16 = jnp.bfloat16

NORM_EPS = 1e-6
ROPE_THETA = 10000.0

DIFF_HEADS = 8
DIFF_HEAD_DIM = 64
DIFF_V_DIM = 2 * DIFF_HEAD_DIM

GLA_HEADS = 4
GLA_DK = 128
GLA_DV = 256
GLA_GATE_TEMP = 16.0
GLA_BLOCK = 256
GLA_DIAG = 8

CONV_WIDTH = 3
FFN_HALO = 16

LANES = 128
VMEM_LIMIT_BYTES = 56 * 1024 * 1024


def _cparams(semantics):
    return pltpu.CompilerParams(dimension_semantics=semantics, vmem_limit_bytes=VMEM_LIMIT_BYTES)


def _rms(x, gain):
    ms = jnp.mean(x * x, axis=-1, keepdims=True)
    return x * lax.rsqrt(ms + NORM_EPS) * gain


def _norm_proj_kernel(x_ref, g_ref, w_ref, cos_ref, sin_ref, o_ref, xn_ref, *, n_rope_blocks, q_scale):
    j = pl.program_id(1)

    @pl.when(j == 0)
    def _():
        xn_ref[...] = _rms(x_ref[...], g_ref[...]).astype(BF16)

    acc = jnp.dot(xn_ref[...], w_ref[...], preferred_element_type=F32)
    tm, tn = acc.shape

    if n_rope_blocks == 0:
        o_ref[...] = acc.astype(o_ref.dtype)
        return

    @pl.when(j >= n_rope_blocks)
    def _():
        o_ref[...] = acc.astype(o_ref.dtype)

    @pl.when(j < n_rope_blocks)
    def _():
        c = cos_ref[...]
        s = sin_ref[...]
        scale = jnp.where(j == 0, q_scale, 1.0).astype(F32)
        lane = lax.broadcasted_iota(jnp.int32, (tm, LANES), 1)
        first_half = (lane % DIFF_HEAD_DIM) < (DIFF_HEAD_DIM // 2)
        for hh in range(tn // LANES):
            t = acc[:, hh * LANES:(hh + 1) * LANES]
            partner = jnp.where(first_half,
                                pltpu.roll(t, LANES - DIFF_HEAD_DIM // 2, 1),
                                pltpu.roll(t, DIFF_HEAD_DIM // 2, 1))
            o_ref[:, hh * LANES:(hh + 1) * LANES] = ((t * c + partner * s) * scale).astype(o_ref.dtype)


def _norm_proj(x, gain, w, cos_t, sin_t, *, tm, tn, n_rope_blocks, q_scale, seq):
    m, d = x.shape
    n = w.shape[1]
    nseq = seq // tm
    kern = functools.partial(_norm_proj_kernel, n_rope_blocks=n_rope_blocks, q_scale=q_scale)
    return pl.pallas_call(
        kern,
        out_shape=jax.ShapeDtypeStruct((m, n), BF16),
        grid=(m // tm, n // tn),
        in_specs=[
            pl.BlockSpec((tm, d), lambda i, j: (i, 0)),
            pl.BlockSpec((1, d), lambda i, j: (0, 0)),
            pl.BlockSpec((d, tn), lambda i, j: (0, j)),
            pl.BlockSpec((tm, LANES), lambda i, j: (i % nseq, 0)),
            pl.BlockSpec((tm, LANES), lambda i, j: (i % nseq, 0)),
        ],
        out_specs=pl.BlockSpec((tm, tn), lambda i, j: (i, j)),
        scratch_shapes=[pltpu.VMEM((tm, d), BF16)],
        compiler_params=_cparams(("parallel", "arbitrary")),
        name="norm_proj",
    )(x, gain, w, cos_t, sin_t)


def _diff_attn_kernel(lam_ref, sg_ref, q_ref, k_ref, v_ref, o_ref, qq_ref, m_ref, l_ref, acc_ref,
                      *, tq, tk, lambda_init):
    qi = pl.program_id(2)
    ki = pl.program_id(3)
    nk = pl.num_programs(3)

    @pl.when(ki == 0)
    def _():
        q = q_ref[0]
        lane = lax.broadcasted_iota(jnp.int32, q.shape, 1)
        zero = jnp.zeros_like(q)
        qq_ref[0:tq, :] = jnp.where(lane < DIFF_HEAD_DIM, q, zero)
        qq_ref[tq:2 * tq, :] = jnp.where(lane >= DIFF_HEAD_DIM, q, zero)
        m_ref[...] = jnp.full(m_ref.shape, -jnp.inf, F32)
        l_ref[...] = jnp.zeros(l_ref.shape, F32)
        acc_ref[...] = jnp.zeros(acc_ref.shape, F32)

    @pl.when(ki * tk < (qi + 1) * tq)
    def _():
        s = lax.dot_general(qq_ref[...], k_ref[0], (((1,), (1,)), ((), ())),
                            preferred_element_type=F32)
        r = lax.broadcasted_iota(jnp.int32, s.shape, 0)
        row = qi * tq + jnp.where(r >= tq, r - tq, r)
        col = ki * tk + lax.broadcasted_iota(jnp.int32, s.shape, 1)
        s = jnp.where(col <= row, s, -jnp.inf)
        m_prev = m_ref[...]
        m_new = jnp.maximum(m_prev, jnp.max(s, axis=1, keepdims=True))
        alpha = jnp.exp(m_prev - m_new)
        p = jnp.exp(s - m_new)
        l_ref[...] = alpha * l_ref[...] + jnp.sum(p, axis=1, keepdims=True)
        acc_ref[...] = alpha * acc_ref[...] + jnp.dot(p.astype(BF16), v_ref[0],
                                                      preferred_element_type=F32)
        m_ref[...] = m_new

    @pl.when(ki == nk - 1)
    def _():
        lv = lam_ref[...]
        lam = (jnp.exp(jnp.sum(lv[0:1] * lv[1:2], axis=1, keepdims=True))
               - jnp.exp(jnp.sum(lv[2:3] * lv[3:4], axis=1, keepdims=True)) + lambda_init)
        o1 = acc_ref[0:tq, :] / l_ref[0:tq, :]
        o2 = acc_ref[tq:2 * tq, :] / l_ref[tq:2 * tq, :]
        o = o1 - lam * o2
        o = _rms(o, sg_ref[...]) * (1.0 - lambda_init)
        o_ref[0] = o.astype(o_ref.dtype)


def _diff_attention(qkv, lam_vecs, subln_gain, *, lambda_init, tq, tk):
    b, s, _ = qkv.shape
    h = DIFF_HEADS
    kern = functools.partial(_diff_attn_kernel, tq=tq, tk=tk, lambda_init=lambda_init)

    def kv_block(qi, ki):
        return jnp.minimum(ki, ((qi + 1) * tq - 1) // tk)

    return pl.pallas_call(
        kern,
        out_shape=jax.ShapeDtypeStruct((b, s, h * DIFF_V_DIM), BF16),
        grid=(b, h, s // tq, s // tk),
        in_specs=[
            pl.BlockSpec((4, DIFF_HEAD_DIM), lambda bi, hi, qi, ki: (0, 0)),
            pl.BlockSpec((1, DIFF_V_DIM), lambda bi, hi, qi, ki: (0, 0)),
            pl.BlockSpec((1, tq, LANES), lambda bi, hi, qi, ki: (bi, qi, hi)),
            pl.BlockSpec((1, tk, LANES), lambda bi, hi, qi, ki: (bi, kv_block(qi, ki), h + hi)),
            pl.BlockSpec((1, tk, LANES), lambda bi, hi, qi, ki: (bi, kv_block(qi, ki), 2 * h + hi)),
        ],
        out_specs=pl.BlockSpec((1, tq, LANES), lambda bi, hi, qi, ki: (bi, qi, hi)),
        scratch_shapes=[
            pltpu.VMEM((2 * tq, LANES), BF16),
            pltpu.VMEM((2 * tq, 1), F32),
            pltpu.VMEM((2 * tq, 1), F32),
            pltpu.VMEM((2 * tq, DIFF_V_DIM), F32),
        ],
        compiler_params=_cparams(("parallel", "parallel", "parallel", "arbitrary")),
        name="diff_attn",
    )(lam_vecs, subln_gain, qkv, qkv, qkv)


def _proj_res_kernel(a_ref, w_ref, res_ref, o_ref):
    o_ref[...] = res_ref[...] + jnp.dot(a_ref[...], w_ref[...], preferred_element_type=F32)


def _proj_res(a, w, res, *, tm):
    m, k = a.shape
    n = w.shape[1]
    return pl.pallas_call(
        _proj_res_kernel,
        out_shape=jax.ShapeDtypeStruct((m, n), F32),
        grid=(m // tm,),
        in_specs=[
            pl.BlockSpec((tm, k), lambda i: (i, 0)),
            pl.BlockSpec((k, n), lambda i: (0, 0)),
            pl.BlockSpec((tm, n), lambda i: (i, 0)),
        ],
        out_specs=pl.BlockSpec((tm, n), lambda i: (i, 0)),
        compiler_params=_cparams(("parallel",)),
        name="proj_res",
    )(a, w, res)


def _gla_gate_kernel(x_ref, g_ref, w1_ref, w2_ref, b_ref, o_ref):
    xn = _rms(x_ref[...], g_ref[...]).astype(BF16)
    t = jnp.dot(xn, w1_ref[...], preferred_element_type=F32)
    z = jnp.dot(t, w2_ref[...], preferred_element_type=F32, precision=lax.Precision.HIGHEST) + b_ref[...]
    log_sig = jnp.minimum(z, 0.0) - jnp.log1p(jnp.exp(-jnp.abs(z)))
    o_ref[...] = log_sig * (1.0 / GLA_GATE_TEMP)


def _gla_gate(x, gain, w1, w2, bias, *, tm):
    m, d = x.shape
    r = w1.shape[1]
    n = w2.shape[1]
    return pl.pallas_call(
        _gla_gate_kernel,
        out_shape=jax.ShapeDtypeStruct((m, n), F32),
        grid=(m // tm,),
        in_specs=[
            pl.BlockSpec((tm, d), lambda i: (i, 0)),
            pl.BlockSpec((1, d), lambda i: (0, 0)),
            pl.BlockSpec((d, r), lambda i: (0, 0)),
            pl.BlockSpec((r, n), lambda i: (0, 0)),
            pl.BlockSpec((1, n), lambda i: (0, 0)),
        ],
        out_specs=pl.BlockSpec((tm, n), lambda i: (i, 0)),
        compiler_params=_cparams(("parallel",)),
        name="gla_gate",
    )(x, gain, w1, w2, bias)


def _split3(x):
    hi = x.astype(BF16)
    r1 = x - hi.astype(F32)
    mid = r1.astype(BF16)
    lo = (r1 - mid.astype(F32)).astype(BF16)
    return hi, mid, lo


def _gla_kernel(q_ref, k_ref, v_ref, r_ref, g_ref, ng_ref, o_ref, state_ref):
    t_blk = q_ref.shape[1]
    hk = GLA_HEADS * GLA_DK

    @pl.when(pl.program_id(1) == 0)
    def _():
        state_ref[...] = jnp.zeros(state_ref.shape, F32)

    ri = lax.broadcasted_iota(jnp.int32, (t_blk, t_blk), 0)
    ci = lax.broadcasted_iota(jnp.int32, (t_blk, t_blk), 1)

    tri = jnp.where(ci <= ri, 1.0, 0.0).astype(BF16)
    g_hi, g_mid, g_lo = _split3(g_ref[0])
    b = (jnp.dot(tri, g_hi, preferred_element_type=F32)
         + jnp.dot(tri, g_mid, preferred_element_type=F32)
         + jnp.dot(tri, g_lo, preferred_element_type=F32))

    q = q_ref[0].astype(F32) * (GLA_DK ** -0.5)
    k = k_ref[0].astype(F32)

    row = lax.broadcasted_iota(jnp.int32, (t_blk, hk), 0)
    b_last = b[t_blk - 1:t_blk, :]
    q_in = (q * jnp.exp(b)).astype(BF16)
    k_out = (k * jnp.exp(b_last - b)).astype(BF16)
    e_last = jnp.exp(b_last)

    level = t_blk // 2
    levels = []
    while level >= GLA_DIAG:
        grp = 2 * level
        b3 = b.reshape(t_blk // grp, grp, hk)
        pivot = jnp.broadcast_to(b3[:, level - 1:level, :], b3.shape).reshape(t_blk, hk)
        upper = (row % grp) >= level
        e = jnp.exp(jnp.where(upper, b - pivot, pivot - b))
        qt = jnp.where(upper, q * e, 0.0).astype(BF16)
        kt = jnp.where(upper, 0.0, k * e).astype(BF16)
        levels.append((grp, qt, kt))
        level //= 2

    nd = t_blk // GLA_DIAG
    b3 = b.reshape(nd, GLA_DIAG, hk)
    q3 = q.reshape(nd, GLA_DIAG, hk)
    k3 = k.reshape(nd, GLA_DIAG, hk)
    ones = jnp.ones((GLA_DK, t_blk), BF16)
    dcol = ci - (ri // GLA_DIAG) * GLA_DIAG
    rloc = ri % GLA_DIAG

    eye = (lax.broadcasted_iota(jnp.int32, (GLA_DK, GLA_DK), 0)
           == lax.broadcasted_iota(jnp.int32, (GLA_DK, GLA_DK), 1))

    diag_terms = []
    for j in range(GLA_DIAG):
        d = jnp.minimum(b3 - b3[:, j:j + 1, :], 0.0)
        pj = (jnp.exp(d) * q3 * k3[:, j:j + 1, :]).reshape(t_blk, hk)
        diag_terms.append(pj)

    ng = ng_ref[...]
    for h in range(GLA_HEADS):
        ks = slice(h * GLA_DK, (h + 1) * GLA_DK)
        vs = slice(h * GLA_DV, (h + 1) * GLA_DV)
        a = jnp.zeros((t_blk, t_blk), F32)
        for grp, qt, kt in levels:
            s_l = lax.dot_general(qt[:, ks], kt[:, ks], (((1,), (1,)), ((), ())),
                                  preferred_element_type=F32)
            if grp == t_blk:
                a = a + s_l
            else:
                a = a + jnp.where((ri // grp) == (ci // grp), s_l, 0.0)
        for j in range(GLA_DIAG):
            pj = diag_terms[j][:, ks]
            p_hi = pj.astype(BF16)
            p_lo = (pj - p_hi.astype(F32)).astype(BF16)
            r_j = (jnp.dot(p_hi, ones, preferred_element_type=F32)
                   + jnp.dot(p_lo, ones, preferred_element_type=F32))
            a = jnp.where((dcol == j) & (rloc >= j), r_j, a)

        v_h = v_ref[0, :, vs]
        state = state_ref[h]
        o = (jnp.dot(q_in[:, ks], state.astype(BF16), preferred_element_type=F32)
             + jnp.dot(a.astype(BF16), v_h, preferred_element_type=F32))

        e_col = jnp.sum(jnp.where(eye, jnp.broadcast_to(e_last[:, ks], (GLA_DK, GLA_DK)), 0.0),
                        axis=1, keepdims=True)
        upd = lax.dot_general(k_out[:, ks], v_h, (((0,), (0,)), ((), ())),
                              preferred_element_type=F32)
        state_ref[h] = e_col * state + upd

        r_h = r_ref[0, :, vs].astype(F32)
        gate = r_h / (1.0 + jnp.exp(-r_h))
        o_ref[0, :, vs] = (_rms(o, ng) * gate).astype(o_ref.dtype)


def _gla(proj, g, norm_gain):
    b, s, _ = proj.shape
    t = GLA_BLOCK
    hk = GLA_HEADS * GLA_DK
    hv = GLA_HEADS * GLA_DV
    return pl.pallas_call(
        _gla_kernel,
        out_shape=jax.ShapeDtypeStruct((b, s, hv), BF16),
        grid=(b, s // t),
        in_specs=[
            pl.BlockSpec((1, t, hk), lambda bi, ti: (bi, ti, 0)),
            pl.BlockSpec((1, t, hk), lambda bi, ti: (bi, ti, 1)),
            pl.BlockSpec((1, t, hv), lambda bi, ti: (bi, ti, 1)),
            pl.BlockSpec((1, t, hv), lambda bi, ti: (bi, ti, 2)),
            pl.BlockSpec((1, t, hk), lambda bi, ti: (bi, ti, 0)),
            pl.BlockSpec((1, GLA_DV), lambda bi, ti: (0, 0)),
        ],
        out_specs=pl.BlockSpec((1, t, hv), lambda bi, ti: (bi, ti, 0)),
        scratch_shapes=[pltpu.VMEM((GLA_HEADS, GLA_DK, GLA_DV), F32)],
        compiler_params=_cparams(("parallel", "arbitrary")),
        name="gla",
    )(proj, proj, proj, proj, g, norm_gain)


def _ffn_kernel(x_ref, halo_ref, g_ref, wg_ref, wu_ref, cwg_ref, cwu_ref, cbg_ref, cbu_ref, wd_ref,
                fg_ref, o_ref, xn_ref, acc_ref, *, tiles_per_seq, final_norm):
    i = pl.program_id(0)
    j = pl.program_id(1)
    nj = pl.num_programs(1)
    tm = x_ref.shape[0]

    @pl.when(j == 0)
    def _():
        gain = g_ref[...]
        halo = _rms(halo_ref[...], gain)
        halo = jnp.where(i % tiles_per_seq == 0, 0.0, halo)
        xn_ref[0:FFN_HALO, :] = halo.astype(BF16)
        xn_ref[FFN_HALO:FFN_HALO + tm, :] = _rms(x_ref[...], gain).astype(BF16)
        acc_ref[...] = jnp.zeros(acc_ref.shape, F32)

    xn = xn_ref[...]

    def conv(u, cw_ref, cb_ref):
        cw = cw_ref[...]
        u1 = pltpu.roll(u, 1, 0)
        u2 = pltpu.roll(u, 2, 0)
        y = cw[2:3, :] * u + cw[1:2, :] * u1 + cw[0:1, :] * u2 + cb_ref[...]
        return y[FFN_HALO:FFN_HALO + tm, :]

    gate = conv(jnp.dot(xn, wg_ref[...], preferred_element_type=F32), cwg_ref, cbg_ref)
    up = conv(jnp.dot(xn, wu_ref[...], preferred_element_type=F32), cwu_ref, cbu_ref)
    act = (gate / (1.0 + jnp.exp(-gate))) * up
    acc_ref[...] += jnp.dot(act.astype(BF16), wd_ref[...], preferred_element_type=F32)

    @pl.when(j == nj - 1)
    def _():
        y = x_ref[...] + acc_ref[...]
        if final_norm:
            y = _rms(y, fg_ref[...])
        o_ref[...] = y


def _ffn(x, gain, w_up, conv_w, conv_b, w_down, final_gain, *, tm, tf, seq, final_norm):
    m, d = x.shape
    f = w_down.shape[0]
    nf = f // tf
    tiles_per_seq = seq // tm
    halo_blocks = tm // FFN_HALO
    kern = functools.partial(_ffn_kernel, tiles_per_seq=tiles_per_seq, final_norm=final_norm)
    return pl.pallas_call(
        kern,
        out_shape=jax.ShapeDtypeStruct((m, d), F32),
        grid=(m // tm, nf),
        in_specs=[
            pl.BlockSpec((tm, d), lambda i, j: (i, 0)),
            pl.BlockSpec((FFN_HALO, d), lambda i, j: (jnp.maximum(i * halo_blocks - 1, 0), 0)),
            pl.BlockSpec((1, d), lambda i, j: (0, 0)),
            pl.BlockSpec((d, tf), lambda i, j: (0, j)),
            pl.BlockSpec((d, tf), lambda i, j: (0, nf + j)),
            pl.BlockSpec((CONV_WIDTH, tf), lambda i, j: (0, j)),
            pl.BlockSpec((CONV_WIDTH, tf), lambda i, j: (0, nf + j)),
            pl.BlockSpec((1, tf), lambda i, j: (0, j)),
            pl.BlockSpec((1, tf), lambda i, j: (0, nf + j)),
            pl.BlockSpec((tf, d), lambda i, j: (j, 0)),
            pl.BlockSpec((1, d), lambda i, j: (0, 0)),
        ],
        out_specs=pl.BlockSpec((tm, d), lambda i, j: (i, 0)),
        scratch_shapes=[pltpu.VMEM((tm + FFN_HALO, d), BF16), pltpu.VMEM((tm, d), F32)],
        compiler_params=_cparams(("parallel", "arbitrary")),
        name="ffn",
    )(x, x, gain, w_up, w_up, conv_w, conv_w, conv_b, conv_b, w_down, final_gain)


def _rope_tables(seq):
    half = DIFF_HEAD_DIM // 2
    inv = 1.0 / (ROPE_THETA ** (jnp.arange(0, DIFF_HEAD_DIM, 2, dtype=F32) / DIFF_HEAD_DIM))
    ang = jnp.arange(seq, dtype=F32)[:, None] * inv[None, :]
    cos, sin = jnp.cos(ang), jnp.sin(ang)
    reps = LANES // half
    cos_t = jnp.tile(cos, (1, reps))
    sin_t = jnp.tile(jnp.concatenate([-sin, sin], axis=1), (1, reps // 2))
    return cos_t, sin_t


def _trunk(x, norm_mix, norm_ffn, norm_final, diff_w_qkv, diff_w_o, diff_lambda, diff_subln,
           gla_w_in, gla_w_a1, gla_w_a2, gla_b_a, gla_norm, gla_w_o,
           ffn_w_up, ffn_conv_w, ffn_conv_b, ffn_w_down, *, tm, tq, tf):
    bsz, seq, d = x.shape
    depth = norm_mix.shape[0]
    m = bsz * seq
    cos_t, sin_t = _rope_tables(seq)
    h = x.reshape(m, d)
    rank = gla_w_a1.shape[-1]
    for layer in range(depth):
        jdx = layer // 2
        gain = norm_mix[layer].reshape(1, d)
        if layer % 2 == 0:
            lambda_init = 0.8 - 0.6 * math.exp(-0.3 * layer)
            qkv = _norm_proj(h, gain, diff_w_qkv[jdx].astype(BF16), cos_t, sin_t, tm=tm, tn=d,
                             n_rope_blocks=2, q_scale=DIFF_HEAD_DIM ** -0.5, seq=seq)
            o = _diff_attention(qkv.reshape(bsz, seq, 3 * d), diff_lambda[jdx],
                                diff_subln[jdx].reshape(1, DIFF_V_DIM),
                                lambda_init=lambda_init, tq=tq, tk=tq)
            h = _proj_res(o.reshape(m, d), diff_w_o[jdx].astype(BF16), h, tm=tm)
        else:
            proj = _norm_proj(h, gain, gla_w_in[jdx].astype(BF16), cos_t, sin_t, tm=tm, tn=d,
                              n_rope_blocks=0, q_scale=1.0, seq=seq)
            w1 = jnp.pad(gla_w_a1[jdx], ((0, 0), (0, LANES - rank))).astype(BF16)
            w2 = jnp.pad(gla_w_a2[jdx], ((0, LANES - rank), (0, 0)))
            g = _gla_gate(h, gain, w1, w2, gla_b_a[jdx].reshape(1, -1), tm=tm)
            o = _gla(proj.reshape(bsz, seq, 3 * d), g.reshape(bsz, seq, -1),
                     gla_norm[jdx].reshape(1, GLA_DV))
            h = _proj_res(o.reshape(m, d), gla_w_o[jdx].astype(BF16), h, tm=tm)
        h = _ffn(h, norm_ffn[layer].reshape(1, d), ffn_w_up[layer].astype(BF16), ffn_conv_w[layer],
                 ffn_conv_b[layer].reshape(1, -1), ffn_w_down[layer].astype(BF16),
                 norm_final.reshape(1, d), tm=tm, tf=tf, seq=seq, final_norm=(layer == depth - 1))
    return h.reshape(bsz, seq, d)


def kernel(x, norm_mix, norm_ffn, norm_final, diff_w_qkv, diff_w_o, diff_lambda, diff_subln, gla_w_in, gla_w_a1, gla_w_a2, gla_b_a, gla_norm, gla_w_o, ffn_w_up, ffn_conv_w, ffn_conv_b, ffn_w_down):
    return _trunk(x, norm_mix, norm_ffn, norm_final, diff_w_qkv, diff_w_o, diff_lambda, diff_subln,
                  gla_w_in, gla_w_a1, gla_w_a2, gla_b_a, gla_norm, gla_w_o,
                  ffn_w_up, ffn_conv_w, ffn_conv_b, ffn_w_down, tm=1024, tq=512, tf=256)
```

```python
import functools
import math

import jax
import jax.numpy as jnp
from jax import lax
from jax.experimental import pallas as pl
from jax.experimental.pallas import tpu as pltpu

F32 = jnp.float32
BF16 = jnp.bfloat16

NORM_EPS = 1e-6
ROPE_THETA = 10000.0

DIFF_HEADS = 8
DIFF_HEAD_DIM = 64
DIFF_V_DIM = 2 * DIFF_HEAD_DIM

GLA_HEADS = 4
GLA_DK = 128
GLA_DV = 256
GLA_GATE_TEMP = 16.0
GLA_BLOCK = 256
GLA_DIAG = 8

CONV_WIDTH = 3
FFN_HALO = 16

LANES = 128
VMEM_LIMIT_BYTES = 56 * 1024 * 1024


def _cparams(semantics):
    return pltpu.CompilerParams(dimension_semantics=semantics, vmem_limit_bytes=VMEM_LIMIT_BYTES)


def _rms(x, gain):
    ms = jnp.mean(x * x, axis=-1, keepdims=True)
    return x * lax.rsqrt(ms + NORM_EPS) * gain


def _norm_proj_kernel(x_ref, g_ref, w_ref, cos_ref, sin_ref, o_ref, xn_ref, *, n_rope_blocks, q_scale):
    j = pl.program_id(1)

    @pl.when(j == 0)
    def _():
        xn_ref[...] = _rms(x_ref[...], g_ref[...]).astype(BF16)

    acc = jnp.dot(xn_ref[...], w_ref[...], preferred_element_type=F32)
    tm, tn = acc.shape

    if n_rope_blocks == 0:
        o_ref[...] = acc.astype(o_ref.dtype)
        return

    @pl.when(j >= n_rope_blocks)
    def _():
        o_ref[...] = acc.astype(o_ref.dtype)

    @pl.when(j < n_rope_blocks)
    def _():
        c = cos_ref[...]
        s = sin_ref[...]
        scale = jnp.where(j == 0, q_scale, 1.0).astype(F32)
        lane = lax.broadcasted_iota(jnp.int32, (tm, LANES), 1)
        first_half = (lane % DIFF_HEAD_DIM) < (DIFF_HEAD_DIM // 2)
        for hh in range(tn // LANES):
            t = acc[:, hh * LANES:(hh + 1) * LANES]
            partner = jnp.where(first_half,
                                pltpu.roll(t, LANES - DIFF_HEAD_DIM // 2, 1),
                                pltpu.roll(t, DIFF_HEAD_DIM // 2, 1))
            o_ref[:, hh * LANES:(hh + 1) * LANES] = ((t * c + partner * s) * scale).astype(o_ref.dtype)


def _norm_proj(x, gain, w, cos_t, sin_t, *, tm, tn, n_rope_blocks, q_scale, seq):
    m, d = x.shape
    n = w.shape[1]
    nseq = seq // tm
    kern = functools.partial(_norm_proj_kernel, n_rope_blocks=n_rope_blocks, q_scale=q_scale)
    return pl.pallas_call(
        kern,
        out_shape=jax.ShapeDtypeStruct((m, n), BF16),
        grid=(m // tm, n // tn),
        in_specs=[
            pl.BlockSpec((tm, d), lambda i, j: (i, 0)),
            pl.BlockSpec((1, d), lambda i, j: (0, 0)),
            pl.BlockSpec((d, tn), lambda i, j: (0, j)),
            pl.BlockSpec((tm, LANES), lambda i, j: (i % nseq, 0)),
            pl.BlockSpec((tm, LANES), lambda i, j: (i % nseq, 0)),
        ],
        out_specs=pl.BlockSpec((tm, tn), lambda i, j: (i, j)),
        scratch_shapes=[pltpu.VMEM((tm, d), BF16)],
        compiler_params=_cparams(("parallel", "arbitrary")),
        name="norm_proj",
    )(x, gain, w, cos_t, sin_t)


ATTN_COL_GROUP = 256


def _diff_attn_kernel(lam_ref, sg_ref, q_ref, k_ref, v_ref, o_ref, vt_ref, acc_ref, *, tq, lambda_init):
    qi = pl.program_id(2)
    n_kv_blocks = vt_ref.shape[0]
    n2 = 2 * tq

    @pl.when(qi == 0)
    def _():
        for c in range(n_kv_blocks):
            vt_ref[c] = v_ref[0, c * tq:(c + 1) * tq, :].astype(F32).T.astype(BF16)

    q = q_ref[0]
    lane = lax.broadcasted_iota(jnp.int32, q.shape, 1)
    zero = jnp.zeros_like(q)
    qq = jnp.concatenate([jnp.where(lane < DIFF_HEAD_DIM, q, zero),
                          jnp.where(lane >= DIFF_HEAD_DIM, q, zero)], axis=0)
    acc_ref[...] = jnp.zeros(acc_ref.shape, F32)

    def block(j, carry, masked):
        m, l = carry
        start = pl.multiple_of(j * tq, tq)
        k_blk = k_ref[0, pl.ds(start, tq), :]
        vt_blk = vt_ref[j]
        m_out, l_out = [], []
        for g in range(n2 // ATTN_COL_GROUP):
            cs = slice(g * ATTN_COL_GROUP, (g + 1) * ATTN_COL_GROUP)
            s = lax.dot_general(k_blk, qq[cs, :], (((1,), (1,)), ((), ())),
                                preferred_element_type=F32)
            if masked:
                kv_pos = lax.broadcasted_iota(jnp.int32, s.shape, 0)
                q_pos = (lax.broadcasted_iota(jnp.int32, s.shape, 1) + g * ATTN_COL_GROUP) % tq
                s = jnp.where(kv_pos <= q_pos, s, -jnp.inf)
            m_g = m[:, cs]
            m_new = jnp.maximum(m_g, jnp.max(s, axis=0, keepdims=True))
            alpha = jnp.exp2(m_g - m_new)
            p = jnp.exp2(s - m_new)
            l_out.append(alpha * l[:, cs] + jnp.sum(p, axis=0, keepdims=True))
            m_out.append(m_new)
            acc_ref[:, cs] = alpha * acc_ref[:, cs] + jnp.dot(vt_blk, p.astype(BF16),
                                                              preferred_element_type=F32)
        return jnp.concatenate(m_out, axis=1), jnp.concatenate(l_out, axis=1)

    init = (jnp.full((1, n2), -jnp.inf, F32), jnp.zeros((1, n2), F32))
    carry = lax.fori_loop(0, qi, functools.partial(block, masked=False), init)
    _, l = block(qi, carry, masked=True)

    lv = lam_ref[...]
    lam = (jnp.exp(jnp.sum(lv[0:1] * lv[1:2], axis=1, keepdims=True))
           - jnp.exp(jnp.sum(lv[2:3] * lv[3:4], axis=1, keepdims=True)) + lambda_init)
    o1 = acc_ref[:, 0:tq] / l[:, 0:tq]
    o2 = acc_ref[:, tq:n2] / l[:, tq:n2]
    o = o1 - lam * o2
    ms = jnp.mean(o * o, axis=0, keepdims=True)
    o = o * lax.rsqrt(ms + NORM_EPS) * sg_ref[...] * (1.0 - lambda_init)
    o_ref[0] = o.T.astype(o_ref.dtype)


def _diff_attention(qkv, lam_vecs, subln_gain, *, lambda_init, tq):
    b, s, _ = qkv.shape
    h = DIFF_HEADS
    kern = functools.partial(_diff_attn_kernel, tq=tq, lambda_init=lambda_init)
    return pl.pallas_call(
        kern,
        out_shape=jax.ShapeDtypeStruct((b, s, h * DIFF_V_DIM), BF16),
        grid=(b, h, s // tq),
        in_specs=[
            pl.BlockSpec((4, DIFF_HEAD_DIM), lambda bi, hi, qi: (0, 0)),
            pl.BlockSpec((DIFF_V_DIM, 1), lambda bi, hi, qi: (0, 0)),
            pl.BlockSpec((1, tq, LANES), lambda bi, hi, qi: (bi, qi, hi)),
            pl.BlockSpec((1, s, LANES), lambda bi, hi, qi: (bi, 0, h + hi)),
            pl.BlockSpec((1, s, LANES), lambda bi, hi, qi: (bi, 0, 2 * h + hi)),
        ],
        out_specs=pl.BlockSpec((1, tq, LANES), lambda bi, hi, qi: (bi, qi, hi)),
        scratch_shapes=[
            pltpu.VMEM((s // tq, DIFF_V_DIM, tq), BF16),
            pltpu.VMEM((DIFF_V_DIM, 2 * tq), F32),
        ],
        compiler_params=_cparams(("parallel", "parallel", "arbitrary")),
        name="diff_attn",
    )(lam_vecs, subln_gain, qkv, qkv, qkv)


def _proj_res_kernel(a_ref, w_ref, res_ref, o_ref):
    o_ref[...] = res_ref[...] + jnp.dot(a_ref[...], w_ref[...], preferred_element_type=F32)


def _proj_res(a, w, res, *, tm):
    m, k = a.shape
    n = w.shape[1]
    return pl.pallas_call(
        _proj_res_kernel,
        out_shape=jax.ShapeDtypeStruct((m, n), F32),
        grid=(m // tm,),
        in_specs=[
            pl.BlockSpec((tm, k), lambda i: (i, 0)),
            pl.BlockSpec((k, n), lambda i: (0, 0)),
            pl.BlockSpec((tm, n), lambda i: (i, 0)),
        ],
        out_specs=pl.BlockSpec((tm, n), lambda i: (i, 0)),
        compiler_params=_cparams(("parallel",)),
        name="proj_res",
    )(a, w, res)


def _gla_gate_kernel(x_ref, g_ref, w1_ref, w2_ref, b_ref, o_ref):
    xn = _rms(x_ref[...], g_ref[...]).astype(BF16)
    t = jnp.dot(xn, w1_ref[...], preferred_element_type=F32)
    z = jnp.dot(t, w2_ref[...], preferred_element_type=F32, precision=lax.Precision.HIGHEST) + b_ref[...]
    log_sig = jnp.minimum(z, 0.0) - jnp.log1p(jnp.exp(-jnp.abs(z)))
    o_ref[...] = log_sig * (1.0 / GLA_GATE_TEMP)


def _gla_gate(x, gain, w1, w2, bias, *, tm):
    m, d = x.shape
    r = w1.shape[1]
    n = w2.shape[1]
    return pl.pallas_call(
        _gla_gate_kernel,
        out_shape=jax.ShapeDtypeStruct((m, n), F32),
        grid=(m // tm,),
        in_specs=[
            pl.BlockSpec((tm, d), lambda i: (i, 0)),
            pl.BlockSpec((1, d), lambda i: (0, 0)),
            pl.BlockSpec((d, r), lambda i: (0, 0)),
            pl.BlockSpec((r, n), lambda i: (0, 0)),
            pl.BlockSpec((1, n), lambda i: (0, 0)),
        ],
        out_specs=pl.BlockSpec((tm, n), lambda i: (i, 0)),
        compiler_params=_cparams(("parallel",)),
        name="gla_gate",
    )(x, gain, w1, w2, bias)


def _split3(x):
    hi = x.astype(BF16)
    r1 = x - hi.astype(F32)
    mid = r1.astype(BF16)
    lo = (r1 - mid.astype(F32)).astype(BF16)
    return hi, mid, lo


def _gla_kernel(q_ref, k_ref, v_ref, r_ref, g_ref, ng_ref, o_ref, state_ref):
    t_blk = q_ref.shape[1]
    hk = GLA_HEADS * GLA_DK

    @pl.when(pl.program_id(1) == 0)
    def _():
        state_ref[...] = jnp.zeros(state_ref.shape, F32)

    ri = lax.broadcasted_iota(jnp.int32, (t_blk, t_blk), 0)
    ci = lax.broadcasted_iota(jnp.int32, (t_blk, t_blk), 1)

    tri = jnp.where(ci <= ri, 1.0, 0.0).astype(BF16)
    g_hi, g_mid, g_lo = _split3(g_ref[0])
    b = (jnp.dot(tri, g_hi, preferred_element_type=F32)
         + jnp.dot(tri, g_mid, preferred_element_type=F32)
         + jnp.dot(tri, g_lo, preferred_element_type=F32))

    q = q_ref[0].astype(F32) * (GLA_DK ** -0.5)
    k = k_ref[0].astype(F32)

    row = lax.broadcasted_iota(jnp.int32, (t_blk, hk), 0)
    b_last = b[t_blk - 1:t_blk, :]
    q_in = (q * jnp.exp(b)).astype(BF16)
    k_out = (k * jnp.exp(b_last - b)).astype(BF16)
    e_last = jnp.exp(b_last)

    level = t_blk // 2
    levels = []
    while level >= GLA_DIAG:
        grp = 2 * level
        b3 = b.reshape(t_blk // grp, grp, hk)
        pivot = jnp.broadcast_to(b3[:, level - 1:level, :], b3.shape).reshape(t_blk, hk)
        upper = (row % grp) >= level
        e = jnp.exp(jnp.where(upper, b - pivot, pivot - b))
        qt = jnp.where(upper, q * e, 0.0).astype(BF16)
        kt = jnp.where(upper, 0.0, k * e).astype(BF16)
        levels.append((grp, qt, kt))
        level //= 2

    nd = t_blk // GLA_DIAG
    b3 = b.reshape(nd, GLA_DIAG, hk)
    q3 = q.reshape(nd, GLA_DIAG, hk)
    k3 = k.reshape(nd, GLA_DIAG, hk)
    ones = jnp.ones((GLA_DK, t_blk), BF16)
    dcol = ci - (ri // GLA_DIAG) * GLA_DIAG
    rloc = ri % GLA_DIAG

    eye = (lax.broadcasted_iota(jnp.int32, (GLA_DK, GLA_DK), 0)
           == lax.broadcasted_iota(jnp.int32, (GLA_DK, GLA_DK), 1))

    diag_terms = []
    for j in range(GLA_DIAG):
        d = jnp.minimum(b3 - b3[:, j:j + 1, :], 0.0)
        pj = (jnp.exp(d) * q3 * k3[:, j:j + 1, :]).reshape(t_blk, hk)
        diag_terms.append(pj)

    ng = ng_ref[...]
    for h in range(GLA_HEADS):
        ks = slice(h * GLA_DK, (h + 1) * GLA_DK)
        vs = slice(h * GLA_DV, (h + 1) * GLA_DV)
        a = jnp.zeros((t_blk, t_blk), F32)
        for grp, qt, kt in levels:
            s_l = lax.dot_general(qt[:, ks], kt[:, ks], (((1,), (1,)), ((), ())),
                                  preferred_element_type=F32)
            if grp == t_blk:
                a = a + s_l
            else:
                a = a + jnp.where((ri // grp) == (ci // grp), s_l, 0.0)
        for j in range(GLA_DIAG):
            pj = diag_terms[j][:, ks]
            p_hi = pj.astype(BF16)
            p_lo = (pj - p_hi.astype(F32)).astype(BF16)
            r_j = (jnp.dot(p_hi, ones, preferred_element_type=F32)
                   + jnp.dot(p_lo, ones, preferred_element_type=F32))
            a = jnp.where((dcol == j) & (rloc >= j), r_j, a)

        v_h = v_ref[0, :, vs]
        state = state_ref[h]
        o = (jnp.dot(q_in[:, ks], state.astype(BF16), preferred_element_type=F32)
             + jnp.dot(a.astype(BF16), v_h, preferred_element_type=F32))

        e_col = jnp.sum(jnp.where(eye, jnp.broadcast_to(e_last[:, ks], (GLA_DK, GLA_DK)), 0.0),
                        axis=1, keepdims=True)
        upd = lax.dot_general(k_out[:, ks], v_h, (((0,), (0,)), ((), ())),
                              preferred_element_type=F32)
        state_ref[h] = e_col * state + upd

        r_h = r_ref[0, :, vs].astype(F32)
        gate = r_h / (1.0 + jnp.exp(-r_h))
        o_ref[0, :, vs] = (_rms(o, ng) * gate).astype(o_ref.dtype)


def _gla(proj, g, norm_gain):
    b, s, _ = proj.shape
    t = GLA_BLOCK
    hk = GLA_HEADS * GLA_DK
    hv = GLA_HEADS * GLA_DV
    return pl.pallas_call(
        _gla_kernel,
        out_shape=jax.ShapeDtypeStruct((b, s, hv), BF16),
        grid=(b, s // t),
        in_specs=[
            pl.BlockSpec((1, t, hk), lambda bi, ti: (bi, ti, 0)),
            pl.BlockSpec((1, t, hk), lambda bi, ti: (bi, ti, 1)),
            pl.BlockSpec((1, t, hv), lambda bi, ti: (bi, ti, 1)),
            pl.BlockSpec((1, t, hv), lambda bi, ti: (bi, ti, 2)),
            pl.BlockSpec((1, t, hk), lambda bi, ti: (bi, ti, 0)),
            pl.BlockSpec((1, GLA_DV), lambda bi, ti: (0, 0)),
        ],
        out_specs=pl.BlockSpec((1, t, hv), lambda bi, ti: (bi, ti, 0)),
        scratch_shapes=[pltpu.VMEM((GLA_HEADS, GLA_DK, GLA_DV), F32)],
        compiler_params=_cparams(("parallel", "arbitrary")),
        name="gla",
    )(proj, proj, proj, proj, g, norm_gain)


def _ffn_kernel(x_ref, halo_ref, g_ref, wg_ref, wu_ref, cwg_ref, cwu_ref, cbg_ref, cbu_ref, wd_ref,
                fg_ref, o_ref, xn_ref, acc_ref, *, tiles_per_seq, final_norm):
    i = pl.program_id(0)
    j = pl.program_id(1)
    nj = pl.num_programs(1)
    tm = x_ref.shape[0]

    @pl.when(j == 0)
    def _():
        gain = g_ref[...]
        halo = _rms(halo_ref[...], gain)
        halo = jnp.where(i % tiles_per_seq == 0, 0.0, halo)
        xn_ref[0:FFN_HALO, :] = halo.astype(BF16)
        xn_ref[FFN_HALO:FFN_HALO + tm, :] = _rms(x_ref[...], gain).astype(BF16)
        acc_ref[...] = jnp.zeros(acc_ref.shape, F32)

    xn = xn_ref[...]

    def conv(u, cw_ref, cb_ref):
        cw = cw_ref[...]
        u1 = pltpu.roll(u, 1, 0)
        u2 = pltpu.roll(u, 2, 0)
        y = cw[2:3, :] * u + cw[1:2, :] * u1 + cw[0:1, :] * u2 + cb_ref[...]
        return y[FFN_HALO:FFN_HALO + tm, :]

    gate = conv(jnp.dot(xn, wg_ref[...], preferred_element_type=F32), cwg_ref, cbg_ref)
    up = conv(jnp.dot(xn, wu_ref[...], preferred_element_type=F32), cwu_ref, cbu_ref)
    act = (gate / (1.0 + jnp.exp(-gate))) * up
    acc_ref[...] += jnp.dot(act.astype(BF16), wd_ref[...], preferred_element_type=F32)

    @pl.when(j == nj - 1)
    def _():
        y = x_ref[...] + acc_ref[...]
        if final_norm:
            y = _rms(y, fg_ref[...])
        o_ref[...] = y


def _ffn(x, gain, w_up, conv_w, conv_b, w_down, final_gain, *, tm, tf, seq, final_norm):
    m, d = x.shape
    f = w_down.shape[0]
    nf = f // tf
    tiles_per_seq = seq // tm
    halo_blocks = tm // FFN_HALO
    kern = functools.partial(_ffn_kernel, tiles_per_seq=tiles_per_seq, final_norm=final_norm)
    return pl.pallas_call(
        kern,
        out_shape=jax.ShapeDtypeStruct((m, d), F32),
        grid=(m // tm, nf),
        in_specs=[
            pl.BlockSpec((tm, d), lambda i, j: (i, 0)),
            pl.BlockSpec((FFN_HALO, d), lambda i, j: (jnp.maximum(i * halo_blocks - 1, 0), 0)),
            pl.BlockSpec((1, d), lambda i, j: (0, 0)),
            pl.BlockSpec((d, tf), lambda i, j: (0, j)),
            pl.BlockSpec((d, tf), lambda i, j: (0, nf + j)),
            pl.BlockSpec((CONV_WIDTH, tf), lambda i, j: (0, j)),
            pl.BlockSpec((CONV_WIDTH, tf), lambda i, j: (0, nf + j)),
            pl.BlockSpec((1, tf), lambda i, j: (0, j)),
            pl.BlockSpec((1, tf), lambda i, j: (0, nf + j)),
            pl.BlockSpec((tf, d), lambda i, j: (j, 0)),
            pl.BlockSpec((1, d), lambda i, j: (0, 0)),
        ],
        out_specs=pl.BlockSpec((tm, d), lambda i, j: (i, 0)),
        scratch_shapes=[pltpu.VMEM((tm + FFN_HALO, d), BF16), pltpu.VMEM((tm, d), F32)],
        compiler_params=_cparams(("parallel", "arbitrary")),
        name="ffn",
    )(x, x, gain, w_up, w_up, conv_w, conv_w, conv_b, conv_b, w_down, final_gain)


def _rope_tables(seq):
    half = DIFF_HEAD_DIM // 2
    inv = 1.0 / (ROPE_THETA ** (jnp.arange(0, DIFF_HEAD_DIM, 2, dtype=F32) / DIFF_HEAD_DIM))
    ang = jnp.arange(seq, dtype=F32)[:, None] * inv[None, :]
    cos, sin = jnp.cos(ang), jnp.sin(ang)
    reps = LANES // half
    cos_t = jnp.tile(cos, (1, reps))
    sin_t = jnp.tile(jnp.concatenate([-sin, sin], axis=1), (1, reps // 2))
    return cos_t, sin_t


def _trunk(x, norm_mix, norm_ffn, norm_final, diff_w_qkv, diff_w_o, diff_lambda, diff_subln,
           gla_w_in, gla_w_a1, gla_w_a2, gla_b_a, gla_norm, gla_w_o,
           ffn_w_up, ffn_conv_w, ffn_conv_b, ffn_w_down, *, tm, tq, tf):
    bsz, seq, d = x.shape
    depth = norm_mix.shape[0]
    m = bsz * seq
    cos_t, sin_t = _rope_tables(seq)
    h = x.reshape(m, d)
    rank = gla_w_a1.shape[-1]
    for layer in range(depth):
        jdx = layer // 2
        gain = norm_mix[layer].reshape(1, d)
        if layer % 2 == 0:
            lambda_init = 0.8 - 0.6 * math.exp(-0.3 * layer)
            qkv = _norm_proj(h, gain, diff_w_qkv[jdx].astype(BF16), cos_t, sin_t, tm=tm, tn=d,
                             n_rope_blocks=2, q_scale=DIFF_HEAD_DIM ** -0.5 * math.log2(math.e), seq=seq)
            o = _diff_attention(qkv.reshape(bsz, seq, 3 * d), diff_lambda[jdx],
                                diff_subln[jdx].reshape(DIFF_V_DIM, 1),
                                lambda_init=lambda_init, tq=tq)
            h = _proj_res(o.reshape(m, d), diff_w_o[jdx].astype(BF16), h, tm=tm)
        else:
            proj = _norm_proj(h, gain, gla_w_in[jdx].astype(BF16), cos_t, sin_t, tm=tm, tn=d,
                              n_rope_blocks=0, q_scale=1.0, seq=seq)
            w1 = jnp.pad(gla_w_a1[jdx], ((0, 0), (0, LANES - rank))).astype(BF16)
            w2 = jnp.pad(gla_w_a2[jdx], ((0, LANES - rank), (0, 0)))
            g = _gla_gate(h, gain, w1, w2, gla_b_a[jdx].reshape(1, -1), tm=tm)
            o = _gla(proj.reshape(bsz, seq, 3 * d), g.reshape(bsz, seq, -1),
                     gla_norm[jdx].reshape(1, GLA_DV))
            h = _proj_res(o.reshape(m, d), gla_w_o[jdx].astype(BF16), h, tm=tm)
        h = _ffn(h, norm_ffn[layer].reshape(1, d), ffn_w_up[layer].astype(BF16), ffn_conv_w[layer],
                 ffn_conv_b[layer].reshape(1, -1), ffn_w_down[layer].astype(BF16),
                 norm_final.reshape(1, d), tm=tm, tf=tf, seq=seq, final_norm=(layer == depth - 1))
    return h.reshape(bsz, seq, d)


def kernel(x, norm_mix, norm_ffn, norm_final, diff_w_qkv, diff_w_o, diff_lambda, diff_subln, gla_w_in, gla_w_a1, gla_w_a2, gla_b_a, gla_norm, gla_w_o, ffn_w_up, ffn_conv_w, ffn_conv_b, ffn_w_down):
    return _trunk(x, norm_mix, norm_ffn, norm_final, diff_w_qkv, diff_w_o, diff_lambda, diff_subln,
                  gla_w_in, gla_w_a1, gla_w_a2, gla_b_a, gla_norm, gla_w_o,
                  ffn_w_up, ffn_conv_w, ffn_conv_b, ffn_w_down, tm=1024, tq=512, tf=256)
```

```python
import functools
import math

import jax
import jax.numpy as jnp
from jax import lax
from jax.experimental import pallas as pl
from jax.experimental.pallas import tpu as pltpu

F32 = jnp.float32
BF16 = jnp.bfloat16

NORM_EPS = 1e-6
ROPE_THETA = 10000.0

DIFF_HEADS = 8
DIFF_HEAD_DIM = 64
DIFF_V_DIM = 2 * DIFF_HEAD_DIM

GLA_HEADS = 4
GLA_DK = 128
GLA_DV = 256
GLA_GATE_TEMP = 16.0
GLA_BLOCK = 256
GLA_DIAG = 8

CONV_WIDTH = 3
FFN_HALO = 16

LANES = 128
VMEM_LIMIT_BYTES = 56 * 1024 * 1024


def _cparams(semantics):
    return pltpu.CompilerParams(dimension_semantics=semantics, vmem_limit_bytes=VMEM_LIMIT_BYTES)


def _rms(x, gain):
    ms = jnp.mean(x * x, axis=-1, keepdims=True)
    return x * lax.rsqrt(ms + NORM_EPS) * gain


def _norm_proj_kernel(x_ref, g_ref, w_ref, cos_ref, sin_ref, o_ref, xn_ref, *, n_rope_blocks, q_scale):
    j = pl.program_id(1)

    @pl.when(j == 0)
    def _():
        xn_ref[...] = _rms(x_ref[...], g_ref[...]).astype(BF16)

    acc = jnp.dot(xn_ref[...], w_ref[...], preferred_element_type=F32)
    tm, tn = acc.shape

    if n_rope_blocks == 0:
        o_ref[...] = acc.astype(o_ref.dtype)
        return

    @pl.when(j >= n_rope_blocks)
    def _():
        o_ref[...] = acc.astype(o_ref.dtype)

    @pl.when(j < n_rope_blocks)
    def _():
        c = cos_ref[...]
        s = sin_ref[...]
        scale = jnp.where(j == 0, q_scale, 1.0).astype(F32)
        lane = lax.broadcasted_iota(jnp.int32, (tm, LANES), 1)
        first_half = (lane % DIFF_HEAD_DIM) < (DIFF_HEAD_DIM // 2)
        for hh in range(tn // LANES):
            t = acc[:, hh * LANES:(hh + 1) * LANES]
            partner = jnp.where(first_half,
                                pltpu.roll(t, LANES - DIFF_HEAD_DIM // 2, 1),
                                pltpu.roll(t, DIFF_HEAD_DIM // 2, 1))
            o_ref[:, hh * LANES:(hh + 1) * LANES] = ((t * c + partner * s) * scale).astype(o_ref.dtype)


def _norm_proj(x, gain, w, cos_t, sin_t, *, tm, tn, n_rope_blocks, q_scale, seq):
    m, d = x.shape
    n = w.shape[1]
    nseq = seq // tm
    kern = functools.partial(_norm_proj_kernel, n_rope_blocks=n_rope_blocks, q_scale=q_scale)
    return pl.pallas_call(
        kern,
        out_shape=jax.ShapeDtypeStruct((m, n), BF16),
        grid=(m // tm, n // tn),
        in_specs=[
            pl.BlockSpec((tm, d), lambda i, j: (i, 0)),
            pl.BlockSpec((1, d), lambda i, j: (0, 0)),
            pl.BlockSpec((d, tn), lambda i, j: (0, j)),
            pl.BlockSpec((tm, LANES), lambda i, j: (i % nseq, 0)),
            pl.BlockSpec((tm, LANES), lambda i, j: (i % nseq, 0)),
        ],
        out_specs=pl.BlockSpec((tm, tn), lambda i, j: (i, j)),
        scratch_shapes=[pltpu.VMEM((tm, d), BF16)],
        compiler_params=_cparams(("parallel", "arbitrary")),
        name="norm_proj",
    )(x, gain, w, cos_t, sin_t)


ATTN_CHUNK = 256
ATTN_HEADS_PER_STEP = 2


def _diff_attn_kernel(lam_ref, sg_ref, q_ref, k_ref, v_ref, o_ref, vt_ref, s_ref, acc_ref, *, tq, lambda_init):
    qi = pl.program_id(2)
    n_kv_blocks = vt_ref.shape[1]
    n2 = 2 * tq
    ch = min(ATTN_CHUNK, tq)
    nch = tq // ch
    heads = range(ATTN_HEADS_PER_STEP)
    lanes = [slice(hh * LANES, (hh + 1) * LANES) for hh in heads]

    @pl.when(qi == 0)
    def _():
        for hh in heads:
            for c in range(n_kv_blocks):
                vt_ref[hh, c] = v_ref[0, c * tq:(c + 1) * tq, lanes[hh]].astype(F32).T.astype(BF16)

    qqt = []
    for hh in heads:
        qt = q_ref[0, :, lanes[hh]].astype(F32).T
        feat = lax.broadcasted_iota(jnp.int32, qt.shape, 0)
        qqt.append(jnp.concatenate([jnp.where(feat < DIFF_HEAD_DIM, qt, 0.0),
                                    jnp.where(feat >= DIFF_HEAD_DIM, qt, 0.0)], axis=1).astype(BF16))
    acc_ref[...] = jnp.zeros(acc_ref.shape, F32)

    def scores(hh, blk, slot, masked):
        cmx8 = None
        for c in range(nch):
            start = pl.multiple_of(blk * tq + c * ch, ch)
            s = jnp.dot(k_ref[0, pl.ds(start, ch), lanes[hh]], qqt[hh], preferred_element_type=F32)
            if masked:
                kv_pos = lax.broadcasted_iota(jnp.int32, s.shape, 0) + c * ch
                col = lax.broadcasted_iota(jnp.int32, s.shape, 1)
                q_pos = jnp.where(col >= tq, col - tq, col)
                s = jnp.where(kv_pos <= q_pos, s, -jnp.inf)
            s_ref[hh, slot, c * ch:(c + 1) * ch, :] = s
            part = jnp.max(s.reshape(ch // 8, 8, n2), axis=0)
            cmx8 = part if cmx8 is None else jnp.maximum(cmx8, part)
        return cmx8

    def softmax_pv(hh, blk, slot, m, l8, cmx8):
        m_new = jnp.maximum(m, jnp.max(cmx8, axis=0, keepdims=True))
        alpha = jnp.exp2(m - m_new)
        l8 = alpha * l8
        pv = None
        for c in range(nch):
            p = jnp.exp2(s_ref[hh, slot, c * ch:(c + 1) * ch, :] - m_new)
            l8 = l8 + jnp.sum(p.reshape(ch // 8, 8, n2), axis=0)
            d = jnp.dot(vt_ref[hh, blk, :, c * ch:(c + 1) * ch], p.astype(BF16), preferred_element_type=F32)
            pv = d if pv is None else pv + d
        acc_ref[hh] = alpha * acc_ref[hh] + pv
        return m_new, l8

    cmx8_0 = [scores(hh, qi, 0, True) for hh in heads]

    def body(t, carry):
        slot = lax.rem(t, 2)
        stats = [softmax_pv(hh, jnp.where(t == 0, qi, t - 1), slot, *carry[hh]) for hh in heads]
        return tuple(stats[hh] + (scores(hh, t, 1 - slot, False),) for hh in heads)

    init = tuple((jnp.full((1, n2), -jnp.inf, F32), jnp.zeros((8, n2), F32), cmx8_0[hh]) for hh in heads)
    carry = lax.fori_loop(0, qi, body, init)

    lv = lam_ref[...]
    lam = (jnp.exp(jnp.sum(lv[0:1] * lv[1:2], axis=1, keepdims=True))
           - jnp.exp(jnp.sum(lv[2:3] * lv[3:4], axis=1, keepdims=True)) + lambda_init)
    last = [softmax_pv(hh, jnp.maximum(qi - 1, 0), lax.rem(qi, 2), *carry[hh]) for hh in heads]
    for hh in heads:
        l = jnp.sum(last[hh][1], axis=0, keepdims=True)
        o1 = acc_ref[hh, :, 0:tq] / l[:, 0:tq]
        o2 = acc_ref[hh, :, tq:n2] / l[:, tq:n2]
        o = o1 - lam * o2
        ms = jnp.mean(o * o, axis=0, keepdims=True)
        o = o * lax.rsqrt(ms + NORM_EPS) * sg_ref[...] * (1.0 - lambda_init)
        o_ref[0, :, lanes[hh]] = o.T.astype(o_ref.dtype)


def _diff_attention(qkv, lam_vecs, subln_gain, *, lambda_init, tq):
    b, s, _ = qkv.shape
    hps = ATTN_HEADS_PER_STEP
    groups = DIFF_HEADS // hps
    width = hps * LANES
    kern = functools.partial(_diff_attn_kernel, tq=tq, lambda_init=lambda_init)
    return pl.pallas_call(
        kern,
        out_shape=jax.ShapeDtypeStruct((b, s, DIFF_HEADS * DIFF_V_DIM), BF16),
        grid=(b, groups, s // tq),
        in_specs=[
            pl.BlockSpec((4, DIFF_HEAD_DIM), lambda bi, gi, qi: (0, 0)),
            pl.BlockSpec((DIFF_V_DIM, 1), lambda bi, gi, qi: (0, 0)),
            pl.BlockSpec((1, tq, width), lambda bi, gi, qi: (bi, qi, gi)),
            pl.BlockSpec((1, s, width), lambda bi, gi, qi: (bi, 0, groups + gi)),
            pl.BlockSpec((1, s, width), lambda bi, gi, qi: (bi, 0, 2 * groups + gi)),
        ],
        out_specs=pl.BlockSpec((1, tq, width), lambda bi, gi, qi: (bi, qi, gi)),
        scratch_shapes=[
            pltpu.VMEM((hps, s // tq, DIFF_V_DIM, tq), BF16),
            pltpu.VMEM((hps, 2, tq, 2 * tq), F32),
            pltpu.VMEM((hps, DIFF_V_DIM, 2 * tq), F32),
        ],
        compiler_params=_cparams(("parallel", "parallel", "arbitrary")),
        name="diff_attn",
    )(lam_vecs, subln_gain, qkv, qkv, qkv)


def _proj_res_kernel(a_ref, w_ref, res_ref, o_ref):
    o_ref[...] = res_ref[...] + jnp.dot(a_ref[...], w_ref[...], preferred_element_type=F32)


def _proj_res(a, w, res, *, tm):
    m, k = a.shape
    n = w.shape[1]
    return pl.pallas_call(
        _proj_res_kernel,
        out_shape=jax.ShapeDtypeStruct((m, n), F32),
        grid=(m // tm,),
        in_specs=[
            pl.BlockSpec((tm, k), lambda i: (i, 0)),
            pl.BlockSpec((k, n), lambda i: (0, 0)),
            pl.BlockSpec((tm, n), lambda i: (i, 0)),
        ],
        out_specs=pl.BlockSpec((tm, n), lambda i: (i, 0)),
        compiler_params=_cparams(("parallel",)),
        name="proj_res",
    )(a, w, res)


def _gla_gate_kernel(x_ref, g_ref, w1_ref, w2_ref, b_ref, o_ref):
    xn = _rms(x_ref[...], g_ref[...]).astype(BF16)
    t = jnp.dot(xn, w1_ref[...], preferred_element_type=F32)
    z = jnp.dot(t, w2_ref[...], preferred_element_type=F32, precision=lax.Precision.HIGHEST) + b_ref[...]
    log_sig = jnp.minimum(z, 0.0) - jnp.log1p(jnp.exp(-jnp.abs(z)))
    o_ref[...] = log_sig * (1.0 / GLA_GATE_TEMP)


def _gla_gate(x, gain, w1, w2, bias, *, tm):
    m, d = x.shape
    r = w1.shape[1]
    n = w2.shape[1]
    return pl.pallas_call(
        _gla_gate_kernel,
        out_shape=jax.ShapeDtypeStruct((m, n), F32),
        grid=(m // tm,),
        in_specs=[
            pl.BlockSpec((tm, d), lambda i: (i, 0)),
            pl.BlockSpec((1, d), lambda i: (0, 0)),
            pl.BlockSpec((d, r), lambda i: (0, 0)),
            pl.BlockSpec((r, n), lambda i: (0, 0)),
            pl.BlockSpec((1, n), lambda i: (0, 0)),
        ],
        out_specs=pl.BlockSpec((tm, n), lambda i: (i, 0)),
        compiler_params=_cparams(("parallel",)),
        name="gla_gate",
    )(x, gain, w1, w2, bias)


def _split3(x):
    hi = x.astype(BF16)
    r1 = x - hi.astype(F32)
    mid = r1.astype(BF16)
    lo = (r1 - mid.astype(F32)).astype(BF16)
    return hi, mid, lo


def _gla_kernel(q_ref, k_ref, v_ref, r_ref, g_ref, ng_ref, o_ref, state_ref):
    t_blk = q_ref.shape[1]
    hk = GLA_HEADS * GLA_DK

    @pl.when(pl.program_id(1) == 0)
    def _():
        state_ref[...] = jnp.zeros(state_ref.shape, F32)

    ri = lax.broadcasted_iota(jnp.int32, (t_blk, t_blk), 0)
    ci = lax.broadcasted_iota(jnp.int32, (t_blk, t_blk), 1)

    tri = jnp.where(ci <= ri, 1.0, 0.0).astype(BF16)
    g_hi, g_mid, g_lo = _split3(g_ref[0])
    b = (jnp.dot(tri, g_hi, preferred_element_type=F32)
         + jnp.dot(tri, g_mid, preferred_element_type=F32)
         + jnp.dot(tri, g_lo, preferred_element_type=F32))

    q = q_ref[0].astype(F32) * (GLA_DK ** -0.5)
    k = k_ref[0].astype(F32)

    row = lax.broadcasted_iota(jnp.int32, (t_blk, hk), 0)
    b_last = b[t_blk - 1:t_blk, :]
    q_in = (q * jnp.exp(b)).astype(BF16)
    k_out = (k * jnp.exp(b_last - b)).astype(BF16)
    e_last = jnp.exp(b_last)

    level = t_blk // 2
    levels = []
    while level >= GLA_DIAG:
        grp = 2 * level
        b3 = b.reshape(t_blk // grp, grp, hk)
        pivot = jnp.broadcast_to(b3[:, level - 1:level, :], b3.shape).reshape(t_blk, hk)
        upper = (row % grp) >= level
        e = jnp.exp(jnp.where(upper, b - pivot, pivot - b))
        qt = jnp.where(upper, q * e, 0.0).astype(BF16)
        kt = jnp.where(upper, 0.0, k * e).astype(BF16)
        levels.append((grp, qt, kt))
        level //= 2

    nd = t_blk // GLA_DIAG
    b3 = b.reshape(nd, GLA_DIAG, hk)
    q3 = q.reshape(nd, GLA_DIAG, hk)
    k3 = k.reshape(nd, GLA_DIAG, hk)
    ones = jnp.ones((GLA_DK, t_blk), BF16)
    dcol = ci - (ri // GLA_DIAG) * GLA_DIAG
    rloc = ri % GLA_DIAG

    eye = (lax.broadcasted_iota(jnp.int32, (GLA_DK, GLA_DK), 0)
           == lax.broadcasted_iota(jnp.int32, (GLA_DK, GLA_DK), 1))

    diag_terms = []
    for j in range(GLA_DIAG):
        d = jnp.minimum(b3 - b3[:, j:j + 1, :], 0.0)
        pj = (jnp.exp(d) * q3 * k3[:, j:j + 1, :]).reshape(t_blk, hk)
        diag_terms.append(pj)

    ng = ng_ref[...]
    for h in range(GLA_HEADS):
        ks = slice(h * GLA_DK, (h + 1) * GLA_DK)
        vs = slice(h * GLA_DV, (h + 1) * GLA_DV)
        a = jnp.zeros((t_blk, t_blk), F32)
        for grp, qt, kt in levels:
            s_l = lax.dot_general(qt[:, ks], kt[:, ks], (((1,), (1,)), ((), ())),
                                  preferred_element_type=F32)
            if grp == t_blk:
                a = a + s_l
            else:
                a = a + jnp.where((ri // grp) == (ci // grp), s_l, 0.0)
        for j in range(GLA_DIAG):
            pj = diag_terms[j][:, ks]
            p_hi = pj.astype(BF16)
            p_lo = (pj - p_hi.astype(F32)).astype(BF16)
            r_j = (jnp.dot(p_hi, ones, preferred_element_type=F32)
                   + jnp.dot(p_lo, ones, preferred_element_type=F32))
            a = jnp.where((dcol == j) & (rloc >= j), r_j, a)

        v_h = v_ref[0, :, vs]
        state = state_ref[h]
        o = (jnp.dot(q_in[:, ks], state.astype(BF16), preferred_element_type=F32)
             + jnp.dot(a.astype(BF16), v_h, preferred_element_type=F32))

        e_col = jnp.sum(jnp.where(eye, jnp.broadcast_to(e_last[:, ks], (GLA_DK, GLA_DK)), 0.0),
                        axis=1, keepdims=True)
        upd = lax.dot_general(k_out[:, ks], v_h, (((0,), (0,)), ((), ())),
                              preferred_element_type=F32)
        state_ref[h] = e_col * state + upd

        r_h = r_ref[0, :, vs].astype(F32)
        gate = r_h / (1.0 + jnp.exp(-r_h))
        o_ref[0, :, vs] = (_rms(o, ng) * gate).astype(o_ref.dtype)


def _gla(proj, g, norm_gain):
    b, s, _ = proj.shape
    t = GLA_BLOCK
    hk = GLA_HEADS * GLA_DK
    hv = GLA_HEADS * GLA_DV
    return pl.pallas_call(
        _gla_kernel,
        out_shape=jax.ShapeDtypeStruct((b, s, hv), BF16),
        grid=(b, s // t),
        in_specs=[
            pl.BlockSpec((1, t, hk), lambda bi, ti: (bi, ti, 0)),
            pl.BlockSpec((1, t, hk), lambda bi, ti: (bi, ti, 1)),
            pl.BlockSpec((1, t, hv), lambda bi, ti: (bi, ti, 1)),
            pl.BlockSpec((1, t, hv), lambda bi, ti: (bi, ti, 2)),
            pl.BlockSpec((1, t, hk), lambda bi, ti: (bi, ti, 0)),
            pl.BlockSpec((1, GLA_DV), lambda bi, ti: (0, 0)),
        ],
        out_specs=pl.BlockSpec((1, t, hv), lambda bi, ti: (bi, ti, 0)),
        scratch_shapes=[pltpu.VMEM((GLA_HEADS, GLA_DK, GLA_DV), F32)],
        compiler_params=_cparams(("parallel", "arbitrary")),
        name="gla",
    )(proj, proj, proj, proj, g, norm_gain)


def _ffn_kernel(x_ref, halo_ref, g_ref, wg_ref, wu_ref, cwg_ref, cwu_ref, cbg_ref, cbu_ref, wd_ref,
                fg_ref, o_ref, xn_ref, acc_ref, *, tiles_per_seq, final_norm):
    i = pl.program_id(0)
    j = pl.program_id(1)
    nj = pl.num_programs(1)
    tm = x_ref.shape[0]

    @pl.when(j == 0)
    def _():
        gain = g_ref[...]
        halo = _rms(halo_ref[...], gain)
        halo = jnp.where(i % tiles_per_seq == 0, 0.0, halo)
        xn_ref[0:FFN_HALO, :] = halo.astype(BF16)
        xn_ref[FFN_HALO:FFN_HALO + tm, :] = _rms(x_ref[...], gain).astype(BF16)
        acc_ref[...] = jnp.zeros(acc_ref.shape, F32)

    xn = xn_ref[...]

    def conv(u, cw_ref, cb_ref):
        cw = cw_ref[...]
        u1 = pltpu.roll(u, 1, 0)
        u2 = pltpu.roll(u, 2, 0)
        y = cw[2:3, :] * u + cw[1:2, :] * u1 + cw[0:1, :] * u2 + cb_ref[...]
        return y[FFN_HALO:FFN_HALO + tm, :]

    gate = conv(jnp.dot(xn, wg_ref[...], preferred_element_type=F32), cwg_ref, cbg_ref)
    up = conv(jnp.dot(xn, wu_ref[...], preferred_element_type=F32), cwu_ref, cbu_ref)
    act = (gate / (1.0 + jnp.exp(-gate))) * up
    acc_ref[...] += jnp.dot(act.astype(BF16), wd_ref[...], preferred_element_type=F32)

    @pl.when(j == nj - 1)
    def _():
        y = x_ref[...] + acc_ref[...]
        if final_norm:
            y = _rms(y, fg_ref[...])
        o_ref[...] = y


def _ffn(x, gain, w_up, conv_w, conv_b, w_down, final_gain, *, tm, tf, seq, final_norm):
    m, d = x.shape
    f = w_down.shape[0]
    nf = f // tf
    tiles_per_seq = seq // tm
    halo_blocks = tm // FFN_HALO
    kern = functools.partial(_ffn_kernel, tiles_per_seq=tiles_per_seq, final_norm=final_norm)
    return pl.pallas_call(
        kern,
        out_shape=jax.ShapeDtypeStruct((m, d), F32),
        grid=(m // tm, nf),
        in_specs=[
            pl.BlockSpec((tm, d), lambda i, j: (i, 0)),
            pl.BlockSpec((FFN_HALO, d), lambda i, j: (jnp.maximum(i * halo_blocks - 1, 0), 0)),
            pl.BlockSpec((1, d), lambda i, j: (0, 0)),
            pl.BlockSpec((d, tf), lambda i, j: (0, j)),
            pl.BlockSpec((d, tf), lambda i, j: (0, nf + j)),
            pl.BlockSpec((CONV_WIDTH, tf), lambda i, j: (0, j)),
            pl.BlockSpec((CONV_WIDTH, tf), lambda i, j: (0, nf + j)),
            pl.BlockSpec((1, tf), lambda i, j: (0, j)),
            pl.BlockSpec((1, tf), lambda i, j: (0, nf + j)),
            pl.BlockSpec((tf, d), lambda i, j: (j, 0)),
            pl.BlockSpec((1, d), lambda i, j: (0, 0)),
        ],
        out_specs=pl.BlockSpec((tm, d), lambda i, j: (i, 0)),
        scratch_shapes=[pltpu.VMEM((tm + FFN_HALO, d), BF16), pltpu.VMEM((tm, d), F32)],
        compiler_params=_cparams(("parallel", "arbitrary")),
        name="ffn",
    )(x, x, gain, w_up, w_up, conv_w, conv_w, conv_b, conv_b, w_down, final_gain)


def _rope_tables(seq):
    half = DIFF_HEAD_DIM // 2
    inv = 1.0 / (ROPE_THETA ** (jnp.arange(0, DIFF_HEAD_DIM, 2, dtype=F32) / DIFF_HEAD_DIM))
    ang = jnp.arange(seq, dtype=F32)[:, None] * inv[None, :]
    cos, sin = jnp.cos(ang), jnp.sin(ang)
    reps = LANES // half
    cos_t = jnp.tile(cos, (1, reps))
    sin_t = jnp.tile(jnp.concatenate([-sin, sin], axis=1), (1, reps // 2))
    return cos_t, sin_t


def _trunk(x, norm_mix, norm_ffn, norm_final, diff_w_qkv, diff_w_o, diff_lambda, diff_subln,
           gla_w_in, gla_w_a1, gla_w_a2, gla_b_a, gla_norm, gla_w_o,
           ffn_w_up, ffn_conv_w, ffn_conv_b, ffn_w_down, *, tm, tq, tf):
    bsz, seq, d = x.shape
    depth = norm_mix.shape[0]
    m = bsz * seq
    cos_t, sin_t = _rope_tables(seq)
    h = x.reshape(m, d)
    rank = gla_w_a1.shape[-1]
    for layer in range(depth):
        jdx = layer // 2
        gain = norm_mix[layer].reshape(1, d)
        if layer % 2 == 0:
            lambda_init = 0.8 - 0.6 * math.exp(-0.3 * layer)
            qkv = _norm_proj(h, gain, diff_w_qkv[jdx].astype(BF16), cos_t, sin_t, tm=tm, tn=d,
                             n_rope_blocks=2, q_scale=DIFF_HEAD_DIM ** -0.5 * math.log2(math.e), seq=seq)
            o = _diff_attention(qkv.reshape(bsz, seq, 3 * d), diff_lambda[jdx],
                                diff_subln[jdx].reshape(DIFF_V_DIM, 1),
                                lambda_init=lambda_init, tq=tq)
            h = _proj_res(o.reshape(m, d), diff_w_o[jdx].astype(BF16), h, tm=tm)
        else:
            proj = _norm_proj(h, gain, gla_w_in[jdx].astype(BF16), cos_t, sin_t, tm=tm, tn=d,
                              n_rope_blocks=0, q_scale=1.0, seq=seq)
            w1 = jnp.pad(gla_w_a1[jdx], ((0, 0), (0, LANES - rank))).astype(BF16)
            w2 = jnp.pad(gla_w_a2[jdx], ((0, LANES - rank), (0, 0)))
            g = _gla_gate(h, gain, w1, w2, gla_b_a[jdx].reshape(1, -1), tm=tm)
            o = _gla(proj.reshape(bsz, seq, 3 * d), g.reshape(bsz, seq, -1),
                     gla_norm[jdx].reshape(1, GLA_DV))
            h = _proj_res(o.reshape(m, d), gla_w_o[jdx].astype(BF16), h, tm=tm)
        h = _ffn(h, norm_ffn[layer].reshape(1, d), ffn_w_up[layer].astype(BF16), ffn_conv_w[layer],
                 ffn_conv_b[layer].reshape(1, -1), ffn_w_down[layer].astype(BF16),
                 norm_final.reshape(1, d), tm=tm, tf=tf, seq=seq, final_norm=(layer == depth - 1))
    return h.reshape(bsz, seq, d)


def kernel(x, norm_mix, norm_ffn, norm_final, diff_w_qkv, diff_w_o, diff_lambda, diff_subln, gla_w_in, gla_w_a1, gla_w_a2, gla_b_a, gla_norm, gla_w_o, ffn_w_up, ffn_conv_w, ffn_conv_b, ffn_w_down):
    return _trunk(x, norm_mix, norm_ffn, norm_final, diff_w_qkv, diff_w_o, diff_lambda, diff_subln,
                  gla_w_in, gla_w_a1, gla_w_a2, gla_b_a, gla_norm, gla_w_o,
                  ffn_w_up, ffn_conv_w, ffn_conv_b, ffn_w_down, tm=1024, tq=512, tf=256)
```

```python
import functools
import math

import jax
import jax.numpy as jnp
from jax import lax
from jax.experimental import pallas as pl
from jax.experimental.pallas import tpu as pltpu

F32 = jnp.float32
BF16 = jnp.bfloat16

NORM_EPS = 1e-6
ROPE_THETA = 10000.0

DIFF_HEADS = 8
DIFF_HEAD_DIM = 64
DIFF_V_DIM = 2 * DIFF_HEAD_DIM

GLA_HEADS = 4
GLA_DK = 128
GLA_DV = 256
GLA_GATE_TEMP = 16.0
GLA_BLOCK = 256
GLA_DIAG = 8

CONV_WIDTH = 3
FFN_HALO = 16

LANES = 128
VMEM_LIMIT_BYTES = 56 * 1024 * 1024


def _cparams(semantics):
    return pltpu.CompilerParams(dimension_semantics=semantics, vmem_limit_bytes=VMEM_LIMIT_BYTES)


def _rms(x, gain):
    ms = jnp.mean(x * x, axis=-1, keepdims=True)
    return x * lax.rsqrt(ms + NORM_EPS) * gain


def _norm_proj_kernel(x_ref, g_ref, w_ref, cos_ref, sin_ref, o_ref, xn_ref, *, n_rope_blocks, q_scale):
    j = pl.program_id(1)

    @pl.when(j == 0)
    def _():
        xn_ref[...] = _rms(x_ref[...], g_ref[...]).astype(BF16)

    acc = jnp.dot(xn_ref[...], w_ref[...], preferred_element_type=F32)
    tm, tn = acc.shape

    if n_rope_blocks == 0:
        o_ref[...] = acc.astype(o_ref.dtype)
        return

    @pl.when(j >= n_rope_blocks)
    def _():
        o_ref[...] = acc.astype(o_ref.dtype)

    @pl.when(j < n_rope_blocks)
    def _():
        c = cos_ref[...]
        s = sin_ref[...]
        scale = jnp.where(j == 0, q_scale, 1.0).astype(F32)
        lane = lax.broadcasted_iota(jnp.int32, (tm, LANES), 1)
        first_half = (lane % DIFF_HEAD_DIM) < (DIFF_HEAD_DIM // 2)
        for hh in range(tn // LANES):
            t = acc[:, hh * LANES:(hh + 1) * LANES]
            partner = jnp.where(first_half,
                                pltpu.roll(t, LANES - DIFF_HEAD_DIM // 2, 1),
                                pltpu.roll(t, DIFF_HEAD_DIM // 2, 1))
            o_ref[:, hh * LANES:(hh + 1) * LANES] = ((t * c + partner * s) * scale).astype(o_ref.dtype)


def _norm_proj(x, gain, w, cos_t, sin_t, *, tm, tn, n_rope_blocks, q_scale, seq):
    m, d = x.shape
    n = w.shape[1]
    nseq = seq // tm
    kern = functools.partial(_norm_proj_kernel, n_rope_blocks=n_rope_blocks, q_scale=q_scale)
    return pl.pallas_call(
        kern,
        out_shape=jax.ShapeDtypeStruct((m, n), BF16),
        grid=(m // tm, n // tn),
        in_specs=[
            pl.BlockSpec((tm, d), lambda i, j: (i, 0)),
            pl.BlockSpec((1, d), lambda i, j: (0, 0)),
            pl.BlockSpec((d, tn), lambda i, j: (0, j)),
            pl.BlockSpec((tm, LANES), lambda i, j: (i % nseq, 0)),
            pl.BlockSpec((tm, LANES), lambda i, j: (i % nseq, 0)),
        ],
        out_specs=pl.BlockSpec((tm, tn), lambda i, j: (i, j)),
        scratch_shapes=[pltpu.VMEM((tm, d), BF16)],
        compiler_params=_cparams(("parallel", "arbitrary")),
        name="norm_proj",
    )(x, gain, w, cos_t, sin_t)


ATTN_CHUNK = 256
ATTN_HEADS_PER_STEP = 4


def _diff_attn_kernel(lam_ref, sg_ref, q_ref, k_ref, v_ref, o_ref, vt_ref, s_ref, acc_ref, *, tq, lambda_init):
    qi = pl.program_id(2)
    n_kv_blocks = vt_ref.shape[1]
    n2 = 2 * tq
    ch = min(ATTN_CHUNK, tq)
    nch = tq // ch
    heads = range(ATTN_HEADS_PER_STEP)
    lanes = [slice(hh * LANES, (hh + 1) * LANES) for hh in heads]

    @pl.when(qi == 0)
    def _():
        for hh in heads:
            for c in range(n_kv_blocks):
                vt_ref[hh, c] = v_ref[0, c * tq:(c + 1) * tq, lanes[hh]].astype(F32).T.astype(BF16)

    qqt = []
    for hh in heads:
        qt = q_ref[0, :, lanes[hh]].astype(F32).T
        feat = lax.broadcasted_iota(jnp.int32, qt.shape, 0)
        qqt.append(jnp.concatenate([jnp.where(feat < DIFF_HEAD_DIM, qt, 0.0),
                                    jnp.where(feat >= DIFF_HEAD_DIM, qt, 0.0)], axis=1).astype(BF16))
    acc_ref[...] = jnp.zeros(acc_ref.shape, F32)

    def scores(hh, blk, slot, masked):
        cmx8 = None
        for c in range(nch):
            start = pl.multiple_of(blk * tq + c * ch, ch)
            s = jnp.dot(k_ref[0, pl.ds(start, ch), lanes[hh]], qqt[hh], preferred_element_type=F32)
            if masked:
                kv_pos = lax.broadcasted_iota(jnp.int32, s.shape, 0) + c * ch
                col = lax.broadcasted_iota(jnp.int32, s.shape, 1)
                q_pos = jnp.where(col >= tq, col - tq, col)
                s = jnp.where(kv_pos <= q_pos, s, -jnp.inf)
            s_ref[hh, slot, c * ch:(c + 1) * ch, :] = s
            part = jnp.max(s.reshape(ch // 8, 8, n2), axis=0)
            cmx8 = part if cmx8 is None else jnp.maximum(cmx8, part)
        return cmx8

    def softmax_pv(hh, blk, slot, m, l8, cmx8):
        m_new = jnp.maximum(m, jnp.max(cmx8, axis=0, keepdims=True))
        alpha = jnp.exp2(m - m_new)
        l8 = alpha * l8
        pv = None
        for c in range(nch):
            p = jnp.exp2(s_ref[hh, slot, c * ch:(c + 1) * ch, :] - m_new)
            l8 = l8 + jnp.sum(p.reshape(ch // 8, 8, n2), axis=0)
            d = jnp.dot(vt_ref[hh, blk, :, c * ch:(c + 1) * ch], p.astype(BF16), preferred_element_type=F32)
            pv = d if pv is None else pv + d
        acc_ref[hh] = alpha * acc_ref[hh] + pv
        return m_new, l8

    cmx8_0 = [scores(hh, qi, 0, True) for hh in heads]

    def body(t, carry):
        slot = lax.rem(t, 2)
        stats = [softmax_pv(hh, jnp.where(t == 0, qi, t - 1), slot, *carry[hh]) for hh in heads]
        return tuple(stats[hh] + (scores(hh, t, 1 - slot, False),) for hh in heads)

    init = tuple((jnp.full((1, n2), -jnp.inf, F32), jnp.zeros((8, n2), F32), cmx8_0[hh]) for hh in heads)
    carry = lax.fori_loop(0, qi, body, init)

    lv = lam_ref[...]
    lam = (jnp.exp(jnp.sum(lv[0:1] * lv[1:2], axis=1, keepdims=True))
           - jnp.exp(jnp.sum(lv[2:3] * lv[3:4], axis=1, keepdims=True)) + lambda_init)
    last = [softmax_pv(hh, jnp.maximum(qi - 1, 0), lax.rem(qi, 2), *carry[hh]) for hh in heads]
    for hh in heads:
        l = jnp.sum(last[hh][1], axis=0, keepdims=True)
        o1 = acc_ref[hh, :, 0:tq] / l[:, 0:tq]
        o2 = acc_ref[hh, :, tq:n2] / l[:, tq:n2]
        o = o1 - lam * o2
        ms = jnp.mean(o * o, axis=0, keepdims=True)
        o = o * lax.rsqrt(ms + NORM_EPS) * sg_ref[...] * (1.0 - lambda_init)
        o_ref[0, :, lanes[hh]] = o.T.astype(o_ref.dtype)


def _diff_attention(qkv, lam_vecs, subln_gain, *, lambda_init, tq):
    b, s, _ = qkv.shape
    hps = ATTN_HEADS_PER_STEP
    groups = DIFF_HEADS // hps
    width = hps * LANES
    kern = functools.partial(_diff_attn_kernel, tq=tq, lambda_init=lambda_init)
    return pl.pallas_call(
        kern,
        out_shape=jax.ShapeDtypeStruct((b, s, DIFF_HEADS * DIFF_V_DIM), BF16),
        grid=(b, groups, s // tq),
        in_specs=[
            pl.BlockSpec((4, DIFF_HEAD_DIM), lambda bi, gi, qi: (0, 0)),
            pl.BlockSpec((DIFF_V_DIM, 1), lambda bi, gi, qi: (0, 0)),
            pl.BlockSpec((1, tq, width), lambda bi, gi, qi: (bi, qi, gi)),
            pl.BlockSpec((1, s, width), lambda bi, gi, qi: (bi, 0, groups + gi)),
            pl.BlockSpec((1, s, width), lambda bi, gi, qi: (bi, 0, 2 * groups + gi)),
        ],
        out_specs=pl.BlockSpec((1, tq, width), lambda bi, gi, qi: (bi, qi, gi)),
        scratch_shapes=[
            pltpu.VMEM((hps, s // tq, DIFF_V_DIM, tq), BF16),
            pltpu.VMEM((hps, 2, tq, 2 * tq), F32),
            pltpu.VMEM((hps, DIFF_V_DIM, 2 * tq), F32),
        ],
        compiler_params=_cparams(("parallel", "parallel", "arbitrary")),
        name="diff_attn",
    )(lam_vecs, subln_gain, qkv, qkv, qkv)


def _proj_res_kernel(a_ref, w_ref, res_ref, o_ref):
    o_ref[...] = res_ref[...] + jnp.dot(a_ref[...], w_ref[...], preferred_element_type=F32)


def _proj_res(a, w, res, *, tm):
    m, k = a.shape
    n = w.shape[1]
    return pl.pallas_call(
        _proj_res_kernel,
        out_shape=jax.ShapeDtypeStruct((m, n), F32),
        grid=(m // tm,),
        in_specs=[
            pl.BlockSpec((tm, k), lambda i: (i, 0)),
            pl.BlockSpec((k, n), lambda i: (0, 0)),
            pl.BlockSpec((tm, n), lambda i: (i, 0)),
        ],
        out_specs=pl.BlockSpec((tm, n), lambda i: (i, 0)),
        compiler_params=_cparams(("parallel",)),
        name="proj_res",
    )(a, w, res)


def _gla_gate_kernel(x_ref, g_ref, w1_ref, w2_ref, b_ref, o_ref):
    xn = _rms(x_ref[...], g_ref[...]).astype(BF16)
    t = jnp.dot(xn, w1_ref[...], preferred_element_type=F32)
    z = jnp.dot(t, w2_ref[...], preferred_element_type=F32, precision=lax.Precision.HIGHEST) + b_ref[...]
    log_sig = jnp.minimum(z, 0.0) - jnp.log1p(jnp.exp(-jnp.abs(z)))
    o_ref[...] = log_sig * (1.0 / GLA_GATE_TEMP)


def _gla_gate(x, gain, w1, w2, bias, *, tm):
    m, d = x.shape
    r = w1.shape[1]
    n = w2.shape[1]
    return pl.pallas_call(
        _gla_gate_kernel,
        out_shape=jax.ShapeDtypeStruct((m, n), F32),
        grid=(m // tm,),
        in_specs=[
            pl.BlockSpec((tm, d), lambda i: (i, 0)),
            pl.BlockSpec((1, d), lambda i: (0, 0)),
            pl.BlockSpec((d, r), lambda i: (0, 0)),
            pl.BlockSpec((r, n), lambda i: (0, 0)),
            pl.BlockSpec((1, n), lambda i: (0, 0)),
        ],
        out_specs=pl.BlockSpec((tm, n), lambda i: (i, 0)),
        compiler_params=_cparams(("parallel",)),
        name="gla_gate",
    )(x, gain, w1, w2, bias)


def _split3(x):
    hi = x.astype(BF16)
    r1 = x - hi.astype(F32)
    mid = r1.astype(BF16)
    lo = (r1 - mid.astype(F32)).astype(BF16)
    return hi, mid, lo


def _gla_kernel(q_ref, k_ref, v_ref, r_ref, g_ref, ng_ref, o_ref, state_ref):
    t_blk = q_ref.shape[1]
    hk = GLA_HEADS * GLA_DK

    @pl.when(pl.program_id(1) == 0)
    def _():
        state_ref[...] = jnp.zeros(state_ref.shape, F32)

    ri = lax.broadcasted_iota(jnp.int32, (t_blk, t_blk), 0)
    ci = lax.broadcasted_iota(jnp.int32, (t_blk, t_blk), 1)

    tri = jnp.where(ci <= ri, 1.0, 0.0).astype(BF16)
    g_hi, g_mid, g_lo = _split3(g_ref[0])
    b = (jnp.dot(tri, g_hi, preferred_element_type=F32)
         + jnp.dot(tri, g_mid, preferred_element_type=F32)
         + jnp.dot(tri, g_lo, preferred_element_type=F32))
    b = b * math.log2(math.e)

    q = q_ref[0].astype(F32) * (GLA_DK ** -0.5)
    k = k_ref[0].astype(F32)

    row = lax.broadcasted_iota(jnp.int32, (t_blk, hk), 0)
    b_last = b[t_blk - 1:t_blk, :]
    q_in = (q * jnp.exp2(b)).astype(BF16)
    k_out = (k * jnp.exp2(b_last - b)).astype(BF16)
    e_last = jnp.exp2(b_last)

    level = t_blk // 2
    levels = []
    while level >= GLA_DIAG:
        grp = 2 * level
        b3 = b.reshape(t_blk // grp, grp, hk)
        pivot = jnp.broadcast_to(b3[:, level - 1:level, :], b3.shape).reshape(t_blk, hk)
        upper = (row % grp) >= level
        e = jnp.exp2(-jnp.abs(b - pivot))
        qt = jnp.where(upper, q * e, 0.0).astype(BF16)
        kt = jnp.where(upper, 0.0, k * e).astype(BF16)
        levels.append((grp, qt, kt))
        level //= 2

    nd = t_blk // GLA_DIAG
    b3 = b.reshape(nd, GLA_DIAG, hk)
    q3 = q.reshape(nd, GLA_DIAG, hk)
    k3 = k.reshape(nd, GLA_DIAG, hk)
    sel_shape = (GLA_DIAG * GLA_DK, t_blk)
    sel = jnp.where(lax.broadcasted_iota(jnp.int32, sel_shape, 0) // GLA_DK
                    == lax.broadcasted_iota(jnp.int32, sel_shape, 1) % GLA_DIAG, 1.0, 0.0).astype(BF16)
    diag_mask = ((ri // GLA_DIAG) == (ci // GLA_DIAG)) & (ci <= ri)

    eye = (lax.broadcasted_iota(jnp.int32, (GLA_DK, GLA_DK), 0)
           == lax.broadcasted_iota(jnp.int32, (GLA_DK, GLA_DK), 1))

    diag_terms = []
    for j in range(GLA_DIAG):
        d = jnp.minimum(b3 - b3[:, j:j + 1, :], 0.0)
        pj = (jnp.exp2(d) * q3 * k3[:, j:j + 1, :]).reshape(t_blk, hk)
        diag_terms.append(pj)

    ng = ng_ref[...]
    for h in range(GLA_HEADS):
        ks = slice(h * GLA_DK, (h + 1) * GLA_DK)
        vs = slice(h * GLA_DV, (h + 1) * GLA_DV)
        a = jnp.zeros((t_blk, t_blk), F32)
        for grp, qt, kt in levels:
            s_l = lax.dot_general(qt[:, ks], kt[:, ks], (((1,), (1,)), ((), ())),
                                  preferred_element_type=F32)
            if grp == t_blk:
                a = a + s_l
            else:
                a = a + jnp.where((ri // grp) == (ci // grp), s_l, 0.0)
        stacked = jnp.concatenate([diag_terms[j][:, ks] for j in range(GLA_DIAG)], axis=1)
        a = jnp.where(diag_mask, jnp.dot(stacked.astype(BF16), sel, preferred_element_type=F32), a)

        v_h = v_ref[0, :, vs]
        state = state_ref[h]
        o = (jnp.dot(q_in[:, ks], state.astype(BF16), preferred_element_type=F32)
             + jnp.dot(a.astype(BF16), v_h, preferred_element_type=F32))

        e_col = jnp.sum(jnp.where(eye, jnp.broadcast_to(e_last[:, ks], (GLA_DK, GLA_DK)), 0.0),
                        axis=1, keepdims=True)
        upd = lax.dot_general(k_out[:, ks], v_h, (((0,), (0,)), ((), ())),
                              preferred_element_type=F32)
        state_ref[h] = e_col * state + upd

        r_h = r_ref[0, :, vs].astype(F32)
        gate = r_h / (1.0 + jnp.exp(-r_h))
        o_ref[0, :, vs] = (_rms(o, ng) * gate).astype(o_ref.dtype)


def _gla(proj, g, norm_gain):
    b, s, _ = proj.shape
    t = GLA_BLOCK
    hk = GLA_HEADS * GLA_DK
    hv = GLA_HEADS * GLA_DV
    return pl.pallas_call(
        _gla_kernel,
        out_shape=jax.ShapeDtypeStruct((b, s, hv), BF16),
        grid=(b, s // t),
        in_specs=[
            pl.BlockSpec((1, t, hk), lambda bi, ti: (bi, ti, 0)),
            pl.BlockSpec((1, t, hk), lambda bi, ti: (bi, ti, 1)),
            pl.BlockSpec((1, t, hv), lambda bi, ti: (bi, ti, 1)),
            pl.BlockSpec((1, t, hv), lambda bi, ti: (bi, ti, 2)),
            pl.BlockSpec((1, t, hk), lambda bi, ti: (bi, ti, 0)),
            pl.BlockSpec((1, GLA_DV), lambda bi, ti: (0, 0)),
        ],
        out_specs=pl.BlockSpec((1, t, hv), lambda bi, ti: (bi, ti, 0)),
        scratch_shapes=[pltpu.VMEM((GLA_HEADS, GLA_DK, GLA_DV), F32)],
        compiler_params=_cparams(("parallel", "arbitrary")),
        name="gla",
    )(proj, proj, proj, proj, g, norm_gain)


def _ffn_kernel(x_ref, halo_ref, g_ref, wg_ref, wu_ref, cwg_ref, cwu_ref, cbg_ref, cbu_ref, wd_ref,
                fg_ref, o_ref, xn_ref, acc_ref, slab_ref, *, tiles_per_seq, final_norm):
    i = pl.program_id(0)
    j = pl.program_id(1)
    nj = pl.num_programs(1)
    tm, d = x_ref.shape
    ext = tm + FFN_HALO
    seg = ext // 8
    nslab = d // LANES

    @pl.when(j == 0)
    def _():
        gain = g_ref[...]
        halo = _rms(halo_ref[...], gain)
        halo = jnp.where(i % tiles_per_seq == 0, 0.0, halo)
        xn = _rms(x_ref[...], gain)
        for c in range(nslab):
            slab_ref[c, 0:FFN_HALO, :] = halo[:, c * LANES:(c + 1) * LANES]
            slab_ref[c, FFN_HALO:ext, :] = xn[:, c * LANES:(c + 1) * LANES]
        for a2 in range(seg // 2):
            rows = [jnp.concatenate([slab_ref[c, pl.ds(2 * a2 + r, 8, stride=seg), :] for c in range(nslab)],
                                    axis=1) for r in range(2)]
            xn_ref[16 * a2:16 * a2 + 16, :] = jnp.concatenate(rows, axis=0).astype(BF16)
        acc_ref[...] = jnp.zeros(acc_ref.shape, F32)

    xn = xn_ref[...]

    def conv(u, cw_ref, cb_ref):
        cw = cw_ref[...]
        u3 = u.reshape(seg, 8, u.shape[-1])
        wrap1 = pltpu.roll(u3[seg - 1], 1, 0)[None]
        wrap2 = pltpu.roll(u3[seg - 2], 1, 0)[None]
        prev1 = jnp.concatenate([wrap1, u3[:seg - 1]], axis=0)
        prev2 = jnp.concatenate([wrap2, wrap1, u3[:seg - 2]], axis=0)
        return cw[2:3, :] * u3 + cw[1:2, :] * prev1 + cw[0:1, :] * prev2 + cb_ref[...]

    gate = conv(jnp.dot(xn, wg_ref[...], preferred_element_type=F32), cwg_ref, cbg_ref)
    up = conv(jnp.dot(xn, wu_ref[...], preferred_element_type=F32), cwu_ref, cbu_ref)
    act = ((gate / (1.0 + jnp.exp(-gate))) * up).reshape(ext, gate.shape[-1])
    acc_ref[...] += jnp.dot(act.astype(BF16), wd_ref[...], preferred_element_type=F32)

    @pl.when(j == nj - 1)
    def _():
        for a in range(seg):
            for c in range(nslab):
                slab_ref[c, pl.ds(a, 8, stride=seg), :] = acc_ref[8 * a:8 * a + 8, c * LANES:(c + 1) * LANES]
        y = x_ref[...] + jnp.concatenate([slab_ref[c, FFN_HALO:ext, :] for c in range(nslab)], axis=1)
        if final_norm:
            y = _rms(y, fg_ref[...])
        o_ref[...] = y


def _ffn(x, gain, w_up, conv_w, conv_b, w_down, final_gain, *, tm, tf, seq, final_norm):
    m, d = x.shape
    f = w_down.shape[0]
    nf = f // tf
    tiles_per_seq = seq // tm
    halo_blocks = tm // FFN_HALO
    kern = functools.partial(_ffn_kernel, tiles_per_seq=tiles_per_seq, final_norm=final_norm)
    return pl.pallas_call(
        kern,
        out_shape=jax.ShapeDtypeStruct((m, d), F32),
        grid=(m // tm, nf),
        in_specs=[
            pl.BlockSpec((tm, d), lambda i, j: (i, 0)),
            pl.BlockSpec((FFN_HALO, d), lambda i, j: (jnp.maximum(i * halo_blocks - 1, 0), 0)),
            pl.BlockSpec((1, d), lambda i, j: (0, 0)),
            pl.BlockSpec((d, tf), lambda i, j: (0, j)),
            pl.BlockSpec((d, tf), lambda i, j: (0, nf + j)),
            pl.BlockSpec((CONV_WIDTH, tf), lambda i, j: (0, j)),
            pl.BlockSpec((CONV_WIDTH, tf), lambda i, j: (0, nf + j)),
            pl.BlockSpec((1, tf), lambda i, j: (0, j)),
            pl.BlockSpec((1, tf), lambda i, j: (0, nf + j)),
            pl.BlockSpec((tf, d), lambda i, j: (j, 0)),
            pl.BlockSpec((1, d), lambda i, j: (0, 0)),
        ],
        out_specs=pl.BlockSpec((tm, d), lambda i, j: (i, 0)),
        scratch_shapes=[pltpu.VMEM((tm + FFN_HALO, d), BF16), pltpu.VMEM((tm + FFN_HALO, d), F32),
                        pltpu.VMEM((d // LANES, tm + FFN_HALO, LANES), F32)],
        compiler_params=_cparams(("parallel", "arbitrary")),
        name="ffn",
    )(x, x, gain, w_up, w_up, conv_w, conv_w, conv_b, conv_b, w_down, final_gain)


def _rope_tables(seq):
    half = DIFF_HEAD_DIM // 2
    inv = 1.0 / (ROPE_THETA ** (jnp.arange(0, DIFF_HEAD_DIM, 2, dtype=F32) / DIFF_HEAD_DIM))
    ang = jnp.arange(seq, dtype=F32)[:, None] * inv[None, :]
    cos, sin = jnp.cos(ang), jnp.sin(ang)
    reps = LANES // half
    cos_t = jnp.tile(cos, (1, reps))
    sin_t = jnp.tile(jnp.concatenate([-sin, sin], axis=1), (1, reps // 2))
    return cos_t, sin_t


def _trunk(x, norm_mix, norm_ffn, norm_final, diff_w_qkv, diff_w_o, diff_lambda, diff_subln,
           gla_w_in, gla_w_a1, gla_w_a2, gla_b_a, gla_norm, gla_w_o,
           ffn_w_up, ffn_conv_w, ffn_conv_b, ffn_w_down, *, tm, tq, tf):
    bsz, seq, d = x.shape
    depth = norm_mix.shape[0]
    m = bsz * seq
    cos_t, sin_t = _rope_tables(seq)
    h = x.reshape(m, d)
    rank = gla_w_a1.shape[-1]
    for layer in range(depth):
        jdx = layer // 2
        gain = norm_mix[layer].reshape(1, d)
        if layer % 2 == 0:
            lambda_init = 0.8 - 0.6 * math.exp(-0.3 * layer)
            qkv = _norm_proj(h, gain, diff_w_qkv[jdx].astype(BF16), cos_t, sin_t, tm=tm, tn=d,
                             n_rope_blocks=2, q_scale=DIFF_HEAD_DIM ** -0.5 * math.log2(math.e), seq=seq)
            o = _diff_attention(qkv.reshape(bsz, seq, 3 * d), diff_lambda[jdx],
                                diff_subln[jdx].reshape(DIFF_V_DIM, 1),
                                lambda_init=lambda_init, tq=tq)
            h = _proj_res(o.reshape(m, d), diff_w_o[jdx].astype(BF16), h, tm=tm)
        else:
            proj = _norm_proj(h, gain, gla_w_in[jdx].astype(BF16), cos_t, sin_t, tm=tm, tn=d,
                              n_rope_blocks=0, q_scale=1.0, seq=seq)
            w1 = jnp.pad(gla_w_a1[jdx], ((0, 0), (0, LANES - rank))).astype(BF16)
            w2 = jnp.pad(gla_w_a2[jdx], ((0, LANES - rank), (0, 0)))
            g = _gla_gate(h, gain, w1, w2, gla_b_a[jdx].reshape(1, -1), tm=tm)
            o = _gla(proj.reshape(bsz, seq, 3 * d), g.reshape(bsz, seq, -1),
                     gla_norm[jdx].reshape(1, GLA_DV))
            h = _proj_res(o.reshape(m, d), gla_w_o[jdx].astype(BF16), h, tm=tm)
        h = _ffn(h, norm_ffn[layer].reshape(1, d), ffn_w_up[layer].astype(BF16), ffn_conv_w[layer],
                 ffn_conv_b[layer].reshape(1, -1), ffn_w_down[layer].astype(BF16),
                 norm_final.reshape(1, d), tm=tm, tf=tf, seq=seq, final_norm=(layer == depth - 1))
    return h.reshape(bsz, seq, d)


def kernel(x, norm_mix, norm_ffn, norm_final, diff_w_qkv, diff_w_o, diff_lambda, diff_subln, gla_w_in, gla_w_a1, gla_w_a2, gla_b_a, gla_norm, gla_w_o, ffn_w_up, ffn_conv_w, ffn_conv_b, ffn_w_down):
    return _trunk(x, norm_mix, norm_ffn, norm_final, diff_w_qkv, diff_w_o, diff_lambda, diff_subln,
                  gla_w_in, gla_w_a1, gla_w_a2, gla_b_a, gla_norm, gla_w_o,
                  ffn_w_up, ffn_conv_w, ffn_conv_b, ffn_w_down, tm=1024, tq=512, tf=256)
```

```python
import functools
import math

import jax
import jax.numpy as jnp
from jax import lax
from jax.experimental import pallas as pl
from jax.experimental.pallas import tpu as pltpu

F32 = jnp.float32
BF16 = jnp.bfloat16

NORM_EPS = 1e-6
ROPE_THETA = 10000.0

DIFF_HEADS = 8
DIFF_HEAD_DIM = 64
DIFF_V_DIM = 2 * DIFF_HEAD_DIM

GLA_HEADS = 4
GLA_DK = 128
GLA_DV = 256
GLA_GATE_TEMP = 16.0
GLA_BLOCK = 256
GLA_DIAG = 8

CONV_WIDTH = 3
FFN_HALO = 16
FFN_SUBTILES = 4

LANES = 128
VMEM_LIMIT_BYTES = 56 * 1024 * 1024


def _cparams(semantics):
    return pltpu.CompilerParams(dimension_semantics=semantics, vmem_limit_bytes=VMEM_LIMIT_BYTES)


def _rms(x, gain):
    ms = jnp.mean(x * x, axis=-1, keepdims=True)
    return x * lax.rsqrt(ms + NORM_EPS) * gain


def _norm_proj_kernel(x_ref, g_ref, w_ref, *rest, rope):
    j = pl.program_id(1)
    if rope:
        cos_ref, sin_ref, o_ref, xn_ref = rest
    else:
        o_ref, xn_ref = rest

    @pl.when(j == 0)
    def _():
        xn_ref[...] = _rms(x_ref[...], g_ref[...]).astype(BF16)

    acc = jnp.dot(xn_ref[...], w_ref[...].astype(BF16), preferred_element_type=F32)
    if not rope:
        o_ref[...] = acc.astype(o_ref.dtype)
        return
    c = cos_ref[0]
    s = sin_ref[0]
    for hh in range(acc.shape[1] // LANES):
        t = acc[:, hh * LANES:(hh + 1) * LANES]
        o_ref[:, hh * LANES:(hh + 1) * LANES] = (t * c + pltpu.roll(t, LANES // 2, 1) * s).astype(o_ref.dtype)


def _norm_proj(x, gain, w, tables, *, tm, tn, seq):
    m, d = x.shape
    n = w.shape[1]
    nseq = seq // tm
    rope = tables is not None
    in_specs = [
        pl.BlockSpec((tm, d), lambda i, j: (i, 0)),
        pl.BlockSpec((1, d), lambda i, j: (0, 0)),
        pl.BlockSpec((d, tn), lambda i, j: (0, j)),
    ]
    if rope:
        assert tables[0].shape[0] == n // tn
        in_specs += [pl.BlockSpec((1, tm, LANES), lambda i, j: (j, i % nseq, 0))] * 2
    return pl.pallas_call(
        functools.partial(_norm_proj_kernel, rope=rope),
        out_shape=jax.ShapeDtypeStruct((m, n), BF16),
        grid=(m // tm, n // tn),
        in_specs=in_specs,
        out_specs=pl.BlockSpec((tm, tn), lambda i, j: (i, j)),
        scratch_shapes=[pltpu.VMEM((tm, d), BF16)],
        compiler_params=_cparams(("parallel", "arbitrary")),
        name="norm_proj",
    )(x, gain, w, *(tables if rope else ()))


ATTN_CHUNK = 256
ATTN_HEADS_PER_STEP = 4


def _diff_attn_kernel(lam_ref, sg_ref, q_ref, k_ref, v_ref, o_ref, vt_ref, s_ref, acc_ref, *, tq, lambda_init):
    qi = pl.program_id(2)
    n_kv_blocks = vt_ref.shape[1]
    n2 = 2 * tq
    ch = min(ATTN_CHUNK, tq)
    nch = tq // ch
    heads = range(ATTN_HEADS_PER_STEP)
    lanes = [slice(hh * LANES, (hh + 1) * LANES) for hh in heads]

    @pl.when(qi == 0)
    def _():
        for hh in heads:
            for c in range(n_kv_blocks):
                vt_ref[hh, c] = v_ref[0, c * tq:(c + 1) * tq, lanes[hh]].astype(F32).T.astype(BF16)

    qqt = []
    for hh in heads:
        qt = q_ref[0, :, lanes[hh]].astype(F32).T
        feat = lax.broadcasted_iota(jnp.int32, qt.shape, 0)
        map1 = (feat % DIFF_HEAD_DIM) < (DIFF_HEAD_DIM // 2)
        qqt.append(jnp.concatenate([jnp.where(map1, qt, 0.0), jnp.where(map1, 0.0, qt)], axis=1).astype(BF16))
    acc_ref[...] = jnp.zeros(acc_ref.shape, F32)

    def scores(hh, blk, slot, masked):
        cmx8 = None
        for c in range(nch):
            start = pl.multiple_of(blk * tq + c * ch, ch)
            s = jnp.dot(k_ref[0, pl.ds(start, ch), lanes[hh]], qqt[hh], preferred_element_type=F32)
            if masked:
                kv_pos = lax.broadcasted_iota(jnp.int32, s.shape, 0) + c * ch
                col = lax.broadcasted_iota(jnp.int32, s.shape, 1)
                q_pos = jnp.where(col >= tq, col - tq, col)
                s = jnp.where(kv_pos <= q_pos, s, -jnp.inf)
            s_ref[hh, slot, c * ch:(c + 1) * ch, :] = s
            part = jnp.max(s.reshape(ch // 8, 8, n2), axis=0)
            cmx8 = part if cmx8 is None else jnp.maximum(cmx8, part)
        return cmx8

    def softmax_pv(hh, blk, slot, m, l8, cmx8):
        m_new = jnp.maximum(m, jnp.max(cmx8, axis=0, keepdims=True))
        alpha = jnp.exp2(m - m_new)
        l8 = alpha * l8
        pv = None
        for c in range(nch):
            p = jnp.exp2(s_ref[hh, slot, c * ch:(c + 1) * ch, :] - m_new)
            l8 = l8 + jnp.sum(p.reshape(ch // 8, 8, n2), axis=0)
            d = jnp.dot(vt_ref[hh, blk, :, c * ch:(c + 1) * ch], p.astype(BF16), preferred_element_type=F32)
            pv = d if pv is None else pv + d
        acc_ref[hh] = alpha * acc_ref[hh] + pv
        return m_new, l8

    cmx8_0 = [scores(hh, qi, 0, True) for hh in heads]

    def body(t, carry):
        slot = lax.rem(t, 2)
        stats = [softmax_pv(hh, jnp.where(t == 0, qi, t - 1), slot, *carry[hh]) for hh in heads]
        return tuple(stats[hh] + (scores(hh, t, 1 - slot, False),) for hh in heads)

    init = tuple((jnp.full((1, n2), -jnp.inf, F32), jnp.zeros((8, n2), F32), cmx8_0[hh]) for hh in heads)
    carry = lax.fori_loop(0, qi, body, init)

    lv = lam_ref[...]
    lam = (jnp.exp(jnp.sum(lv[0:1] * lv[1:2], axis=1, keepdims=True))
           - jnp.exp(jnp.sum(lv[2:3] * lv[3:4], axis=1, keepdims=True)) + lambda_init)
    last = [softmax_pv(hh, jnp.maximum(qi - 1, 0), lax.rem(qi, 2), *carry[hh]) for hh in heads]
    for hh in heads:
        l = jnp.sum(last[hh][1], axis=0, keepdims=True)
        o1 = acc_ref[hh, :, 0:tq] / l[:, 0:tq]
        o2 = acc_ref[hh, :, tq:n2] / l[:, tq:n2]
        o = o1 - lam * o2
        ms = jnp.mean(o * o, axis=0, keepdims=True)
        o = o * lax.rsqrt(ms + NORM_EPS) * sg_ref[...] * (1.0 - lambda_init)
        o_ref[0, :, lanes[hh]] = o.T.astype(o_ref.dtype)


def _diff_attention(qkv, lam_vecs, subln_gain, *, lambda_init, tq):
    b, s, _ = qkv.shape
    hps = ATTN_HEADS_PER_STEP
    groups = DIFF_HEADS // hps
    width = hps * LANES
    kern = functools.partial(_diff_attn_kernel, tq=tq, lambda_init=lambda_init)
    return pl.pallas_call(
        kern,
        out_shape=jax.ShapeDtypeStruct((b, s, DIFF_HEADS * DIFF_V_DIM), BF16),
        grid=(b, groups, s // tq),
        in_specs=[
            pl.BlockSpec((4, DIFF_HEAD_DIM), lambda bi, gi, qi: (0, 0)),
            pl.BlockSpec((DIFF_V_DIM, 1), lambda bi, gi, qi: (0, 0)),
            pl.BlockSpec((1, tq, width), lambda bi, gi, qi: (bi, qi, gi)),
            pl.BlockSpec((1, s, width), lambda bi, gi, qi: (bi, 0, groups + gi)),
            pl.BlockSpec((1, s, width), lambda bi, gi, qi: (bi, 0, 2 * groups + gi)),
        ],
        out_specs=pl.BlockSpec((1, tq, width), lambda bi, gi, qi: (bi, qi, gi)),
        scratch_shapes=[
            pltpu.VMEM((hps, s // tq, DIFF_V_DIM, tq), BF16),
            pltpu.VMEM((hps, 2, tq, 2 * tq), F32),
            pltpu.VMEM((hps, DIFF_V_DIM, 2 * tq), F32),
        ],
        compiler_params=_cparams(("parallel", "parallel", "arbitrary")),
        name="diff_attn",
    )(lam_vecs, subln_gain, qkv, qkv, qkv)


def _proj_res_kernel(a_ref, w_ref, res_ref, o_ref):
    o_ref[...] = res_ref[...] + jnp.dot(a_ref[...], w_ref[...].astype(BF16), preferred_element_type=F32)


def _proj_res(a, w, res, *, tm):
    m, k = a.shape
    n = w.shape[1]
    return pl.pallas_call(
        _proj_res_kernel,
        out_shape=jax.ShapeDtypeStruct((m, n), F32),
        grid=(m // tm,),
        in_specs=[
            pl.BlockSpec((tm, k), lambda i: (i, 0)),
            pl.BlockSpec((k, n), lambda i: (0, 0)),
            pl.BlockSpec((tm, n), lambda i: (i, 0)),
        ],
        out_specs=pl.BlockSpec((tm, n), lambda i: (i, 0)),
        compiler_params=_cparams(("parallel",)),
        name="proj_res",
    )(a, w, res)


def _gla_gate_kernel(x_ref, g_ref, w1_ref, w2_ref, b_ref, o_ref):
    xn = _rms(x_ref[...], g_ref[...]).astype(BF16)
    t = jnp.dot(xn, w1_ref[...], preferred_element_type=F32)
    z = jnp.dot(t, w2_ref[...], preferred_element_type=F32, precision=lax.Precision.HIGHEST) + b_ref[...]
    log_sig = jnp.minimum(z, 0.0) - jnp.log1p(jnp.exp(-jnp.abs(z)))
    o_ref[...] = log_sig * (1.0 / GLA_GATE_TEMP)


def _gla_gate(x, gain, w1, w2, bias, *, tm):
    m, d = x.shape
    r = w1.shape[1]
    n = w2.shape[1]
    return pl.pallas_call(
        _gla_gate_kernel,
        out_shape=jax.ShapeDtypeStruct((m, n), F32),
        grid=(m // tm,),
        in_specs=[
            pl.BlockSpec((tm, d), lambda i: (i, 0)),
            pl.BlockSpec((1, d), lambda i: (0, 0)),
            pl.BlockSpec((d, r), lambda i: (0, 0)),
            pl.BlockSpec((r, n), lambda i: (0, 0)),
            pl.BlockSpec((1, n), lambda i: (0, 0)),
        ],
        out_specs=pl.BlockSpec((tm, n), lambda i: (i, 0)),
        compiler_params=_cparams(("parallel",)),
        name="gla_gate",
    )(x, gain, w1, w2, bias)


def _split3(x):
    hi = x.astype(BF16)
    r1 = x - hi.astype(F32)
    mid = r1.astype(BF16)
    lo = (r1 - mid.astype(F32)).astype(BF16)
    return hi, mid, lo


def _gla_kernel(q_ref, k_ref, v_ref, r_ref, g_ref, ng_ref, o_ref, state_ref):
    t_blk = q_ref.shape[1]
    hk = GLA_HEADS * GLA_DK

    @pl.when(pl.program_id(1) == 0)
    def _():
        state_ref[...] = jnp.zeros(state_ref.shape, F32)

    ri = lax.broadcasted_iota(jnp.int32, (t_blk, t_blk), 0)
    ci = lax.broadcasted_iota(jnp.int32, (t_blk, t_blk), 1)

    tri = jnp.where(ci <= ri, 1.0, 0.0).astype(BF16)
    g_hi, g_mid, g_lo = _split3(g_ref[0])
    b = (jnp.dot(tri, g_hi, preferred_element_type=F32)
         + jnp.dot(tri, g_mid, preferred_element_type=F32)
         + jnp.dot(tri, g_lo, preferred_element_type=F32))
    b = b * math.log2(math.e)

    q = q_ref[0].astype(F32) * (GLA_DK ** -0.5)
    k = k_ref[0].astype(F32)

    row = lax.broadcasted_iota(jnp.int32, (t_blk, hk), 0)
    b_last = b[t_blk - 1:t_blk, :]
    q_in = (q * jnp.exp2(b)).astype(BF16)
    k_out = (k * jnp.exp2(b_last - b)).astype(BF16)
    e_last = jnp.exp2(b_last)

    level = t_blk // 2
    levels = []
    while level >= GLA_DIAG:
        grp = 2 * level
        b3 = b.reshape(t_blk // grp, grp, hk)
        pivot = jnp.broadcast_to(b3[:, level - 1:level, :], b3.shape).reshape(t_blk, hk)
        upper = (row % grp) >= level
        e = jnp.exp2(-jnp.abs(b - pivot))
        qt = jnp.where(upper, q * e, 0.0).astype(BF16)
        kt = jnp.where(upper, 0.0, k * e).astype(BF16)
        levels.append((grp, qt, kt))
        level //= 2

    nd = t_blk // GLA_DIAG
    b3 = b.reshape(nd, GLA_DIAG, hk)
    q3 = q.reshape(nd, GLA_DIAG, hk)
    k3 = k.reshape(nd, GLA_DIAG, hk)
    sel_shape = (GLA_DIAG * GLA_DK, t_blk)
    sel = jnp.where(lax.broadcasted_iota(jnp.int32, sel_shape, 0) // GLA_DK
                    == lax.broadcasted_iota(jnp.int32, sel_shape, 1) % GLA_DIAG, 1.0, 0.0).astype(BF16)
    diag_mask = ((ri // GLA_DIAG) == (ci // GLA_DIAG)) & (ci <= ri)

    eye = (lax.broadcasted_iota(jnp.int32, (GLA_DK, GLA_DK), 0)
           == lax.broadcasted_iota(jnp.int32, (GLA_DK, GLA_DK), 1))

    diag_terms = []
    for j in range(GLA_DIAG):
        d = jnp.minimum(b3 - b3[:, j:j + 1, :], 0.0)
        pj = (jnp.exp2(d) * q3 * k3[:, j:j + 1, :]).reshape(t_blk, hk)
        diag_terms.append(pj)

    ng = ng_ref[...]
    for h in range(GLA_HEADS):
        ks = slice(h * GLA_DK, (h + 1) * GLA_DK)
        vs = slice(h * GLA_DV, (h + 1) * GLA_DV)
        a = jnp.zeros((t_blk, t_blk), F32)
        for grp, qt, kt in levels:
            s_l = lax.dot_general(qt[:, ks], kt[:, ks], (((1,), (1,)), ((), ())),
                                  preferred_element_type=F32)
            if grp == t_blk:
                a = a + s_l
            else:
                a = a + jnp.where((ri // grp) == (ci // grp), s_l, 0.0)
        stacked = jnp.concatenate([diag_terms[j][:, ks] for j in range(GLA_DIAG)], axis=1)
        a = jnp.where(diag_mask, jnp.dot(stacked.astype(BF16), sel, preferred_element_type=F32), a)

        v_h = v_ref[0, :, vs]
        state = state_ref[h]
        o = (jnp.dot(q_in[:, ks], state.astype(BF16), preferred_element_type=F32)
             + jnp.dot(a.astype(BF16), v_h, preferred_element_type=F32))

        e_col = jnp.sum(jnp.where(eye, jnp.broadcast_to(e_last[:, ks], (GLA_DK, GLA_DK)), 0.0),
                        axis=1, keepdims=True)
        upd = lax.dot_general(k_out[:, ks], v_h, (((0,), (0,)), ((), ())),
                              preferred_element_type=F32)
        state_ref[h] = e_col * state + upd

        r_h = r_ref[0, :, vs].astype(F32)
        gate = r_h / (1.0 + jnp.exp(-r_h))
        o_ref[0, :, vs] = (_rms(o, ng) * gate).astype(o_ref.dtype)


def _gla(proj, g, norm_gain):
    b, s, _ = proj.shape
    t = GLA_BLOCK
    hk = GLA_HEADS * GLA_DK
    hv = GLA_HEADS * GLA_DV
    return pl.pallas_call(
        _gla_kernel,
        out_shape=jax.ShapeDtypeStruct((b, s, hv), BF16),
        grid=(b, s // t),
        in_specs=[
            pl.BlockSpec((1, t, hk), lambda bi, ti: (bi, ti, 0)),
            pl.BlockSpec((1, t, hk), lambda bi, ti: (bi, ti, 1)),
            pl.BlockSpec((1, t, hv), lambda bi, ti: (bi, ti, 1)),
            pl.BlockSpec((1, t, hv), lambda bi, ti: (bi, ti, 2)),
            pl.BlockSpec((1, t, hk), lambda bi, ti: (bi, ti, 0)),
            pl.BlockSpec((1, GLA_DV), lambda bi, ti: (0, 0)),
        ],
        out_specs=pl.BlockSpec((1, t, hv), lambda bi, ti: (bi, ti, 0)),
        scratch_shapes=[pltpu.VMEM((GLA_HEADS, GLA_DK, GLA_DV), F32)],
        compiler_params=_cparams(("parallel", "arbitrary")),
        name="gla",
    )(proj, proj, proj, proj, g, norm_gain)


def _ffn_kernel(x_ref, halo_ref, g_ref, wg_ref, wu_ref, cwg_ref, cwu_ref, cbg_ref, cbu_ref, wd_ref,
                fg_ref, o_ref, xn_ref, acc_ref, slab_ref, *, tiles_per_seq, final_norm):
    i = pl.program_id(0)
    j = pl.program_id(1)
    nj = pl.num_programs(1)
    tm, d = x_ref.shape
    rows = tm // FFN_SUBTILES
    ext = rows + FFN_HALO
    seg = ext // 8
    nslab = d // LANES
    subs = range(FFN_SUBTILES)

    @pl.when(j == 0)
    def _():
        gain = g_ref[...]
        for s in subs:
            if s == 0:
                halo = jnp.where(i % tiles_per_seq == 0, 0.0, _rms(halo_ref[...], gain))
            else:
                halo = _rms(x_ref[s * rows - FFN_HALO:s * rows, :], gain)
            xn = _rms(x_ref[s * rows:(s + 1) * rows, :], gain)
            for c in range(nslab):
                slab_ref[c, 0:FFN_HALO, :] = halo[:, c * LANES:(c + 1) * LANES]
                slab_ref[c, FFN_HALO:ext, :] = xn[:, c * LANES:(c + 1) * LANES]
            for a2 in range(seg // 2):
                grp = [jnp.concatenate([slab_ref[c, pl.ds(2 * a2 + r, 8, stride=seg), :] for c in range(nslab)],
                                       axis=1) for r in range(2)]
                xn_ref[s, 16 * a2:16 * a2 + 16, :] = jnp.concatenate(grp, axis=0).astype(BF16)
        acc_ref[...] = jnp.zeros(acc_ref.shape, F32)

    def conv(u, cw_ref, cb_ref):
        cw = cw_ref[...]
        u3 = u.reshape(seg, 8, u.shape[-1])
        wrap1 = pltpu.roll(u3[seg - 1], 1, 0)[None]
        wrap2 = pltpu.roll(u3[seg - 2], 1, 0)[None]
        prev1 = jnp.concatenate([wrap1, u3[:seg - 1]], axis=0)
        prev2 = jnp.concatenate([wrap2, wrap1, u3[:seg - 2]], axis=0)
        return cw[2:3, :] * u3 + cw[1:2, :] * prev1 + cw[0:1, :] * prev2 + cb_ref[...]

    w_gate = wg_ref[...].astype(BF16)
    w_up = wu_ref[...].astype(BF16)
    w_down = wd_ref[...].astype(BF16)
    ups = [(jnp.dot(xn_ref[s], w_gate, preferred_element_type=F32),
            jnp.dot(xn_ref[s], w_up, preferred_element_type=F32)) for s in subs]
    acts = []
    for s in subs:
        gate = conv(ups[s][0], cwg_ref, cbg_ref)
        up = conv(ups[s][1], cwu_ref, cbu_ref)
        acts.append(((gate / (1.0 + jnp.exp(-gate))) * up).reshape(ext, gate.shape[-1]).astype(BF16))
    for s in subs:
        acc_ref[s] += jnp.dot(acts[s], w_down, preferred_element_type=F32)

    @pl.when(j == nj - 1)
    def _():
        for s in subs:
            for a in range(seg):
                for c in range(nslab):
                    slab_ref[c, pl.ds(a, 8, stride=seg), :] = acc_ref[s, 8 * a:8 * a + 8, c * LANES:(c + 1) * LANES]
            y = (x_ref[s * rows:(s + 1) * rows, :]
                 + jnp.concatenate([slab_ref[c, FFN_HALO:ext, :] for c in range(nslab)], axis=1))
            if final_norm:
                y = _rms(y, fg_ref[...])
            o_ref[s * rows:(s + 1) * rows, :] = y


def _ffn(x, gain, w_up, conv_w, conv_b, w_down, final_gain, *, tm, tf, seq, final_norm):
    m, d = x.shape
    f = w_down.shape[0]
    nf = f // tf
    tiles_per_seq = seq // tm
    halo_blocks = tm // FFN_HALO
    kern = functools.partial(_ffn_kernel, tiles_per_seq=tiles_per_seq, final_norm=final_norm)
    return pl.pallas_call(
        kern,
        out_shape=jax.ShapeDtypeStruct((m, d), F32),
        grid=(m // tm, nf),
        in_specs=[
            pl.BlockSpec((tm, d), lambda i, j: (i, 0)),
            pl.BlockSpec((FFN_HALO, d), lambda i, j: (jnp.maximum(i * halo_blocks - 1, 0), 0)),
            pl.BlockSpec((1, d), lambda i, j: (0, 0)),
            pl.BlockSpec((d, tf), lambda i, j: (0, j)),
            pl.BlockSpec((d, tf), lambda i, j: (0, nf + j)),
            pl.BlockSpec((CONV_WIDTH, tf), lambda i, j: (0, j)),
            pl.BlockSpec((CONV_WIDTH, tf), lambda i, j: (0, nf + j)),
            pl.BlockSpec((1, tf), lambda i, j: (0, j)),
            pl.BlockSpec((1, tf), lambda i, j: (0, nf + j)),
            pl.BlockSpec((tf, d), lambda i, j: (j, 0)),
            pl.BlockSpec((1, d), lambda i, j: (0, 0)),
        ],
        out_specs=pl.BlockSpec((tm, d), lambda i, j: (i, 0)),
        scratch_shapes=[pltpu.VMEM((FFN_SUBTILES, tm // FFN_SUBTILES + FFN_HALO, d), BF16),
                        pltpu.VMEM((FFN_SUBTILES, tm // FFN_SUBTILES + FFN_HALO, d), F32),
                        pltpu.VMEM((d // LANES, tm // FFN_SUBTILES + FFN_HALO, LANES), F32)],
        compiler_params=_cparams(("parallel", "arbitrary")),
        name="ffn",
    )(x, x, gain, w_up, w_up, conv_w, conv_w, conv_b, conv_b, w_down, final_gain)


def _rope_tables(seq, q_scale):
    half = DIFF_HEAD_DIM // 2
    inv = 1.0 / (ROPE_THETA ** (jnp.arange(0, DIFF_HEAD_DIM, 2, dtype=F32) / DIFF_HEAD_DIM))
    ang = jnp.arange(seq, dtype=F32)[:, None] * inv[None, :]
    cos, sin = jnp.cos(ang), jnp.sin(ang)
    cos_t = jnp.tile(cos, (1, LANES // half))
    sin_t = jnp.concatenate([-sin, -sin, sin, sin], axis=1)
    cos3 = jnp.stack([cos_t * q_scale, cos_t, jnp.ones_like(cos_t)])
    sin3 = jnp.stack([sin_t * q_scale, sin_t, jnp.zeros_like(sin_t)])
    return cos3, sin3


def _reorder_qk_columns(w_qkv):
    d = w_qkv.shape[0]
    width = DIFF_HEADS * DIFF_V_DIM
    half = DIFF_HEAD_DIM // 2
    qk = w_qkv[:, :2 * width].reshape(d, 2, DIFF_HEADS, 2, 2, half)
    qk = qk.transpose(0, 1, 2, 4, 3, 5).reshape(d, 2 * width)
    return jnp.concatenate([qk, w_qkv[:, 2 * width:]], axis=1)


def _trunk(x, norm_mix, norm_ffn, norm_final, diff_w_qkv, diff_w_o, diff_lambda, diff_subln,
           gla_w_in, gla_w_a1, gla_w_a2, gla_b_a, gla_norm, gla_w_o,
           ffn_w_up, ffn_conv_w, ffn_conv_b, ffn_w_down, *, tm, tq, tf):
    bsz, seq, d = x.shape
    depth = norm_mix.shape[0]
    m = bsz * seq
    tables = _rope_tables(seq, DIFF_HEAD_DIM ** -0.5 * math.log2(math.e))
    h = x.reshape(m, d)
    rank = gla_w_a1.shape[-1]
    for layer in range(depth):
        jdx = layer // 2
        gain = norm_mix[layer].reshape(1, d)
        if layer % 2 == 0:
            lambda_init = 0.8 - 0.6 * math.exp(-0.3 * layer)
            qkv = _norm_proj(h, gain, _reorder_qk_columns(diff_w_qkv[jdx]), tables,
                             tm=tm, tn=d, seq=seq)
            o = _diff_attention(qkv.reshape(bsz, seq, 3 * d), diff_lambda[jdx],
                                diff_subln[jdx].reshape(DIFF_V_DIM, 1),
                                lambda_init=lambda_init, tq=tq)
            h = _proj_res(o.reshape(m, d), diff_w_o[jdx], h, tm=tm)
        else:
            proj = _norm_proj(h, gain, gla_w_in[jdx], None, tm=tm, tn=d, seq=seq)
            w1 = jnp.pad(gla_w_a1[jdx], ((0, 0), (0, LANES - rank))).astype(BF16)
            w2 = jnp.pad(gla_w_a2[jdx], ((0, LANES - rank), (0, 0)))
            g = _gla_gate(h, gain, w1, w2, gla_b_a[jdx].reshape(1, -1), tm=tm)
            o = _gla(proj.reshape(bsz, seq, 3 * d), g.reshape(bsz, seq, -1),
                     gla_norm[jdx].reshape(1, GLA_DV))
            h = _proj_res(o.reshape(m, d), gla_w_o[jdx], h, tm=tm)
        h = _ffn(h, norm_ffn[layer].reshape(1, d), ffn_w_up[layer], ffn_conv_w[layer],
                 ffn_conv_b[layer].reshape(1, -1), ffn_w_down[layer],
                 norm_final.reshape(1, d), tm=tm, tf=tf, seq=seq, final_norm=(layer == depth - 1))
    return h.reshape(bsz, seq, d)


def kernel(x, norm_mix, norm_ffn, norm_final, diff_w_qkv, diff_w_o, diff_lambda, diff_subln, gla_w_in, gla_w_a1, gla_w_a2, gla_b_a, gla_norm, gla_w_o, ffn_w_up, ffn_conv_w, ffn_conv_b, ffn_w_down):
    return _trunk(x, norm_mix, norm_ffn, norm_final, diff_w_qkv, diff_w_o, diff_lambda, diff_subln,
                  gla_w_in, gla_w_a1, gla_w_a2, gla_b_a, gla_norm, gla_w_o,
                  ffn_w_up, ffn_conv_w, ffn_conv_b, ffn_w_down, tm=1024, tq=512, tf=256)
```

```python
import functools
import math

import jax
import jax.numpy as jnp
from jax import lax
from jax.experimental import pallas as pl
from jax.experimental.pallas import tpu as pltpu

F32 = jnp.float32
BF16 = jnp.bfloat16

NORM_EPS = 1e-6
ROPE_THETA = 10000.0

DIFF_HEADS = 8
DIFF_HEAD_DIM = 64
DIFF_V_DIM = 2 * DIFF_HEAD_DIM

GLA_HEADS = 4
GLA_DK = 128
GLA_DV = 256
GLA_GATE_TEMP = 16.0
GLA_BLOCK = 256
GLA_DIAG = 8

CONV_WIDTH = 3
FFN_HALO = 16
FFN_SUBTILES = 4

LANES = 128
VMEM_LIMIT_BYTES = 56 * 1024 * 1024


def _cparams(semantics):
    return pltpu.CompilerParams(dimension_semantics=semantics, vmem_limit_bytes=VMEM_LIMIT_BYTES)


def _rms(x, gain):
    ms = jnp.mean(x * x, axis=-1, keepdims=True)
    return x * lax.rsqrt(ms + NORM_EPS) * gain


def _norm_proj_kernel(x_ref, g_ref, w_ref, *rest, rope):
    j = pl.program_id(1)
    if rope:
        cos_ref, sin_ref, o_ref, xn_ref = rest
    else:
        o_ref, xn_ref = rest

    @pl.when(j == 0)
    def _():
        xn_ref[...] = _rms(x_ref[...], g_ref[...]).astype(BF16)

    acc = jnp.dot(xn_ref[...], w_ref[...].astype(BF16), preferred_element_type=F32)
    if not rope:
        o_ref[...] = acc.astype(o_ref.dtype)
        return
    c = cos_ref[0]
    s = sin_ref[0]
    for hh in range(acc.shape[1] // LANES):
        t = acc[:, hh * LANES:(hh + 1) * LANES]
        o_ref[:, hh * LANES:(hh + 1) * LANES] = (t * c + pltpu.roll(t, LANES // 2, 1) * s).astype(o_ref.dtype)


def _norm_proj(x, gain, w, layer, tables, *, tm, tn, seq):
    m, d = x.shape
    n = w.shape[2]
    nseq = seq // tm
    rope = tables is not None
    in_specs = [
        pl.BlockSpec((tm, d), lambda i, j: (i, 0)),
        pl.BlockSpec((1, d), lambda i, j: (0, 0)),
        pl.BlockSpec((None, d, tn), lambda i, j: (layer, 0, j)),
    ]
    if rope:
        assert tables[0].shape[0] == n // tn
        in_specs += [pl.BlockSpec((1, tm, LANES), lambda i, j: (j, i % nseq, 0))] * 2
    return pl.pallas_call(
        functools.partial(_norm_proj_kernel, rope=rope),
        out_shape=jax.ShapeDtypeStruct((m, n), BF16),
        grid=(m // tm, n // tn),
        in_specs=in_specs,
        out_specs=pl.BlockSpec((tm, tn), lambda i, j: (i, j)),
        scratch_shapes=[pltpu.VMEM((tm, d), BF16)],
        compiler_params=_cparams(("parallel", "arbitrary")),
        name="norm_proj",
    )(x, gain, w, *(tables if rope else ()))


ATTN_CHUNK = 256
ATTN_HEADS_PER_STEP = 4


def _diff_attn_kernel(lam_ref, sg_ref, q_ref, k_ref, v_ref, o_ref, vt_ref, s_ref, acc_ref, *, tq, lambda_init):
    qi = pl.program_id(2)
    n_kv_blocks = vt_ref.shape[1]
    n2 = 2 * tq
    ch = min(ATTN_CHUNK, tq)
    nch = tq // ch
    heads = range(ATTN_HEADS_PER_STEP)
    lanes = [slice(hh * LANES, (hh + 1) * LANES) for hh in heads]

    @pl.when(qi == 0)
    def _():
        for hh in heads:
            for c in range(n_kv_blocks):
                vt_ref[hh, c] = v_ref[0, c * tq:(c + 1) * tq, lanes[hh]].astype(F32).T.astype(BF16)

    qqt = []
    for hh in heads:
        qt = q_ref[0, :, lanes[hh]].astype(F32).T
        feat = lax.broadcasted_iota(jnp.int32, qt.shape, 0)
        map1 = (feat % DIFF_HEAD_DIM) < (DIFF_HEAD_DIM // 2)
        qqt.append(jnp.concatenate([jnp.where(map1, qt, 0.0), jnp.where(map1, 0.0, qt)], axis=1).astype(BF16))
    acc_ref[...] = jnp.zeros(acc_ref.shape, F32)

    def scores(hh, blk, slot, masked):
        cmx8 = None
        for c in range(nch):
            start = pl.multiple_of(blk * tq + c * ch, ch)
            s = jnp.dot(k_ref[0, pl.ds(start, ch), lanes[hh]], qqt[hh], preferred_element_type=F32)
            if masked:
                kv_pos = lax.broadcasted_iota(jnp.int32, s.shape, 0) + c * ch
                col = lax.broadcasted_iota(jnp.int32, s.shape, 1)
                q_pos = jnp.where(col >= tq, col - tq, col)
                s = jnp.where(kv_pos <= q_pos, s, -jnp.inf)
            s_ref[hh, slot, c * ch:(c + 1) * ch, :] = s
            part = jnp.max(s.reshape(ch // 8, 8, n2), axis=0)
            cmx8 = part if cmx8 is None else jnp.maximum(cmx8, part)
        return cmx8

    def softmax_pv(hh, blk, slot, m, l8, cmx8):
        m_new = jnp.maximum(m, jnp.max(cmx8, axis=0, keepdims=True))
        alpha = jnp.exp2(m - m_new)
        l8 = alpha * l8
        pv = None
        for c in range(nch):
            p = jnp.exp2(s_ref[hh, slot, c * ch:(c + 1) * ch, :] - m_new)
            l8 = l8 + jnp.sum(p.reshape(ch // 8, 8, n2), axis=0)
            d = jnp.dot(vt_ref[hh, blk, :, c * ch:(c + 1) * ch], p.astype(BF16), preferred_element_type=F32)
            pv = d if pv is None else pv + d
        acc_ref[hh] = alpha * acc_ref[hh] + pv
        return m_new, l8

    cmx8_0 = [scores(hh, qi, 0, True) for hh in heads]

    def body(t, carry):
        slot = lax.rem(t, 2)
        stats = [softmax_pv(hh, jnp.where(t == 0, qi, t - 1), slot, *carry[hh]) for hh in heads]
        return tuple(stats[hh] + (scores(hh, t, 1 - slot, False),) for hh in heads)

    init = tuple((jnp.full((1, n2), -jnp.inf, F32), jnp.zeros((8, n2), F32), cmx8_0[hh]) for hh in heads)
    carry = lax.fori_loop(0, qi, body, init)

    lv = lam_ref[...]
    lam = (jnp.exp(jnp.sum(lv[0:1] * lv[1:2], axis=1, keepdims=True))
           - jnp.exp(jnp.sum(lv[2:3] * lv[3:4], axis=1, keepdims=True)) + lambda_init)
    last = [softmax_pv(hh, jnp.maximum(qi - 1, 0), lax.rem(qi, 2), *carry[hh]) for hh in heads]
    for hh in heads:
        l = jnp.sum(last[hh][1], axis=0, keepdims=True)
        o1 = acc_ref[hh, :, 0:tq] / l[:, 0:tq]
        o2 = acc_ref[hh, :, tq:n2] / l[:, tq:n2]
        o = o1 - lam * o2
        ms = jnp.mean(o * o, axis=0, keepdims=True)
        o = o * lax.rsqrt(ms + NORM_EPS) * sg_ref[...] * (1.0 - lambda_init)
        o_ref[0, :, lanes[hh]] = o.T.astype(o_ref.dtype)


def _diff_attention(qkv, lam_vecs, subln_gain, *, lambda_init, tq):
    b, s, _ = qkv.shape
    hps = ATTN_HEADS_PER_STEP
    groups = DIFF_HEADS // hps
    width = hps * LANES
    kern = functools.partial(_diff_attn_kernel, tq=tq, lambda_init=lambda_init)
    return pl.pallas_call(
        kern,
        out_shape=jax.ShapeDtypeStruct((b, s, DIFF_HEADS * DIFF_V_DIM), BF16),
        grid=(b, groups, s // tq),
        in_specs=[
            pl.BlockSpec((4, DIFF_HEAD_DIM), lambda bi, gi, qi: (0, 0)),
            pl.BlockSpec((DIFF_V_DIM, 1), lambda bi, gi, qi: (0, 0)),
            pl.BlockSpec((1, tq, width), lambda bi, gi, qi: (bi, qi, gi)),
            pl.BlockSpec((1, s, width), lambda bi, gi, qi: (bi, 0, groups + gi)),
            pl.BlockSpec((1, s, width), lambda bi, gi, qi: (bi, 0, 2 * groups + gi)),
        ],
        out_specs=pl.BlockSpec((1, tq, width), lambda bi, gi, qi: (bi, qi, gi)),
        scratch_shapes=[
            pltpu.VMEM((hps, s // tq, DIFF_V_DIM, tq), BF16),
            pltpu.VMEM((hps, 2, tq, 2 * tq), F32),
            pltpu.VMEM((hps, DIFF_V_DIM, 2 * tq), F32),
        ],
        compiler_params=_cparams(("parallel", "parallel", "arbitrary")),
        name="diff_attn",
    )(lam_vecs, subln_gain, qkv, qkv, qkv)


def _proj_res_kernel(a_ref, w_ref, res_ref, o_ref):
    o_ref[...] = res_ref[...] + jnp.dot(a_ref[...], w_ref[...].astype(BF16), preferred_element_type=F32)


def _proj_res(a, w, layer, res, *, tm):
    m, k = a.shape
    n = w.shape[2]
    return pl.pallas_call(
        _proj_res_kernel,
        out_shape=jax.ShapeDtypeStruct((m, n), F32),
        grid=(m // tm,),
        in_specs=[
            pl.BlockSpec((tm, k), lambda i: (i, 0)),
            pl.BlockSpec((None, k, n), lambda i: (layer, 0, 0)),
            pl.BlockSpec((tm, n), lambda i: (i, 0)),
        ],
        out_specs=pl.BlockSpec((tm, n), lambda i: (i, 0)),
        compiler_params=_cparams(("parallel",)),
        name="proj_res",
    )(a, w, res)


def _gla_gate_kernel(x_ref, g_ref, w1_ref, w2_ref, b_ref, o_ref):
    xn = _rms(x_ref[...], g_ref[...]).astype(BF16)
    t = jnp.dot(xn, w1_ref[...], preferred_element_type=F32)
    z = jnp.dot(t, w2_ref[...], preferred_element_type=F32, precision=lax.Precision.HIGHEST) + b_ref[...]
    log_sig = jnp.minimum(z, 0.0) - jnp.log1p(jnp.exp(-jnp.abs(z)))
    o_ref[...] = log_sig * (1.0 / GLA_GATE_TEMP)


def _gla_gate(x, gain, w1, w2, bias, *, tm):
    m, d = x.shape
    r = w1.shape[1]
    n = w2.shape[1]
    return pl.pallas_call(
        _gla_gate_kernel,
        out_shape=jax.ShapeDtypeStruct((m, n), F32),
        grid=(m // tm,),
        in_specs=[
            pl.BlockSpec((tm, d), lambda i: (i, 0)),
            pl.BlockSpec((1, d), lambda i: (0, 0)),
            pl.BlockSpec((d, r), lambda i: (0, 0)),
            pl.BlockSpec((r, n), lambda i: (0, 0)),
            pl.BlockSpec((1, n), lambda i: (0, 0)),
        ],
        out_specs=pl.BlockSpec((tm, n), lambda i: (i, 0)),
        compiler_params=_cparams(("parallel",)),
        name="gla_gate",
    )(x, gain, w1, w2, bias)


def _split3(x):
    hi = x.astype(BF16)
    r1 = x - hi.astype(F32)
    mid = r1.astype(BF16)
    lo = (r1 - mid.astype(F32)).astype(BF16)
    return hi, mid, lo


def _gla_kernel(q_ref, k_ref, v_ref, r_ref, g_ref, ng_ref, o_ref, state_ref):
    t_blk = q_ref.shape[1]
    hk = GLA_HEADS * GLA_DK

    @pl.when(pl.program_id(1) == 0)
    def _():
        state_ref[...] = jnp.zeros(state_ref.shape, F32)

    ri = lax.broadcasted_iota(jnp.int32, (t_blk, t_blk), 0)
    ci = lax.broadcasted_iota(jnp.int32, (t_blk, t_blk), 1)

    tri = jnp.where(ci <= ri, 1.0, 0.0).astype(BF16)
    g_hi, g_mid, g_lo = _split3(g_ref[0])
    b = (jnp.dot(tri, g_hi, preferred_element_type=F32)
         + jnp.dot(tri, g_mid, preferred_element_type=F32)
         + jnp.dot(tri, g_lo, preferred_element_type=F32))
    b = b * math.log2(math.e)

    q = q_ref[0].astype(F32) * (GLA_DK ** -0.5)
    k = k_ref[0].astype(F32)

    row = lax.broadcasted_iota(jnp.int32, (t_blk, hk), 0)
    b_last = b[t_blk - 1:t_blk, :]
    q_in = (q * jnp.exp2(b)).astype(BF16)
    k_out = (k * jnp.exp2(b_last - b)).astype(BF16)
    e_last = jnp.exp2(b_last)

    level = t_blk // 2
    levels = []
    while level >= GLA_DIAG:
        grp = 2 * level
        b3 = b.reshape(t_blk // grp, grp, hk)
        pivot = jnp.broadcast_to(b3[:, level - 1:level, :], b3.shape).reshape(t_blk, hk)
        upper = (row % grp) >= level
        e = jnp.exp2(-jnp.abs(b - pivot))
        qt = jnp.where(upper, q * e, 0.0).astype(BF16)
        kt = jnp.where(upper, 0.0, k * e).astype(BF16)
        levels.append((grp, qt, kt))
        level //= 2

    nd = t_blk // GLA_DIAG
    b3 = b.reshape(nd, GLA_DIAG, hk)
    q3 = q.reshape(nd, GLA_DIAG, hk)
    k3 = k.reshape(nd, GLA_DIAG, hk)
    sel_shape = (GLA_DIAG * GLA_DK, t_blk)
    sel = jnp.where(lax.broadcasted_iota(jnp.int32, sel_shape, 0) // GLA_DK
                    == lax.broadcasted_iota(jnp.int32, sel_shape, 1) % GLA_DIAG, 1.0, 0.0).astype(BF16)
    diag_mask = ((ri // GLA_DIAG) == (ci // GLA_DIAG)) & (ci <= ri)

    eye = (lax.broadcasted_iota(jnp.int32, (GLA_DK, GLA_DK), 0)
           == lax.broadcasted_iota(jnp.int32, (GLA_DK, GLA_DK), 1))

    diag_terms = []
    for j in range(GLA_DIAG):
        d = jnp.minimum(b3 - b3[:, j:j + 1, :], 0.0)
        pj = (jnp.exp2(d) * q3 * k3[:, j:j + 1, :]).reshape(t_blk, hk)
        diag_terms.append(pj)

    ng = ng_ref[...]
    for h in range(GLA_HEADS):
        ks = slice(h * GLA_DK, (h + 1) * GLA_DK)
        vs = slice(h * GLA_DV, (h + 1) * GLA_DV)
        a = jnp.zeros((t_blk, t_blk), F32)
        for grp, qt, kt in levels:
            s_l = lax.dot_general(qt[:, ks], kt[:, ks], (((1,), (1,)), ((), ())),
                                  preferred_element_type=F32)
            if grp == t_blk:
                a = a + s_l
            else:
                a = a + jnp.where((ri // grp) == (ci // grp), s_l, 0.0)
        stacked = jnp.concatenate([diag_terms[j][:, ks] for j in range(GLA_DIAG)], axis=1)
        a = jnp.where(diag_mask, jnp.dot(stacked.astype(BF16), sel, preferred_element_type=F32), a)

        v_h = v_ref[0, :, vs]
        state = state_ref[h]
        o = (jnp.dot(q_in[:, ks], state.astype(BF16), preferred_element_type=F32)
             + jnp.dot(a.astype(BF16), v_h, preferred_element_type=F32))

        e_col = jnp.sum(jnp.where(eye, jnp.broadcast_to(e_last[:, ks], (GLA_DK, GLA_DK)), 0.0),
                        axis=1, keepdims=True)
        upd = lax.dot_general(k_out[:, ks], v_h, (((0,), (0,)), ((), ())),
                              preferred_element_type=F32)
        state_ref[h] = e_col * state + upd

        r_h = r_ref[0, :, vs].astype(F32)
        gate = r_h / (1.0 + jnp.exp(-r_h))
        o_ref[0, :, vs] = (_rms(o, ng) * gate).astype(o_ref.dtype)


def _gla(proj, g, norm_gain):
    b, s, _ = proj.shape
    t = GLA_BLOCK
    hk = GLA_HEADS * GLA_DK
    hv = GLA_HEADS * GLA_DV
    return pl.pallas_call(
        _gla_kernel,
        out_shape=jax.ShapeDtypeStruct((b, s, hv), BF16),
        grid=(b, s // t),
        in_specs=[
            pl.BlockSpec((1, t, hk), lambda bi, ti: (bi, ti, 0)),
            pl.BlockSpec((1, t, hk), lambda bi, ti: (bi, ti, 1)),
            pl.BlockSpec((1, t, hv), lambda bi, ti: (bi, ti, 1)),
            pl.BlockSpec((1, t, hv), lambda bi, ti: (bi, ti, 2)),
            pl.BlockSpec((1, t, hk), lambda bi, ti: (bi, ti, 0)),
            pl.BlockSpec((1, GLA_DV), lambda bi, ti: (0, 0)),
        ],
        out_specs=pl.BlockSpec((1, t, hv), lambda bi, ti: (bi, ti, 0)),
        scratch_shapes=[pltpu.VMEM((GLA_HEADS, GLA_DK, GLA_DV), F32)],
        compiler_params=_cparams(("parallel", "arbitrary")),
        name="gla",
    )(proj, proj, proj, proj, g, norm_gain)


def _ffn_kernel(x_ref, halo_ref, g_ref, wg_ref, wu_ref, cwg_ref, cwu_ref, cbg_ref, cbu_ref, wd_ref,
                fg_ref, o_ref, xn_ref, acc_ref, slab_ref, *, tiles_per_seq, final_norm):
    i = pl.program_id(0)
    j = pl.program_id(1)
    nj = pl.num_programs(1)
    tm, d = x_ref.shape
    rows = tm // FFN_SUBTILES
    ext = rows + FFN_HALO
    seg = ext // 8
    nslab = d // LANES
    subs = range(FFN_SUBTILES)

    @pl.when(j == 0)
    def _():
        gain = g_ref[...]
        for s in subs:
            if s == 0:
                halo = jnp.where(i % tiles_per_seq == 0, 0.0, _rms(halo_ref[...], gain))
            else:
                halo = _rms(x_ref[s * rows - FFN_HALO:s * rows, :], gain)
            xn = _rms(x_ref[s * rows:(s + 1) * rows, :], gain)
            for c in range(nslab):
                slab_ref[c, 0:FFN_HALO, :] = halo[:, c * LANES:(c + 1) * LANES]
                slab_ref[c, FFN_HALO:ext, :] = xn[:, c * LANES:(c + 1) * LANES]
            for a2 in range(seg // 2):
                grp = [jnp.concatenate([slab_ref[c, pl.ds(2 * a2 + r, 8, stride=seg), :] for c in range(nslab)],
                                       axis=1) for r in range(2)]
                xn_ref[s, 16 * a2:16 * a2 + 16, :] = jnp.concatenate(grp, axis=0).astype(BF16)
        acc_ref[...] = jnp.zeros(acc_ref.shape, F32)

    def conv(u, cw_ref, cb_ref):
        cw = cw_ref[...]
        u3 = u.reshape(seg, 8, u.shape[-1])
        wrap1 = pltpu.roll(u3[seg - 1], 1, 0)[None]
        wrap2 = pltpu.roll(u3[seg - 2], 1, 0)[None]
        prev1 = jnp.concatenate([wrap1, u3[:seg - 1]], axis=0)
        prev2 = jnp.concatenate([wrap2, wrap1, u3[:seg - 2]], axis=0)
        return cw[2:3, :] * u3 + cw[1:2, :] * prev1 + cw[0:1, :] * prev2 + cb_ref[...]

    w_gate = wg_ref[...].astype(BF16)
    w_up = wu_ref[...].astype(BF16)
    w_down = wd_ref[...].astype(BF16)
    ups = [(jnp.dot(xn_ref[s], w_gate, preferred_element_type=F32),
            jnp.dot(xn_ref[s], w_up, preferred_element_type=F32)) for s in subs]
    acts = []
    for s in subs:
        gate = conv(ups[s][0], cwg_ref, cbg_ref)
        up = conv(ups[s][1], cwu_ref, cbu_ref)
        acts.append(((gate / (1.0 + jnp.exp(-gate))) * up).reshape(ext, gate.shape[-1]).astype(BF16))
    for s in subs:
        acc_ref[s] += jnp.dot(acts[s], w_down, preferred_element_type=F32)

    @pl.when(j == nj - 1)
    def _():
        for s in subs:
            for a in range(seg):
                for c in range(nslab):
                    slab_ref[c, pl.ds(a, 8, stride=seg), :] = acc_ref[s, 8 * a:8 * a + 8, c * LANES:(c + 1) * LANES]
            y = (x_ref[s * rows:(s + 1) * rows, :]
                 + jnp.concatenate([slab_ref[c, FFN_HALO:ext, :] for c in range(nslab)], axis=1))
            if final_norm:
                y = _rms(y, fg_ref[...])
            o_ref[s * rows:(s + 1) * rows, :] = y


def _ffn(x, gain, w_up, conv_w, conv_b, w_down, layer, final_gain, *, tm, tf, seq, final_norm):
    m, d = x.shape
    f = w_down.shape[1]
    nf = f // tf
    tiles_per_seq = seq // tm
    halo_blocks = tm // FFN_HALO
    kern = functools.partial(_ffn_kernel, tiles_per_seq=tiles_per_seq, final_norm=final_norm)
    return pl.pallas_call(
        kern,
        out_shape=jax.ShapeDtypeStruct((m, d), F32),
        grid=(m // tm, nf),
        in_specs=[
            pl.BlockSpec((tm, d), lambda i, j: (i, 0)),
            pl.BlockSpec((FFN_HALO, d), lambda i, j: (jnp.maximum(i * halo_blocks - 1, 0), 0)),
            pl.BlockSpec((None, 1, d), lambda i, j: (layer, 0, 0)),
            pl.BlockSpec((None, d, tf), lambda i, j: (layer, 0, j)),
            pl.BlockSpec((None, d, tf), lambda i, j: (layer, 0, nf + j)),
            pl.BlockSpec((None, CONV_WIDTH, tf), lambda i, j: (layer, 0, j)),
            pl.BlockSpec((None, CONV_WIDTH, tf), lambda i, j: (layer, 0, nf + j)),
            pl.BlockSpec((None, 1, tf), lambda i, j: (layer, 0, j)),
            pl.BlockSpec((None, 1, tf), lambda i, j: (layer, 0, nf + j)),
            pl.BlockSpec((None, tf, d), lambda i, j: (layer, j, 0)),
            pl.BlockSpec((1, d), lambda i, j: (0, 0)),
        ],
        out_specs=pl.BlockSpec((tm, d), lambda i, j: (i, 0)),
        scratch_shapes=[pltpu.VMEM((FFN_SUBTILES, tm // FFN_SUBTILES + FFN_HALO, d), BF16),
                        pltpu.VMEM((FFN_SUBTILES, tm // FFN_SUBTILES + FFN_HALO, d), F32),
                        pltpu.VMEM((d // LANES, tm // FFN_SUBTILES + FFN_HALO, LANES), F32)],
        compiler_params=_cparams(("parallel", "arbitrary")),
        name="ffn",
    )(x, x, gain, w_up, w_up, conv_w, conv_w, conv_b, conv_b, w_down, final_gain)


def _rope_tables(seq, q_scale):
    half = DIFF_HEAD_DIM // 2
    inv = 1.0 / (ROPE_THETA ** (jnp.arange(0, DIFF_HEAD_DIM, 2, dtype=F32) / DIFF_HEAD_DIM))
    ang = jnp.arange(seq, dtype=F32)[:, None] * inv[None, :]
    cos, sin = jnp.cos(ang), jnp.sin(ang)
    cos_t = jnp.tile(cos, (1, LANES // half))
    sin_t = jnp.concatenate([-sin, -sin, sin, sin], axis=1)
    cos3 = jnp.stack([cos_t * q_scale, cos_t, jnp.ones_like(cos_t)])
    sin3 = jnp.stack([sin_t * q_scale, sin_t, jnp.zeros_like(sin_t)])
    return cos3, sin3


def _reorder_qk_columns(w_qkv):
    layers, d, _ = w_qkv.shape
    width = DIFF_HEADS * DIFF_V_DIM
    half = DIFF_HEAD_DIM // 2
    qk = w_qkv[:, :, :2 * width].reshape(layers, d, 2, DIFF_HEADS, 2, 2, half)
    qk = qk.transpose(0, 1, 2, 3, 5, 4, 6).reshape(layers, d, 2 * width)
    return jnp.concatenate([qk, w_qkv[:, :, 2 * width:]], axis=2).astype(BF16)


def _trunk(x, norm_mix, norm_ffn, norm_final, diff_w_qkv, diff_w_o, diff_lambda, diff_subln,
           gla_w_in, gla_w_a1, gla_w_a2, gla_b_a, gla_norm, gla_w_o,
           ffn_w_up, ffn_conv_w, ffn_conv_b, ffn_w_down, *, tm, tq, tf):
    bsz, seq, d = x.shape
    depth = norm_mix.shape[0]
    m = bsz * seq
    tables = _rope_tables(seq, DIFF_HEAD_DIM ** -0.5 * math.log2(math.e))
    w_qkv = _reorder_qk_columns(diff_w_qkv)
    h = x.reshape(m, d)
    rank = gla_w_a1.shape[-1]
    for layer in range(depth):
        jdx = layer // 2
        gain = norm_mix[layer].reshape(1, d)
        if layer % 2 == 0:
            lambda_init = 0.8 - 0.6 * math.exp(-0.3 * layer)
            qkv = _norm_proj(h, gain, w_qkv, jdx, tables, tm=tm, tn=d, seq=seq)
            o = _diff_attention(qkv.reshape(bsz, seq, 3 * d), diff_lambda[jdx],
                                diff_subln[jdx].reshape(DIFF_V_DIM, 1),
                                lambda_init=lambda_init, tq=tq)
            h = _proj_res(o.reshape(m, d), diff_w_o, jdx, h, tm=tm)
        else:
            proj = _norm_proj(h, gain, gla_w_in, jdx, None, tm=tm, tn=d, seq=seq)
            w1 = jnp.pad(gla_w_a1[jdx], ((0, 0), (0, LANES - rank))).astype(BF16)
            w2 = jnp.pad(gla_w_a2[jdx], ((0, LANES - rank), (0, 0)))
            g = _gla_gate(h, gain, w1, w2, gla_b_a[jdx].reshape(1, -1), tm=tm)
            o = _gla(proj.reshape(bsz, seq, 3 * d), g.reshape(bsz, seq, -1),
                     gla_norm[jdx].reshape(1, GLA_DV))
            h = _proj_res(o.reshape(m, d), gla_w_o, jdx, h, tm=tm)
        h = _ffn(h, norm_ffn.reshape(depth, 1, d), ffn_w_up, ffn_conv_w, ffn_conv_b.reshape(depth, 1, -1),
                 ffn_w_down, layer, norm_final.reshape(1, d), tm=tm, tf=tf, seq=seq,
                 final_norm=(layer == depth - 1))
    return h.reshape(bsz, seq, d)


def kernel(x, norm_mix, norm_ffn, norm_final, diff_w_qkv, diff_w_o, diff_lambda, diff_subln, gla_w_in, gla_w_a1, gla_w_a2, gla_b_a, gla_norm, gla_w_o, ffn_w_up, ffn_conv_w, ffn_conv_b, ffn_w_down):
    return _trunk(x, norm_mix, norm_ffn, norm_final, diff_w_qkv, diff_w_o, diff_lambda, diff_subln,
                  gla_w_in, gla_w_a1, gla_w_a2, gla_b_a, gla_norm, gla_w_o,
                  ffn_w_up, ffn_conv_w, ffn_conv_b, ffn_w_down, tm=1024, tq=512, tf=256)
```

```python
import functools
import math

import jax
import jax.numpy as jnp
from jax import lax
from jax.experimental import pallas as pl
from jax.experimental.pallas import tpu as pltpu

F32 = jnp.float32
BF16 = jnp.bfloat16

NORM_EPS = 1e-6
ROPE_THETA = 10000.0

DIFF_HEADS = 8
DIFF_HEAD_DIM = 64
DIFF_V_DIM = 2 * DIFF_HEAD_DIM

GLA_HEADS = 4
GLA_DK = 128
GLA_DV = 256
GLA_GATE_TEMP = 16.0
GLA_BLOCK = 256
GLA_DIAG = 8

CONV_WIDTH = 3
FFN_HALO = 16
FFN_SUBTILES = 4

LANES = 128
VMEM_LIMIT_BYTES = 56 * 1024 * 1024


def _cparams(semantics):
    return pltpu.CompilerParams(dimension_semantics=semantics, vmem_limit_bytes=VMEM_LIMIT_BYTES)


def _rms(x, gain):
    ms = jnp.mean(x * x, axis=-1, keepdims=True)
    return x * lax.rsqrt(ms + NORM_EPS) * gain


def _norm_proj_kernel(x_ref, g_ref, w_ref, *rest, rope):
    j = pl.program_id(1)
    if rope:
        cos_ref, sin_ref, o_ref, xn_ref = rest
    else:
        o_ref, xn_ref = rest

    @pl.when(j == 0)
    def _():
        xn_ref[...] = _rms(x_ref[...], g_ref[...]).astype(BF16)

    acc = jnp.dot(xn_ref[...], w_ref[...].astype(BF16), preferred_element_type=F32)
    if not rope:
        o_ref[...] = acc.astype(o_ref.dtype)
        return
    c = cos_ref[0]
    s = sin_ref[0]
    for hh in range(acc.shape[1] // LANES):
        t = acc[:, hh * LANES:(hh + 1) * LANES]
        o_ref[:, hh * LANES:(hh + 1) * LANES] = (t * c + pltpu.roll(t, LANES // 2, 1) * s).astype(o_ref.dtype)


def _norm_proj(x, gain, w, layer, tables, *, tm, tn, seq):
    m, d = x.shape
    n = w.shape[2]
    nseq = seq // tm
    rope = tables is not None
    in_specs = [
        pl.BlockSpec((tm, d), lambda i, j: (i, 0)),
        pl.BlockSpec((1, d), lambda i, j: (0, 0)),
        pl.BlockSpec((None, d, tn), lambda i, j: (layer, 0, j)),
    ]
    if rope:
        assert tables[0].shape[0] == n // tn
        in_specs += [pl.BlockSpec((1, tm, LANES), lambda i, j: (j, i % nseq, 0))] * 2
    return pl.pallas_call(
        functools.partial(_norm_proj_kernel, rope=rope),
        out_shape=jax.ShapeDtypeStruct((m, n), BF16),
        grid=(m // tm, n // tn),
        in_specs=in_specs,
        out_specs=pl.BlockSpec((tm, tn), lambda i, j: (i, j)),
        scratch_shapes=[pltpu.VMEM((tm, d), BF16)],
        compiler_params=_cparams(("parallel", "arbitrary")),
        name="norm_proj",
    )(x, gain, w, *(tables if rope else ()))


ATTN_CHUNK = 256
ATTN_HEADS_PER_STEP = 4
ATTN_V_ROWS = DIFF_V_DIM + 16


def _diff_attn_kernel(lam_ref, sg_ref, q_ref, k_ref, v_ref, o_ref, vt_ref, s_ref, acc_ref, *, tq, lambda_init):
    qi = pl.program_id(2)
    n_kv_blocks = vt_ref.shape[1]
    n2 = 2 * tq
    ch = min(ATTN_CHUNK, tq)
    nch = tq // ch
    heads = range(ATTN_HEADS_PER_STEP)
    lanes = [slice(hh * LANES, (hh + 1) * LANES) for hh in heads]

    @pl.when(qi == 0)
    def _():
        for hh in heads:
            for c in range(n_kv_blocks):
                vt_ref[hh, c, 0:DIFF_V_DIM, :] = v_ref[0, c * tq:(c + 1) * tq, lanes[hh]].astype(F32).T.astype(BF16)
                ones_row = lax.broadcasted_iota(jnp.int32, (ATTN_V_ROWS - DIFF_V_DIM, tq), 0) == 0
                vt_ref[hh, c, DIFF_V_DIM:ATTN_V_ROWS, :] = jnp.where(ones_row, 1.0, 0.0).astype(BF16)

    qqt = []
    for hh in heads:
        qt = q_ref[0, :, lanes[hh]].astype(F32).T
        feat = lax.broadcasted_iota(jnp.int32, qt.shape, 0)
        map1 = (feat % DIFF_HEAD_DIM) < (DIFF_HEAD_DIM // 2)
        qqt.append(jnp.concatenate([jnp.where(map1, qt, 0.0), jnp.where(map1, 0.0, qt)], axis=1).astype(BF16))
    acc_ref[...] = jnp.zeros(acc_ref.shape, F32)

    def scores(hh, blk, slot, masked):
        cmx8 = None
        for c in range(nch):
            start = pl.multiple_of(blk * tq + c * ch, ch)
            s = jnp.dot(k_ref[0, pl.ds(start, ch), lanes[hh]], qqt[hh], preferred_element_type=F32)
            if masked:
                kv_pos = lax.broadcasted_iota(jnp.int32, s.shape, 0) + c * ch
                col = lax.broadcasted_iota(jnp.int32, s.shape, 1)
                q_pos = jnp.where(col >= tq, col - tq, col)
                s = jnp.where(kv_pos <= q_pos, s, -jnp.inf)
            s_ref[hh, slot, c * ch:(c + 1) * ch, :] = s
            part = jnp.max(s.reshape(ch // 8, 8, n2), axis=0)
            cmx8 = part if cmx8 is None else jnp.maximum(cmx8, part)
        return cmx8

    def softmax_pv(hh, blk, slot, m, cmx8):
        m_new = jnp.maximum(m, jnp.max(cmx8, axis=0, keepdims=True))
        alpha = jnp.exp2(m - m_new)
        pv = None
        for c in range(nch):
            p = jnp.exp2((s_ref[hh, slot, c * ch:(c + 1) * ch, :] - m_new).astype(BF16))
            d = jnp.dot(vt_ref[hh, blk, :, c * ch:(c + 1) * ch], p, preferred_element_type=F32)
            pv = d if pv is None else pv + d
        acc_ref[hh] = alpha * acc_ref[hh] + pv
        return (m_new,)

    cmx8_0 = [scores(hh, qi, 0, True) for hh in heads]

    def body(t, carry):
        slot = lax.rem(t, 2)
        stats = [softmax_pv(hh, jnp.where(t == 0, qi, t - 1), slot, *carry[hh]) for hh in heads]
        return tuple(stats[hh] + (scores(hh, t, 1 - slot, False),) for hh in heads)

    init = tuple((jnp.full((1, n2), -jnp.inf, F32), cmx8_0[hh]) for hh in heads)
    carry = lax.fori_loop(0, qi, body, init)

    lv = lam_ref[...]
    lam = (jnp.exp(jnp.sum(lv[0:1] * lv[1:2], axis=1, keepdims=True))
           - jnp.exp(jnp.sum(lv[2:3] * lv[3:4], axis=1, keepdims=True)) + lambda_init)
    for hh in heads:
        softmax_pv(hh, jnp.maximum(qi - 1, 0), lax.rem(qi, 2), *carry[hh])
    for hh in heads:
        l = acc_ref[hh, DIFF_V_DIM:DIFF_V_DIM + 1, :]
        o1 = acc_ref[hh, 0:DIFF_V_DIM, 0:tq] / l[:, 0:tq]
        o2 = acc_ref[hh, 0:DIFF_V_DIM, tq:n2] / l[:, tq:n2]
        o = o1 - lam * o2
        ms = jnp.mean(o * o, axis=0, keepdims=True)
        o = o * lax.rsqrt(ms + NORM_EPS) * sg_ref[...] * (1.0 - lambda_init)
        o_ref[0, :, lanes[hh]] = o.T.astype(o_ref.dtype)


def _diff_attention(qkv, lam_vecs, subln_gain, *, lambda_init, tq):
    b, s, _ = qkv.shape
    hps = ATTN_HEADS_PER_STEP
    groups = DIFF_HEADS // hps
    width = hps * LANES
    kern = functools.partial(_diff_attn_kernel, tq=tq, lambda_init=lambda_init)
    return pl.pallas_call(
        kern,
        out_shape=jax.ShapeDtypeStruct((b, s, DIFF_HEADS * DIFF_V_DIM), BF16),
        grid=(b, groups, s // tq),
        in_specs=[
            pl.BlockSpec((4, DIFF_HEAD_DIM), lambda bi, gi, qi: (0, 0)),
            pl.BlockSpec((DIFF_V_DIM, 1), lambda bi, gi, qi: (0, 0)),
            pl.BlockSpec((1, tq, width), lambda bi, gi, qi: (bi, qi, gi)),
            pl.BlockSpec((1, s, width), lambda bi, gi, qi: (bi, 0, groups + gi)),
            pl.BlockSpec((1, s, width), lambda bi, gi, qi: (bi, 0, 2 * groups + gi)),
        ],
        out_specs=pl.BlockSpec((1, tq, width), lambda bi, gi, qi: (bi, qi, gi)),
        scratch_shapes=[
            pltpu.VMEM((hps, s // tq, ATTN_V_ROWS, tq), BF16),
            pltpu.VMEM((hps, 2, tq, 2 * tq), F32),
            pltpu.VMEM((hps, ATTN_V_ROWS, 2 * tq), F32),
        ],
        compiler_params=_cparams(("parallel", "parallel", "arbitrary")),
        name="diff_attn",
    )(lam_vecs, subln_gain, qkv, qkv, qkv)


def _proj_res_kernel(a_ref, w_ref, res_ref, o_ref):
    o_ref[...] = res_ref[...] + jnp.dot(a_ref[...], w_ref[...].astype(BF16), preferred_element_type=F32)


def _proj_res(a, w, layer, res, *, tm):
    m, k = a.shape
    n = w.shape[2]
    return pl.pallas_call(
        _proj_res_kernel,
        out_shape=jax.ShapeDtypeStruct((m, n), F32),
        grid=(m // tm,),
        in_specs=[
            pl.BlockSpec((tm, k), lambda i: (i, 0)),
            pl.BlockSpec((None, k, n), lambda i: (layer, 0, 0)),
            pl.BlockSpec((tm, n), lambda i: (i, 0)),
        ],
        out_specs=pl.BlockSpec((tm, n), lambda i: (i, 0)),
        compiler_params=_cparams(("parallel",)),
        name="proj_res",
    )(a, w, res)


def _gla_gate_kernel(x_ref, g_ref, w1_ref, w2_ref, b_ref, o_ref):
    xn = _rms(x_ref[...], g_ref[...]).astype(BF16)
    t = jnp.dot(xn, w1_ref[...], preferred_element_type=F32)
    z = jnp.dot(t, w2_ref[...], preferred_element_type=F32, precision=lax.Precision.HIGHEST) + b_ref[...]
    log_sig = jnp.minimum(z, 0.0) - jnp.log1p(jnp.exp(-jnp.abs(z)))
    o_ref[...] = log_sig * (1.0 / GLA_GATE_TEMP)


def _gla_gate(x, gain, w1, w2, bias, *, tm):
    m, d = x.shape
    r = w1.shape[1]
    n = w2.shape[1]
    return pl.pallas_call(
        _gla_gate_kernel,
        out_shape=jax.ShapeDtypeStruct((m, n), F32),
        grid=(m // tm,),
        in_specs=[
            pl.BlockSpec((tm, d), lambda i: (i, 0)),
            pl.BlockSpec((1, d), lambda i: (0, 0)),
            pl.BlockSpec((d, r), lambda i: (0, 0)),
            pl.BlockSpec((r, n), lambda i: (0, 0)),
            pl.BlockSpec((1, n), lambda i: (0, 0)),
        ],
        out_specs=pl.BlockSpec((tm, n), lambda i: (i, 0)),
        compiler_params=_cparams(("parallel",)),
        name="gla_gate",
    )(x, gain, w1, w2, bias)


def _split3(x):
    hi = x.astype(BF16)
    r1 = x - hi.astype(F32)
    mid = r1.astype(BF16)
    lo = (r1 - mid.astype(F32)).astype(BF16)
    return hi, mid, lo


def _gla_kernel(q_ref, k_ref, v_ref, r_ref, g_ref, ng_ref, o_ref, state_ref):
    t_blk = q_ref.shape[1]
    hk = GLA_HEADS * GLA_DK

    @pl.when(pl.program_id(1) == 0)
    def _():
        state_ref[...] = jnp.zeros(state_ref.shape, F32)

    ri = lax.broadcasted_iota(jnp.int32, (t_blk, t_blk), 0)
    ci = lax.broadcasted_iota(jnp.int32, (t_blk, t_blk), 1)

    tri = jnp.where(ci <= ri, 1.0, 0.0).astype(BF16)
    g_hi, g_mid, g_lo = _split3(g_ref[0])
    b = (jnp.dot(tri, g_hi, preferred_element_type=F32)
         + jnp.dot(tri, g_mid, preferred_element_type=F32)
         + jnp.dot(tri, g_lo, preferred_element_type=F32))
    b = b * math.log2(math.e)

    q = q_ref[0].astype(F32) * (GLA_DK ** -0.5)
    k = k_ref[0].astype(F32)

    row = lax.broadcasted_iota(jnp.int32, (t_blk, hk), 0)
    b_last = b[t_blk - 1:t_blk, :]
    q_in = (q * jnp.exp2(b)).astype(BF16)
    k_out = (k * jnp.exp2(b_last - b)).astype(BF16)
    e_last = jnp.exp2(b_last)

    level = t_blk // 2
    levels = []
    while level >= GLA_DIAG:
        grp = 2 * level
        b3 = b.reshape(t_blk // grp, grp, hk)
        pivot = jnp.broadcast_to(b3[:, level - 1:level, :], b3.shape).reshape(t_blk, hk)
        upper = (row % grp) >= level
        e = jnp.exp2(-jnp.abs(b - pivot))
        qt = jnp.where(upper, q * e, 0.0).astype(BF16)
        kt = jnp.where(upper, 0.0, k * e).astype(BF16)
        levels.append((grp, qt, kt))
        level //= 2

    nd = t_blk // GLA_DIAG
    b3 = b.reshape(nd, GLA_DIAG, hk)
    q3 = q.reshape(nd, GLA_DIAG, hk)
    k3 = k.reshape(nd, GLA_DIAG, hk)
    sel_shape = (GLA_DIAG * GLA_DK, t_blk)
    sel = jnp.where(lax.broadcasted_iota(jnp.int32, sel_shape, 0) // GLA_DK
                    == lax.broadcasted_iota(jnp.int32, sel_shape, 1) % GLA_DIAG, 1.0, 0.0).astype(BF16)
    diag_mask = ((ri // GLA_DIAG) == (ci // GLA_DIAG)) & (ci <= ri)

    eye = (lax.broadcasted_iota(jnp.int32, (GLA_DK, GLA_DK), 0)
           == lax.broadcasted_iota(jnp.int32, (GLA_DK, GLA_DK), 1))

    diag_terms = []
    for j in range(GLA_DIAG):
        d = jnp.minimum(b3 - b3[:, j:j + 1, :], 0.0)
        pj = (jnp.exp2(d) * q3 * k3[:, j:j + 1, :]).reshape(t_blk, hk)
        diag_terms.append(pj)

    ng = ng_ref[...]
    for h in range(GLA_HEADS):
        ks = slice(h * GLA_DK, (h + 1) * GLA_DK)
        vs = slice(h * GLA_DV, (h + 1) * GLA_DV)
        a = jnp.zeros((t_blk, t_blk), F32)
        for grp, qt, kt in levels:
            s_l = lax.dot_general(qt[:, ks], kt[:, ks], (((1,), (1,)), ((), ())),
                                  preferred_element_type=F32)
            if grp == t_blk:
                a = a + s_l
            else:
                a = a + jnp.where((ri // grp) == (ci // grp), s_l, 0.0)
        stacked = jnp.concatenate([diag_terms[j][:, ks] for j in range(GLA_DIAG)], axis=1)
        a = jnp.where(diag_mask, jnp.dot(stacked.astype(BF16), sel, preferred_element_type=F32), a)

        v_h = v_ref[0, :, vs]
        state = state_ref[h]
        o = (jnp.dot(q_in[:, ks], state.astype(BF16), preferred_element_type=F32)
             + jnp.dot(a.astype(BF16), v_h, preferred_element_type=F32))

        e_col = jnp.sum(jnp.where(eye, jnp.broadcast_to(e_last[:, ks], (GLA_DK, GLA_DK)), 0.0),
                        axis=1, keepdims=True)
        upd = lax.dot_general(k_out[:, ks], v_h, (((0,), (0,)), ((), ())),
                              preferred_element_type=F32)
        state_ref[h] = e_col * state + upd

        r_h = r_ref[0, :, vs].astype(F32)
        gate = r_h / (1.0 + jnp.exp(-r_h))
        o_ref[0, :, vs] = (_rms(o, ng) * gate).astype(o_ref.dtype)


def _gla(proj, g, norm_gain):
    b, s, _ = proj.shape
    t = GLA_BLOCK
    hk = GLA_HEADS * GLA_DK
    hv = GLA_HEADS * GLA_DV
    return pl.pallas_call(
        _gla_kernel,
        out_shape=jax.ShapeDtypeStruct((b, s, hv), BF16),
        grid=(b, s // t),
        in_specs=[
            pl.BlockSpec((1, t, hk), lambda bi, ti: (bi, ti, 0)),
            pl.BlockSpec((1, t, hk), lambda bi, ti: (bi, ti, 1)),
            pl.BlockSpec((1, t, hv), lambda bi, ti: (bi, ti, 1)),
            pl.BlockSpec((1, t, hv), lambda bi, ti: (bi, ti, 2)),
            pl.BlockSpec((1, t, hk), lambda bi, ti: (bi, ti, 0)),
            pl.BlockSpec((1, GLA_DV), lambda bi, ti: (0, 0)),
        ],
        out_specs=pl.BlockSpec((1, t, hv), lambda bi, ti: (bi, ti, 0)),
        scratch_shapes=[pltpu.VMEM((GLA_HEADS, GLA_DK, GLA_DV), F32)],
        compiler_params=_cparams(("parallel", "arbitrary")),
        name="gla",
    )(proj, proj, proj, proj, g, norm_gain)


def _ffn_kernel(x_ref, halo_ref, g_ref, wg_ref, wu_ref, cwg_ref, cwu_ref, cbg_ref, cbu_ref, wd_ref,
                fg_ref, o_ref, xn_ref, acc_ref, slab_ref, *, tiles_per_seq, final_norm):
    i = pl.program_id(0)
    j = pl.program_id(1)
    nj = pl.num_programs(1)
    tm, d = x_ref.shape
    rows = tm // FFN_SUBTILES
    ext = rows + FFN_HALO
    seg = ext // 8
    nslab = d // LANES
    subs = range(FFN_SUBTILES)

    @pl.when(j == 0)
    def _():
        gain = g_ref[...]
        for s in subs:
            if s == 0:
                halo = jnp.where(i % tiles_per_seq == 0, 0.0, _rms(halo_ref[...], gain))
            else:
                halo = _rms(x_ref[s * rows - FFN_HALO:s * rows, :], gain)
            xn = _rms(x_ref[s * rows:(s + 1) * rows, :], gain)
            for c in range(nslab):
                slab_ref[c, 0:FFN_HALO, :] = halo[:, c * LANES:(c + 1) * LANES]
                slab_ref[c, FFN_HALO:ext, :] = xn[:, c * LANES:(c + 1) * LANES]
            for a2 in range(seg // 2):
                grp = [jnp.concatenate([slab_ref[c, pl.ds(2 * a2 + r, 8, stride=seg), :] for c in range(nslab)],
                                       axis=1) for r in range(2)]
                xn_ref[s, 16 * a2:16 * a2 + 16, :] = jnp.concatenate(grp, axis=0).astype(BF16)
        acc_ref[...] = jnp.zeros(acc_ref.shape, F32)

    def conv(u, cw_ref, cb_ref):
        cw = cw_ref[...]
        u3 = u.reshape(seg, 8, u.shape[-1])
        wrap1 = pltpu.roll(u3[seg - 1], 1, 0)[None]
        wrap2 = pltpu.roll(u3[seg - 2], 1, 0)[None]
        prev1 = jnp.concatenate([wrap1, u3[:seg - 1]], axis=0)
        prev2 = jnp.concatenate([wrap2, wrap1, u3[:seg - 2]], axis=0)
        return cw[2:3, :] * u3 + cw[1:2, :] * prev1 + cw[0:1, :] * prev2 + cb_ref[...]

    w_gate = wg_ref[...].astype(BF16)
    w_up = wu_ref[...].astype(BF16)
    w_down = wd_ref[...].astype(BF16)
    ups = [(jnp.dot(xn_ref[s], w_gate, preferred_element_type=F32),
            jnp.dot(xn_ref[s], w_up, preferred_element_type=F32)) for s in subs]
    acts = []
    for s in subs:
        gate = conv(ups[s][0], cwg_ref, cbg_ref)
        up = conv(ups[s][1], cwu_ref, cbu_ref)
        acts.append(((gate / (1.0 + jnp.exp(-gate))) * up).reshape(ext, gate.shape[-1]).astype(BF16))
    for s in subs:
        acc_ref[s] += jnp.dot(acts[s], w_down, preferred_element_type=F32)

    @pl.when(j == nj - 1)
    def _():
        for s in subs:
            for a in range(seg):
                for c in range(nslab):
                    slab_ref[c, pl.ds(a, 8, stride=seg), :] = acc_ref[s, 8 * a:8 * a + 8, c * LANES:(c + 1) * LANES]
            y = (x_ref[s * rows:(s + 1) * rows, :]
                 + jnp.concatenate([slab_ref[c, FFN_HALO:ext, :] for c in range(nslab)], axis=1))
            if final_norm:
                y = _rms(y, fg_ref[...])
            o_ref[s * rows:(s + 1) * rows, :] = y


def _ffn(x, gain, w_up, conv_w, conv_b, w_down, layer, final_gain, *, tm, tf, seq, final_norm):
    m, d = x.shape
    f = w_down.shape[1]
    nf = f // tf
    tiles_per_seq = seq // tm
    halo_blocks = tm // FFN_HALO
    kern = functools.partial(_ffn_kernel, tiles_per_seq=tiles_per_seq, final_norm=final_norm)
    return pl.pallas_call(
        kern,
        out_shape=jax.ShapeDtypeStruct((m, d), F32),
        grid=(m // tm, nf),
        in_specs=[
            pl.BlockSpec((tm, d), lambda i, j: (i, 0)),
            pl.BlockSpec((FFN_HALO, d), lambda i, j: (jnp.maximum(i * halo_blocks - 1, 0), 0)),
            pl.BlockSpec((None, 1, d), lambda i, j: (layer, 0, 0)),
            pl.BlockSpec((None, d, tf), lambda i, j: (layer, 0, j)),
            pl.BlockSpec((None, d, tf), lambda i, j: (layer, 0, nf + j)),
            pl.BlockSpec((None, CONV_WIDTH, tf), lambda i, j: (layer, 0, j)),
            pl.BlockSpec((None, CONV_WIDTH, tf), lambda i, j: (layer, 0, nf + j)),
            pl.BlockSpec((None, 1, tf), lambda i, j: (layer, 0, j)),
            pl.BlockSpec((None, 1, tf), lambda i, j: (layer, 0, nf + j)),
            pl.BlockSpec((None, tf, d), lambda i, j: (layer, j, 0)),
            pl.BlockSpec((1, d), lambda i, j: (0, 0)),
        ],
        out_specs=pl.BlockSpec((tm, d), lambda i, j: (i, 0)),
        scratch_shapes=[pltpu.VMEM((FFN_SUBTILES, tm // FFN_SUBTILES + FFN_HALO, d), BF16),
                        pltpu.VMEM((FFN_SUBTILES, tm // FFN_SUBTILES + FFN_HALO, d), F32),
                        pltpu.VMEM((d // LANES, tm // FFN_SUBTILES + FFN_HALO, LANES), F32)],
        compiler_params=_cparams(("parallel", "arbitrary")),
        name="ffn",
    )(x, x, gain, w_up, w_up, conv_w, conv_w, conv_b, conv_b, w_down, final_gain)


def _rope_tables(seq, q_scale):
    half = DIFF_HEAD_DIM // 2
    inv = 1.0 / (ROPE_THETA ** (jnp.arange(0, DIFF_HEAD_DIM, 2, dtype=F32) / DIFF_HEAD_DIM))
    ang = jnp.arange(seq, dtype=F32)[:, None] * inv[None, :]
    cos, sin = jnp.cos(ang), jnp.sin(ang)
    cos_t = jnp.tile(cos, (1, LANES // half))
    sin_t = jnp.concatenate([-sin, -sin, sin, sin], axis=1)
    cos3 = jnp.stack([cos_t * q_scale, cos_t, jnp.ones_like(cos_t)])
    sin3 = jnp.stack([sin_t * q_scale, sin_t, jnp.zeros_like(sin_t)])
    return cos3, sin3


def _reorder_qk_columns(w_qkv):
    layers, d, _ = w_qkv.shape
    width = DIFF_HEADS * DIFF_V_DIM
    half = DIFF_HEAD_DIM // 2
    qk = w_qkv[:, :, :2 * width].reshape(layers, d, 2, DIFF_HEADS, 2, 2, half)
    qk = qk.transpose(0, 1, 2, 3, 5, 4, 6).reshape(layers, d, 2 * width)
    return jnp.concatenate([qk, w_qkv[:, :, 2 * width:]], axis=2).astype(BF16)


def _trunk(x, norm_mix, norm_ffn, norm_final, diff_w_qkv, diff_w_o, diff_lambda, diff_subln,
           gla_w_in, gla_w_a1, gla_w_a2, gla_b_a, gla_norm, gla_w_o,
           ffn_w_up, ffn_conv_w, ffn_conv_b, ffn_w_down, *, tm, tq, tf):
    bsz, seq, d = x.shape
    depth = norm_mix.shape[0]
    m = bsz * seq
    tables = _rope_tables(seq, DIFF_HEAD_DIM ** -0.5 * math.log2(math.e))
    w_qkv = _reorder_qk_columns(diff_w_qkv)
    h = x.reshape(m, d)
    rank = gla_w_a1.shape[-1]
    for layer in range(depth):
        jdx = layer // 2
        gain = norm_mix[layer].reshape(1, d)
        if layer % 2 == 0:
            lambda_init = 0.8 - 0.6 * math.exp(-0.3 * layer)
            qkv = _norm_proj(h, gain, w_qkv, jdx, tables, tm=tm, tn=d, seq=seq)
            o = _diff_attention(qkv.reshape(bsz, seq, 3 * d), diff_lambda[jdx],
                                diff_subln[jdx].reshape(DIFF_V_DIM, 1),
                                lambda_init=lambda_init, tq=tq)
            h = _proj_res(o.reshape(m, d), diff_w_o, jdx, h, tm=tm)
        else:
            proj = _norm_proj(h, gain, gla_w_in, jdx, None, tm=tm, tn=d, seq=seq)
            w1 = jnp.pad(gla_w_a1[jdx], ((0, 0), (0, LANES - rank))).astype(BF16)
            w2 = jnp.pad(gla_w_a2[jdx], ((0, LANES - rank), (0, 0)))
            g = _gla_gate(h, gain, w1, w2, gla_b_a[jdx].reshape(1, -1), tm=tm)
            o = _gla(proj.reshape(bsz, seq, 3 * d), g.reshape(bsz, seq, -1),
                     gla_norm[jdx].reshape(1, GLA_DV))
            h = _proj_res(o.reshape(m, d), gla_w_o, jdx, h, tm=tm)
        h = _ffn(h, norm_ffn.reshape(depth, 1, d), ffn_w_up, ffn_conv_w, ffn_conv_b.reshape(depth, 1, -1),
                 ffn_w_down, layer, norm_final.reshape(1, d), tm=tm, tf=tf, seq=seq,
                 final_norm=(layer == depth - 1))
    return h.reshape(bsz, seq, d)


def kernel(x, norm_mix, norm_ffn, norm_final, diff_w_qkv, diff_w_o, diff_lambda, diff_subln, gla_w_in, gla_w_a1, gla_w_a2, gla_b_a, gla_norm, gla_w_o, ffn_w_up, ffn_conv_w, ffn_conv_b, ffn_w_down):
    return _trunk(x, norm_mix, norm_ffn, norm_final, diff_w_qkv, diff_w_o, diff_lambda, diff_subln,
                  gla_w_in, gla_w_a1, gla_w_a2, gla_b_a, gla_norm, gla_w_o,
                  ffn_w_up, ffn_conv_w, ffn_conv_b, ffn_w_down, tm=1024, tq=512, tf=256)
```

```python
import functools
import math

import jax
import jax.numpy as jnp
from jax import lax
from jax.experimental import pallas as pl
from jax.experimental.pallas import tpu as pltpu

F32 = jnp.float32
BF16 = jnp.bfloat16

NORM_EPS = 1e-6
ROPE_THETA = 10000.0

DIFF_HEADS = 8
DIFF_HEAD_DIM = 64
DIFF_V_DIM = 2 * DIFF_HEAD_DIM

GLA_HEADS = 4
GLA_DK = 128
GLA_DV = 256
GLA_GATE_TEMP = 16.0
GLA_BLOCK = 256
GLA_DIAG = 8

CONV_WIDTH = 3
FFN_HALO = 16
FFN_SUBTILES = 4

LANES = 128
VMEM_LIMIT_BYTES = 56 * 1024 * 1024


def _cparams(semantics):
    return pltpu.CompilerParams(dimension_semantics=semantics, vmem_limit_bytes=VMEM_LIMIT_BYTES)


def _rms(x, gain):
    ms = jnp.mean(x * x, axis=-1, keepdims=True)
    return x * lax.rsqrt(ms + NORM_EPS) * gain


def _norm_proj_kernel(x_ref, g_ref, w_ref, *rest, rope):
    j = pl.program_id(1)
    if rope:
        cos_ref, sin_ref, o_ref, xn_ref = rest
    else:
        o_ref, xn_ref = rest

    @pl.when(j == 0)
    def _():
        xn_ref[...] = _rms(x_ref[...], g_ref[...]).astype(BF16)

    acc = jnp.dot(xn_ref[...], w_ref[...].astype(BF16), preferred_element_type=F32)
    if not rope:
        o_ref[...] = acc.astype(o_ref.dtype)
        return
    c = cos_ref[0]
    s = sin_ref[0]
    for hh in range(acc.shape[1] // LANES):
        t = acc[:, hh * LANES:(hh + 1) * LANES]
        o_ref[:, hh * LANES:(hh + 1) * LANES] = (t * c + pltpu.roll(t, LANES // 2, 1) * s).astype(o_ref.dtype)


def _norm_proj(x, gain, w, layer, tables, *, tm, tn, seq):
    m, d = x.shape
    n = w.shape[2]
    nseq = seq // tm
    rope = tables is not None
    in_specs = [
        pl.BlockSpec((tm, d), lambda i, j: (i, 0)),
        pl.BlockSpec((1, d), lambda i, j: (0, 0)),
        pl.BlockSpec((None, d, tn), lambda i, j: (layer, 0, j)),
    ]
    if rope:
        assert tables[0].shape[0] == n // tn
        in_specs += [pl.BlockSpec((1, tm, LANES), lambda i, j: (j, i % nseq, 0))] * 2
    return pl.pallas_call(
        functools.partial(_norm_proj_kernel, rope=rope),
        out_shape=jax.ShapeDtypeStruct((m, n), BF16),
        grid=(m // tm, n // tn),
        in_specs=in_specs,
        out_specs=pl.BlockSpec((tm, tn), lambda i, j: (i, j)),
        scratch_shapes=[pltpu.VMEM((tm, d), BF16)],
        compiler_params=_cparams(("parallel", "arbitrary")),
        name="norm_proj",
    )(x, gain, w, *(tables if rope else ()))


ATTN_CHUNK = 256
ATTN_HEADS_PER_STEP = 4
ATTN_V_ROWS = DIFF_V_DIM + 16


def _diff_attn_kernel(lam_ref, sg_ref, q_ref, k_ref, v_ref, o_ref, vt_ref, s_ref, acc_ref, *, tq, lambda_init):
    qi = pl.program_id(2)
    n_kv_blocks = vt_ref.shape[1]
    n2 = 2 * tq
    ch = min(ATTN_CHUNK, tq)
    nch = tq // ch
    heads = range(ATTN_HEADS_PER_STEP)
    lanes = [slice(hh * LANES, (hh + 1) * LANES) for hh in heads]

    @pl.when(qi == 0)
    def _():
        for hh in heads:
            for c in range(n_kv_blocks):
                vt_ref[hh, c, 0:DIFF_V_DIM, :] = v_ref[0, c * tq:(c + 1) * tq, lanes[hh]].astype(F32).T.astype(BF16)
                ones_row = lax.broadcasted_iota(jnp.int32, (ATTN_V_ROWS - DIFF_V_DIM, tq), 0) == 0
                vt_ref[hh, c, DIFF_V_DIM:ATTN_V_ROWS, :] = jnp.where(ones_row, 1.0, 0.0).astype(BF16)

    qqt = []
    for hh in heads:
        qt = q_ref[0, :, lanes[hh]].astype(F32).T
        feat = lax.broadcasted_iota(jnp.int32, qt.shape, 0)
        map1 = (feat % DIFF_HEAD_DIM) < (DIFF_HEAD_DIM // 2)
        qqt.append(jnp.concatenate([jnp.where(map1, qt, 0.0), jnp.where(map1, 0.0, qt)], axis=1).astype(BF16))
    acc_ref[...] = jnp.zeros(acc_ref.shape, F32)

    def scores(hh, blk, slot, masked):
        cmx8 = None
        for c in range(nch):
            start = pl.multiple_of(blk * tq + c * ch, ch)
            s = jnp.dot(k_ref[0, pl.ds(start, ch), lanes[hh]], qqt[hh], preferred_element_type=F32)
            if masked:
                kv_pos = lax.broadcasted_iota(jnp.int32, s.shape, 0) + c * ch
                col = lax.broadcasted_iota(jnp.int32, s.shape, 1)
                q_pos = jnp.where(col >= tq, col - tq, col)
                s = jnp.where(kv_pos <= q_pos, s, -jnp.inf)
            s_ref[hh, slot, c * ch:(c + 1) * ch, :] = s
            part = jnp.max(s.reshape(ch // 8, 8, n2), axis=0)
            cmx8 = part if cmx8 is None else jnp.maximum(cmx8, part)
        return cmx8

    def softmax_pv(hh, blk, slot, m, cmx8):
        m_new = jnp.maximum(m, jnp.max(cmx8, axis=0, keepdims=True))
        alpha = jnp.exp2(m - m_new)
        pv = None
        for c in range(nch):
            p = jnp.exp2((s_ref[hh, slot, c * ch:(c + 1) * ch, :] - m_new).astype(BF16))
            d = jnp.dot(vt_ref[hh, blk, :, c * ch:(c + 1) * ch], p, preferred_element_type=F32)
            pv = d if pv is None else pv + d
        acc_ref[hh] = alpha * acc_ref[hh] + pv
        return (m_new,)

    cmx8_0 = [scores(hh, qi, 0, True) for hh in heads]

    def body(t, carry):
        slot = lax.rem(t, 2)
        stats = [softmax_pv(hh, jnp.where(t == 0, qi, t - 1), slot, *carry[hh]) for hh in heads]
        return tuple(stats[hh] + (scores(hh, t, 1 - slot, False),) for hh in heads)

    init = tuple((jnp.full((1, n2), -jnp.inf, F32), cmx8_0[hh]) for hh in heads)
    carry = lax.fori_loop(0, qi, body, init)

    lv = lam_ref[...]
    lam = (jnp.exp(jnp.sum(lv[0:1] * lv[1:2], axis=1, keepdims=True))
           - jnp.exp(jnp.sum(lv[2:3] * lv[3:4], axis=1, keepdims=True)) + lambda_init)
    for hh in heads:
        softmax_pv(hh, jnp.maximum(qi - 1, 0), lax.rem(qi, 2), *carry[hh])
    for hh in heads:
        l = acc_ref[hh, DIFF_V_DIM:DIFF_V_DIM + 1, :]
        o1 = acc_ref[hh, 0:DIFF_V_DIM, 0:tq] / l[:, 0:tq]
        o2 = acc_ref[hh, 0:DIFF_V_DIM, tq:n2] / l[:, tq:n2]
        o = o1 - lam * o2
        ms = jnp.mean(o * o, axis=0, keepdims=True)
        o = o * lax.rsqrt(ms + NORM_EPS) * sg_ref[...] * (1.0 - lambda_init)
        o_ref[0, :, lanes[hh]] = o.T.astype(o_ref.dtype)


def _diff_attention(qkv, lam_vecs, subln_gain, *, lambda_init, tq):
    b, s, _ = qkv.shape
    hps = ATTN_HEADS_PER_STEP
    groups = DIFF_HEADS // hps
    width = hps * LANES
    kern = functools.partial(_diff_attn_kernel, tq=tq, lambda_init=lambda_init)
    return pl.pallas_call(
        kern,
        out_shape=jax.ShapeDtypeStruct((b, s, DIFF_HEADS * DIFF_V_DIM), BF16),
        grid=(b, groups, s // tq),
        in_specs=[
            pl.BlockSpec((4, DIFF_HEAD_DIM), lambda bi, gi, qi: (0, 0)),
            pl.BlockSpec((DIFF_V_DIM, 1), lambda bi, gi, qi: (0, 0)),
            pl.BlockSpec((1, tq, width), lambda bi, gi, qi: (bi, qi, gi)),
            pl.BlockSpec((1, s, width), lambda bi, gi, qi: (bi, 0, groups + gi)),
            pl.BlockSpec((1, s, width), lambda bi, gi, qi: (bi, 0, 2 * groups + gi)),
        ],
        out_specs=pl.BlockSpec((1, tq, width), lambda bi, gi, qi: (bi, qi, gi)),
        scratch_shapes=[
            pltpu.VMEM((hps, s // tq, ATTN_V_ROWS, tq), BF16),
            pltpu.VMEM((hps, 2, tq, 2 * tq), F32),
            pltpu.VMEM((hps, ATTN_V_ROWS, 2 * tq), F32),
        ],
        compiler_params=_cparams(("parallel", "parallel", "arbitrary")),
        name="diff_attn",
    )(lam_vecs, subln_gain, qkv, qkv, qkv)


def _proj_res_kernel(a_ref, w_ref, res_ref, o_ref):
    o_ref[...] = res_ref[...] + jnp.dot(a_ref[...], w_ref[...].astype(BF16), preferred_element_type=F32)


def _proj_res(a, w, layer, res, *, tm):
    m, k = a.shape
    n = w.shape[2]
    return pl.pallas_call(
        _proj_res_kernel,
        out_shape=jax.ShapeDtypeStruct((m, n), F32),
        grid=(m // tm,),
        in_specs=[
            pl.BlockSpec((tm, k), lambda i: (i, 0)),
            pl.BlockSpec((None, k, n), lambda i: (layer, 0, 0)),
            pl.BlockSpec((tm, n), lambda i: (i, 0)),
        ],
        out_specs=pl.BlockSpec((tm, n), lambda i: (i, 0)),
        compiler_params=_cparams(("parallel",)),
        name="proj_res",
    )(a, w, res)


def _gla_gate_kernel(x_ref, g_ref, w1_ref, w2_ref, b_ref, o_ref):
    xn = _rms(x_ref[...], g_ref[...]).astype(BF16)
    t = jnp.dot(xn, w1_ref[...], preferred_element_type=F32)
    z = jnp.dot(t, w2_ref[...], preferred_element_type=F32, precision=lax.Precision.HIGHEST) + b_ref[...]
    log_sig = jnp.minimum(z, 0.0) - jnp.log1p(jnp.exp(-jnp.abs(z)))
    o_ref[...] = log_sig * (1.0 / GLA_GATE_TEMP)


def _gla_gate(x, gain, w1, w2, bias, *, tm):
    m, d = x.shape
    r = w1.shape[1]
    n = w2.shape[1]
    return pl.pallas_call(
        _gla_gate_kernel,
        out_shape=jax.ShapeDtypeStruct((m, n), F32),
        grid=(m // tm,),
        in_specs=[
            pl.BlockSpec((tm, d), lambda i: (i, 0)),
            pl.BlockSpec((1, d), lambda i: (0, 0)),
            pl.BlockSpec((d, r), lambda i: (0, 0)),
            pl.BlockSpec((r, n), lambda i: (0, 0)),
            pl.BlockSpec((1, n), lambda i: (0, 0)),
        ],
        out_specs=pl.BlockSpec((tm, n), lambda i: (i, 0)),
        compiler_params=_cparams(("parallel",)),
        name="gla_gate",
    )(x, gain, w1, w2, bias)


def _split3(x):
    hi = x.astype(BF16)
    r1 = x - hi.astype(F32)
    mid = r1.astype(BF16)
    lo = (r1 - mid.astype(F32)).astype(BF16)
    return hi, mid, lo


def _gla_kernel(q_ref, k_ref, v_ref, r_ref, g_ref, ng_ref, o_ref, state_ref):
    t_blk = q_ref.shape[1]
    hk = GLA_HEADS * GLA_DK

    @pl.when(pl.program_id(1) == 0)
    def _():
        state_ref[...] = jnp.zeros(state_ref.shape, F32)

    ri = lax.broadcasted_iota(jnp.int32, (t_blk, t_blk), 0)
    ci = lax.broadcasted_iota(jnp.int32, (t_blk, t_blk), 1)

    tri = jnp.where(ci <= ri, 1.0, 0.0).astype(BF16)
    g_hi, g_mid, g_lo = _split3(g_ref[0])
    b = (jnp.dot(tri, g_hi, preferred_element_type=F32)
         + jnp.dot(tri, g_mid, preferred_element_type=F32)
         + jnp.dot(tri, g_lo, preferred_element_type=F32))
    b = b * math.log2(math.e)

    q = q_ref[0].astype(F32) * (GLA_DK ** -0.5)
    k = k_ref[0].astype(F32)

    row = lax.broadcasted_iota(jnp.int32, (t_blk, hk), 0)
    b_last = b[t_blk - 1:t_blk, :]
    q_in = (q * jnp.exp2(b)).astype(BF16)
    k_out = (k * jnp.exp2(b_last - b)).astype(BF16)
    e_last = jnp.exp2(b_last)

    level = t_blk // 2
    levels = []
    while level >= GLA_DIAG:
        grp = 2 * level
        b3 = b.reshape(t_blk // grp, grp, hk)
        pivot = jnp.broadcast_to(b3[:, level - 1:level, :], b3.shape).reshape(t_blk, hk)
        upper = (row % grp) >= level
        e = jnp.exp2(-jnp.abs(b - pivot))
        qt = jnp.where(upper, q * e, 0.0).astype(BF16)
        kt = jnp.where(upper, 0.0, k * e).astype(BF16)
        levels.append((grp, qt, kt))
        level //= 2

    nd = t_blk // GLA_DIAG
    b3 = b.reshape(nd, GLA_DIAG, hk)
    q3 = q.reshape(nd, GLA_DIAG, hk)
    k3 = k.reshape(nd, GLA_DIAG, hk)
    sel_shape = (GLA_DIAG * GLA_DK, t_blk)
    sel = jnp.where(lax.broadcasted_iota(jnp.int32, sel_shape, 0) // GLA_DK
                    == lax.broadcasted_iota(jnp.int32, sel_shape, 1) % GLA_DIAG, 1.0, 0.0).astype(BF16)
    diag_mask = ((ri // GLA_DIAG) == (ci // GLA_DIAG)) & (ci <= ri)

    eye = (lax.broadcasted_iota(jnp.int32, (GLA_DK, GLA_DK), 0)
           == lax.broadcasted_iota(jnp.int32, (GLA_DK, GLA_DK), 1))

    diag_terms = []
    for j in range(GLA_DIAG):
        d = jnp.minimum(b3 - b3[:, j:j + 1, :], 0.0)
        pj = (jnp.exp2(d) * q3 * k3[:, j:j + 1, :]).reshape(t_blk, hk)
        diag_terms.append(pj)

    ng = ng_ref[...]
    for h in range(GLA_HEADS):
        ks = slice(h * GLA_DK, (h + 1) * GLA_DK)
        vs = slice(h * GLA_DV, (h + 1) * GLA_DV)
        a = jnp.zeros((t_blk, t_blk), F32)
        for grp, qt, kt in levels:
            s_l = lax.dot_general(qt[:, ks], kt[:, ks], (((1,), (1,)), ((), ())),
                                  preferred_element_type=F32)
            if grp == t_blk:
                a = a + s_l
            else:
                a = a + jnp.where((ri // grp) == (ci // grp), s_l, 0.0)
        stacked = jnp.concatenate([diag_terms[j][:, ks] for j in range(GLA_DIAG)], axis=1)
        a = jnp.where(diag_mask, jnp.dot(stacked.astype(BF16), sel, preferred_element_type=F32), a)

        v_h = v_ref[0, :, vs]
        state = state_ref[h]
        o = (jnp.dot(q_in[:, ks], state.astype(BF16), preferred_element_type=F32)
             + jnp.dot(a.astype(BF16), v_h, preferred_element_type=F32))

        e_col = jnp.sum(jnp.where(eye, jnp.broadcast_to(e_last[:, ks], (GLA_DK, GLA_DK)), 0.0),
                        axis=1, keepdims=True)
        upd = lax.dot_general(k_out[:, ks], v_h, (((0,), (0,)), ((), ())),
                              preferred_element_type=F32)
        state_ref[h] = e_col * state + upd

        r_h = r_ref[0, :, vs].astype(F32)
        gate = r_h / (1.0 + jnp.exp(-r_h))
        o_ref[0, :, vs] = (_rms(o, ng) * gate).astype(o_ref.dtype)


def _gla(proj, g, norm_gain):
    b, s, _ = proj.shape
    t = GLA_BLOCK
    hk = GLA_HEADS * GLA_DK
    hv = GLA_HEADS * GLA_DV
    return pl.pallas_call(
        _gla_kernel,
        out_shape=jax.ShapeDtypeStruct((b, s, hv), BF16),
        grid=(b, s // t),
        in_specs=[
            pl.BlockSpec((1, t, hk), lambda bi, ti: (bi, ti, 0)),
            pl.BlockSpec((1, t, hk), lambda bi, ti: (bi, ti, 1)),
            pl.BlockSpec((1, t, hv), lambda bi, ti: (bi, ti, 1)),
            pl.BlockSpec((1, t, hv), lambda bi, ti: (bi, ti, 2)),
            pl.BlockSpec((1, t, hk), lambda bi, ti: (bi, ti, 0)),
            pl.BlockSpec((1, GLA_DV), lambda bi, ti: (0, 0)),
        ],
        out_specs=pl.BlockSpec((1, t, hv), lambda bi, ti: (bi, ti, 0)),
        scratch_shapes=[pltpu.VMEM((GLA_HEADS, GLA_DK, GLA_DV), F32)],
        compiler_params=_cparams(("parallel", "arbitrary")),
        name="gla",
    )(proj, proj, proj, proj, g, norm_gain)


def _ffn_kernel(x_ref, halo_ref, g_ref, wg_ref, wu_ref, cwg_ref, cwu_ref, cbg_ref, cbu_ref, wd_ref,
                fg_ref, o_ref, xn_ref, acc_ref, slab_ref, *, tiles_per_seq, final_norm):
    i = pl.program_id(0)
    j = pl.program_id(1)
    nj = pl.num_programs(1)
    tm, d = x_ref.shape
    rows = tm // FFN_SUBTILES
    ext = rows + FFN_HALO
    seg = ext // 8
    nslab = d // LANES
    subs = range(FFN_SUBTILES)

    @pl.when(j == 0)
    def _():
        gain = g_ref[...]
        for s in subs:
            if s == 0:
                halo = jnp.where(i % tiles_per_seq == 0, 0.0, _rms(halo_ref[...], gain))
            else:
                halo = _rms(x_ref[s * rows - FFN_HALO:s * rows, :], gain)
            xn = _rms(x_ref[s * rows:(s + 1) * rows, :], gain)
            for c in range(nslab):
                slab_ref[c, 0:FFN_HALO, :] = halo[:, c * LANES:(c + 1) * LANES]
                slab_ref[c, FFN_HALO:ext, :] = xn[:, c * LANES:(c + 1) * LANES]
            for a2 in range(seg // 2):
                grp = [jnp.concatenate([slab_ref[c, pl.ds(2 * a2 + r, 8, stride=seg), :] for c in range(nslab)],
                                       axis=1) for r in range(2)]
                xn_ref[s, 16 * a2:16 * a2 + 16, :] = jnp.concatenate(grp, axis=0).astype(BF16)
        acc_ref[...] = jnp.zeros(acc_ref.shape, F32)

    def conv(u, cw_ref, cb_ref):
        cw = cw_ref[...]
        u3 = u.reshape(seg, 8, u.shape[-1])
        wrap1 = pltpu.roll(u3[seg - 1], 1, 0)[None]
        wrap2 = pltpu.roll(u3[seg - 2], 1, 0)[None]
        prev1 = jnp.concatenate([wrap1, u3[:seg - 1]], axis=0)
        prev2 = jnp.concatenate([wrap2, wrap1, u3[:seg - 2]], axis=0)
        return cw[2:3, :] * u3 + cw[1:2, :] * prev1 + cw[0:1, :] * prev2 + cb_ref[...]

    w_gate = wg_ref[...]
    w_up = wu_ref[...]
    w_down = wd_ref[...].astype(BF16)
    ups = [(jnp.dot(xn_ref[s], w_gate, preferred_element_type=F32),
            jnp.dot(xn_ref[s], w_up, preferred_element_type=F32)) for s in subs]
    acts = []
    for s in subs:
        gate = conv(ups[s][0], cwg_ref, cbg_ref)
        up = conv(ups[s][1], cwu_ref, cbu_ref)
        acts.append(((gate / (1.0 + jnp.exp(-gate))) * up).reshape(ext, gate.shape[-1]).astype(BF16))
    for s in subs:
        acc_ref[s] += jnp.dot(acts[s], w_down, preferred_element_type=F32)

    @pl.when(j == nj - 1)
    def _():
        for s in subs:
            for a in range(seg):
                for c in range(nslab):
                    slab_ref[c, pl.ds(a, 8, stride=seg), :] = acc_ref[s, 8 * a:8 * a + 8, c * LANES:(c + 1) * LANES]
            y = (x_ref[s * rows:(s + 1) * rows, :]
                 + jnp.concatenate([slab_ref[c, FFN_HALO:ext, :] for c in range(nslab)], axis=1))
            if final_norm:
                y = _rms(y, fg_ref[...])
            o_ref[s * rows:(s + 1) * rows, :] = y


def _ffn(x, gain, w_up, conv_w, conv_b, w_down, layer, final_gain, *, tm, tf, seq, final_norm):
    m, d = x.shape
    f = w_down.shape[1]
    nf = f // tf
    assert w_up.shape[1:] == (2 * nf, d, tf)
    tiles_per_seq = seq // tm
    halo_blocks = tm // FFN_HALO
    kern = functools.partial(_ffn_kernel, tiles_per_seq=tiles_per_seq, final_norm=final_norm)
    return pl.pallas_call(
        kern,
        out_shape=jax.ShapeDtypeStruct((m, d), F32),
        grid=(m // tm, nf),
        in_specs=[
            pl.BlockSpec((tm, d), lambda i, j: (i, 0)),
            pl.BlockSpec((FFN_HALO, d), lambda i, j: (jnp.maximum(i * halo_blocks - 1, 0), 0)),
            pl.BlockSpec((None, 1, d), lambda i, j: (layer, 0, 0)),
            pl.BlockSpec((None, None, d, tf), lambda i, j: (layer, j, 0, 0)),
            pl.BlockSpec((None, None, d, tf), lambda i, j: (layer, nf + j, 0, 0)),
            pl.BlockSpec((None, CONV_WIDTH, tf), lambda i, j: (layer, 0, j)),
            pl.BlockSpec((None, CONV_WIDTH, tf), lambda i, j: (layer, 0, nf + j)),
            pl.BlockSpec((None, 1, tf), lambda i, j: (layer, 0, j)),
            pl.BlockSpec((None, 1, tf), lambda i, j: (layer, 0, nf + j)),
            pl.BlockSpec((None, tf, d), lambda i, j: (layer, j, 0)),
            pl.BlockSpec((1, d), lambda i, j: (0, 0)),
        ],
        out_specs=pl.BlockSpec((tm, d), lambda i, j: (i, 0)),
        scratch_shapes=[pltpu.VMEM((FFN_SUBTILES, tm // FFN_SUBTILES + FFN_HALO, d), BF16),
                        pltpu.VMEM((FFN_SUBTILES, tm // FFN_SUBTILES + FFN_HALO, d), F32),
                        pltpu.VMEM((d // LANES, tm // FFN_SUBTILES + FFN_HALO, LANES), F32)],
        compiler_params=_cparams(("parallel", "arbitrary")),
        name="ffn",
    )(x, x, gain, w_up, w_up, conv_w, conv_w, conv_b, conv_b, w_down, final_gain)


def _rope_tables(seq, q_scale):
    half = DIFF_HEAD_DIM // 2
    inv = 1.0 / (ROPE_THETA ** (jnp.arange(0, DIFF_HEAD_DIM, 2, dtype=F32) / DIFF_HEAD_DIM))
    ang = jnp.arange(seq, dtype=F32)[:, None] * inv[None, :]
    cos, sin = jnp.cos(ang), jnp.sin(ang)
    cos_t = jnp.tile(cos, (1, LANES // half))
    sin_t = jnp.concatenate([-sin, -sin, sin, sin], axis=1)
    cos3 = jnp.stack([cos_t * q_scale, cos_t, jnp.ones_like(cos_t)])
    sin3 = jnp.stack([sin_t * q_scale, sin_t, jnp.zeros_like(sin_t)])
    return cos3, sin3


def _reorder_qk_columns(w_qkv):
    layers, d, _ = w_qkv.shape
    width = DIFF_HEADS * DIFF_V_DIM
    half = DIFF_HEAD_DIM // 2
    qk = w_qkv[:, :, :2 * width].reshape(layers, d, 2, DIFF_HEADS, 2, 2, half)
    qk = qk.transpose(0, 1, 2, 3, 5, 4, 6).reshape(layers, d, 2 * width)
    return jnp.concatenate([qk, w_qkv[:, :, 2 * width:]], axis=2).astype(BF16)


def _trunk(x, norm_mix, norm_ffn, norm_final, diff_w_qkv, diff_w_o, diff_lambda, diff_subln,
           gla_w_in, gla_w_a1, gla_w_a2, gla_b_a, gla_norm, gla_w_o,
           ffn_w_up, ffn_conv_w, ffn_conv_b, ffn_w_down, *, tm, tq, tf):
    bsz, seq, d = x.shape
    depth = norm_mix.shape[0]
    m = bsz * seq
    tables = _rope_tables(seq, DIFF_HEAD_DIM ** -0.5 * math.log2(math.e))
    w_qkv = _reorder_qk_columns(diff_w_qkv)
    w_up_blocks = ffn_w_up.reshape(depth, d, -1, tf).transpose(0, 2, 1, 3).astype(BF16)
    h = x.reshape(m, d)
    rank = gla_w_a1.shape[-1]
    for layer in range(depth):
        jdx = layer // 2
        gain = norm_mix[layer].reshape(1, d)
        if layer % 2 == 0:
            lambda_init = 0.8 - 0.6 * math.exp(-0.3 * layer)
            qkv = _norm_proj(h, gain, w_qkv, jdx, tables, tm=tm, tn=d, seq=seq)
            o = _diff_attention(qkv.reshape(bsz, seq, 3 * d), diff_lambda[jdx],
                                diff_subln[jdx].reshape(DIFF_V_DIM, 1),
                                lambda_init=lambda_init, tq=tq)
            h = _proj_res(o.reshape(m, d), diff_w_o, jdx, h, tm=tm)
        else:
            proj = _norm_proj(h, gain, gla_w_in, jdx, None, tm=tm, tn=d, seq=seq)
            w1 = jnp.pad(gla_w_a1[jdx], ((0, 0), (0, LANES - rank))).astype(BF16)
            w2 = jnp.pad(gla_w_a2[jdx], ((0, LANES - rank), (0, 0)))
            g = _gla_gate(h, gain, w1, w2, gla_b_a[jdx].reshape(1, -1), tm=tm)
            o = _gla(proj.reshape(bsz, seq, 3 * d), g.reshape(bsz, seq, -1),
                     gla_norm[jdx].reshape(1, GLA_DV))
            h = _proj_res(o.reshape(m, d), gla_w_o, jdx, h, tm=tm)
        h = _ffn(h, norm_ffn.reshape(depth, 1, d), w_up_blocks, ffn_conv_w, ffn_conv_b.reshape(depth, 1, -1),
                 ffn_w_down, layer, norm_final.reshape(1, d), tm=tm, tf=tf, seq=seq,
                 final_norm=(layer == depth - 1))
    return h.reshape(bsz, seq, d)


def kernel(x, norm_mix, norm_ffn, norm_final, diff_w_qkv, diff_w_o, diff_lambda, diff_subln, gla_w_in, gla_w_a1, gla_w_a2, gla_b_a, gla_norm, gla_w_o, ffn_w_up, ffn_conv_w, ffn_conv_b, ffn_w_down):
    return _trunk(x, norm_mix, norm_ffn, norm_final, diff_w_qkv, diff_w_o, diff_lambda, diff_subln,
                  gla_w_in, gla_w_a1, gla_w_a2, gla_b_a, gla_norm, gla_w_o,
                  ffn_w_up, ffn_conv_w, ffn_conv_b, ffn_w_down, tm=1024, tq=512, tf=256)
```

```python
import functools
import math

import jax
import jax.numpy as jnp
from jax import lax
from jax.experimental import pallas as pl
from jax.experimental.pallas import tpu as pltpu

F32 = jnp.float32
BF16 = jnp.bfloat16

NORM_EPS = 1e-6
ROPE_THETA = 10000.0

DIFF_HEADS = 8
DIFF_HEAD_DIM = 64
DIFF_V_DIM = 2 * DIFF_HEAD_DIM

GLA_HEADS = 4
GLA_DK = 128
GLA_DV = 256
GLA_GATE_TEMP = 16.0
GLA_BLOCK = 256
GLA_DIAG = 8

CONV_WIDTH = 3
FFN_HALO = 16
FFN_SUBTILES = 4

LANES = 128
VMEM_LIMIT_BYTES = 56 * 1024 * 1024


def _cparams(semantics):
    return pltpu.CompilerParams(dimension_semantics=semantics, vmem_limit_bytes=VMEM_LIMIT_BYTES)


def _rms(x, gain):
    ms = jnp.mean(x * x, axis=-1, keepdims=True)
    return x * lax.rsqrt(ms + NORM_EPS) * gain


def _norm_proj_kernel(x_ref, g_ref, w_ref, *rest, rope):
    j = pl.program_id(1)
    if rope:
        cos_ref, sin_ref, o_ref, xn_ref = rest
    else:
        o_ref, xn_ref = rest

    @pl.when(j == 0)
    def _():
        xn_ref[...] = _rms(x_ref[...], g_ref[...]).astype(BF16)

    acc = jnp.dot(xn_ref[...], w_ref[...].astype(BF16), preferred_element_type=F32)
    if not rope:
        o_ref[...] = acc.astype(o_ref.dtype)
        return
    c = cos_ref[0]
    s = sin_ref[0]
    for hh in range(acc.shape[1] // LANES):
        t = acc[:, hh * LANES:(hh + 1) * LANES]
        o_ref[:, hh * LANES:(hh + 1) * LANES] = (t * c + pltpu.roll(t, LANES // 2, 1) * s).astype(o_ref.dtype)


def _norm_proj(x, gain, w, layer, tables, *, tm, tn, seq):
    m, d = x.shape
    n = w.shape[2]
    nseq = seq // tm
    rope = tables is not None
    in_specs = [
        pl.BlockSpec((tm, d), lambda i, j: (i, 0)),
        pl.BlockSpec((1, d), lambda i, j: (0, 0)),
        pl.BlockSpec((None, d, tn), lambda i, j: (layer, 0, j)),
    ]
    if rope:
        assert tables[0].shape[0] == n // tn
        in_specs += [pl.BlockSpec((1, tm, LANES), lambda i, j: (j, i % nseq, 0))] * 2
    return pl.pallas_call(
        functools.partial(_norm_proj_kernel, rope=rope),
        out_shape=jax.ShapeDtypeStruct((m, n), BF16),
        grid=(m // tm, n // tn),
        in_specs=in_specs,
        out_specs=pl.BlockSpec((tm, tn), lambda i, j: (i, j)),
        scratch_shapes=[pltpu.VMEM((tm, d), BF16)],
        compiler_params=_cparams(("parallel", "arbitrary")),
        name="norm_proj",
    )(x, gain, w, *(tables if rope else ()))


ATTN_CHUNK = 256
ATTN_HEADS_PER_STEP = 4
ATTN_V_ROWS = DIFF_V_DIM + 16


def _diff_attn_kernel(lam_ref, sg_ref, q_ref, k_ref, v_ref, o_ref, vt_ref, s_ref, acc_ref, *, tq, lambda_init):
    qi = pl.program_id(2)
    n_kv_blocks = vt_ref.shape[1]
    n2 = 2 * tq
    ch = min(ATTN_CHUNK, tq)
    nch = tq // ch
    heads = range(ATTN_HEADS_PER_STEP)
    lanes = [slice(hh * LANES, (hh + 1) * LANES) for hh in heads]

    @pl.when(qi == 0)
    def _():
        for hh in heads:
            for c in range(n_kv_blocks):
                vt_ref[hh, c, 0:DIFF_V_DIM, :] = v_ref[0, c * tq:(c + 1) * tq, lanes[hh]].astype(F32).T.astype(BF16)
                ones_row = lax.broadcasted_iota(jnp.int32, (ATTN_V_ROWS - DIFF_V_DIM, tq), 0) == 0
                vt_ref[hh, c, DIFF_V_DIM:ATTN_V_ROWS, :] = jnp.where(ones_row, 1.0, 0.0).astype(BF16)

    qqt = []
    for hh in heads:
        qt = q_ref[0, :, lanes[hh]].astype(F32).T
        feat = lax.broadcasted_iota(jnp.int32, qt.shape, 0)
        map1 = (feat % DIFF_HEAD_DIM) < (DIFF_HEAD_DIM // 2)
        qqt.append(jnp.concatenate([jnp.where(map1, qt, 0.0), jnp.where(map1, 0.0, qt)], axis=1).astype(BF16))
    acc_ref[...] = jnp.zeros(acc_ref.shape, F32)

    def scores(hh, blk, slot, masked):
        cmx8 = None
        for c in range(nch):
            start = pl.multiple_of(blk * tq + c * ch, ch)
            s = jnp.dot(k_ref[0, pl.ds(start, ch), lanes[hh]], qqt[hh], preferred_element_type=F32)
            if masked:
                kv_pos = lax.broadcasted_iota(jnp.int32, s.shape, 0) + c * ch
                col = lax.broadcasted_iota(jnp.int32, s.shape, 1)
                q_pos = jnp.where(col >= tq, col - tq, col)
                s = jnp.where(kv_pos <= q_pos, s, -jnp.inf)
            s_ref[hh, slot, c * ch:(c + 1) * ch, :] = s
            part = jnp.max(s.reshape(ch // 8, 8, n2), axis=0)
            cmx8 = part if cmx8 is None else jnp.maximum(cmx8, part)
        return cmx8

    def softmax_pv(hh, blk, slot, m, cmx8):
        m_new = jnp.maximum(m, jnp.max(cmx8, axis=0, keepdims=True))
        alpha = jnp.exp2(m - m_new)
        pv = None
        for c in range(nch):
            p = jnp.exp2((s_ref[hh, slot, c * ch:(c + 1) * ch, :] - m_new).astype(BF16))
            d = jnp.dot(vt_ref[hh, blk, :, c * ch:(c + 1) * ch], p, preferred_element_type=F32)
            pv = d if pv is None else pv + d
        acc_ref[hh] = alpha * acc_ref[hh] + pv
        return (m_new,)

    cmx8_0 = [scores(hh, qi, 0, True) for hh in heads]

    def body(t, carry):
        slot = lax.rem(t, 2)
        stats = [softmax_pv(hh, jnp.where(t == 0, qi, t - 1), slot, *carry[hh]) for hh in heads]
        return tuple(stats[hh] + (scores(hh, t, 1 - slot, False),) for hh in heads)

    init = tuple((jnp.full((1, n2), -jnp.inf, F32), cmx8_0[hh]) for hh in heads)
    carry = lax.fori_loop(0, qi, body, init)

    lv = lam_ref[...]
    lam = (jnp.exp(jnp.sum(lv[0:1] * lv[1:2], axis=1, keepdims=True))
           - jnp.exp(jnp.sum(lv[2:3] * lv[3:4], axis=1, keepdims=True)) + lambda_init)
    for hh in heads:
        softmax_pv(hh, jnp.maximum(qi - 1, 0), lax.rem(qi, 2), *carry[hh])
    for hh in heads:
        l = acc_ref[hh, DIFF_V_DIM:DIFF_V_DIM + 1, :]
        o1 = acc_ref[hh, 0:DIFF_V_DIM, 0:tq] / l[:, 0:tq]
        o2 = acc_ref[hh, 0:DIFF_V_DIM, tq:n2] / l[:, tq:n2]
        o = o1 - lam * o2
        ms = jnp.mean(o * o, axis=0, keepdims=True)
        o = o * lax.rsqrt(ms + NORM_EPS) * sg_ref[...] * (1.0 - lambda_init)
        o_ref[0, :, lanes[hh]] = o.T.astype(o_ref.dtype)


def _diff_attention(qkv, lam_vecs, subln_gain, *, lambda_init, tq):
    b, s, _ = qkv.shape
    hps = ATTN_HEADS_PER_STEP
    groups = DIFF_HEADS // hps
    width = hps * LANES
    kern = functools.partial(_diff_attn_kernel, tq=tq, lambda_init=lambda_init)
    return pl.pallas_call(
        kern,
        out_shape=jax.ShapeDtypeStruct((b, s, DIFF_HEADS * DIFF_V_DIM), BF16),
        grid=(b, groups, s // tq),
        in_specs=[
            pl.BlockSpec((4, DIFF_HEAD_DIM), lambda bi, gi, qi: (0, 0)),
            pl.BlockSpec((DIFF_V_DIM, 1), lambda bi, gi, qi: (0, 0)),
            pl.BlockSpec((1, tq, width), lambda bi, gi, qi: (bi, qi, gi)),
            pl.BlockSpec((1, s, width), lambda bi, gi, qi: (bi, 0, groups + gi)),
            pl.BlockSpec((1, s, width), lambda bi, gi, qi: (bi, 0, 2 * groups + gi)),
        ],
        out_specs=pl.BlockSpec((1, tq, width), lambda bi, gi, qi: (bi, qi, gi)),
        scratch_shapes=[
            pltpu.VMEM((hps, s // tq, ATTN_V_ROWS, tq), BF16),
            pltpu.VMEM((hps, 2, tq, 2 * tq), F32),
            pltpu.VMEM((hps, ATTN_V_ROWS, 2 * tq), F32),
        ],
        compiler_params=_cparams(("parallel", "parallel", "arbitrary")),
        name="diff_attn",
    )(lam_vecs, subln_gain, qkv, qkv, qkv)


def _proj_res_kernel(a_ref, w_ref, res_ref, o_ref):
    o_ref[...] = res_ref[...] + jnp.dot(a_ref[...], w_ref[...].astype(BF16), preferred_element_type=F32)


def _proj_res(a, w, layer, res, *, tm):
    m, k = a.shape
    n = w.shape[2]
    return pl.pallas_call(
        _proj_res_kernel,
        out_shape=jax.ShapeDtypeStruct((m, n), F32),
        grid=(m // tm,),
        in_specs=[
            pl.BlockSpec((tm, k), lambda i: (i, 0)),
            pl.BlockSpec((None, k, n), lambda i: (layer, 0, 0)),
            pl.BlockSpec((tm, n), lambda i: (i, 0)),
        ],
        out_specs=pl.BlockSpec((tm, n), lambda i: (i, 0)),
        compiler_params=_cparams(("parallel",)),
        name="proj_res",
    )(a, w, res)


def _gla_gate_kernel(x_ref, g_ref, w1_ref, w2_ref, b_ref, o_ref):
    xn = _rms(x_ref[...], g_ref[...]).astype(BF16)
    t = jnp.dot(xn, w1_ref[...], preferred_element_type=F32)
    z = jnp.dot(t, w2_ref[...], preferred_element_type=F32, precision=lax.Precision.HIGHEST) + b_ref[...]
    log_sig = jnp.minimum(z, 0.0) - jnp.log1p(jnp.exp(-jnp.abs(z)))
    o_ref[...] = log_sig * (1.0 / GLA_GATE_TEMP)


def _gla_gate(x, gain, w1, w2, bias, *, tm):
    m, d = x.shape
    r = w1.shape[1]
    n = w2.shape[1]
    return pl.pallas_call(
        _gla_gate_kernel,
        out_shape=jax.ShapeDtypeStruct((m, n), F32),
        grid=(m // tm,),
        in_specs=[
            pl.BlockSpec((tm, d), lambda i: (i, 0)),
            pl.BlockSpec((1, d), lambda i: (0, 0)),
            pl.BlockSpec((d, r), lambda i: (0, 0)),
            pl.BlockSpec((r, n), lambda i: (0, 0)),
            pl.BlockSpec((1, n), lambda i: (0, 0)),
        ],
        out_specs=pl.BlockSpec((tm, n), lambda i: (i, 0)),
        compiler_params=_cparams(("parallel",)),
        name="gla_gate",
    )(x, gain, w1, w2, bias)


def _split3(x):
    hi = x.astype(BF16)
    r1 = x - hi.astype(F32)
    mid = r1.astype(BF16)
    lo = (r1 - mid.astype(F32)).astype(BF16)
    return hi, mid, lo


def _gla_kernel(q_ref, k_ref, v_ref, r_ref, g_ref, ng_ref, o_ref, state_ref):
    t_blk = q_ref.shape[1]
    hk = GLA_HEADS * GLA_DK

    @pl.when(pl.program_id(1) == 0)
    def _():
        state_ref[...] = jnp.zeros(state_ref.shape, F32)

    ri = lax.broadcasted_iota(jnp.int32, (t_blk, t_blk), 0)
    ci = lax.broadcasted_iota(jnp.int32, (t_blk, t_blk), 1)

    tri = jnp.where(ci <= ri, 1.0, 0.0).astype(BF16)
    g_hi, g_mid, g_lo = _split3(g_ref[0])
    b = (jnp.dot(tri, g_hi, preferred_element_type=F32)
         + jnp.dot(tri, g_mid, preferred_element_type=F32)
         + jnp.dot(tri, g_lo, preferred_element_type=F32))
    b = b * math.log2(math.e)

    q = q_ref[0].astype(F32) * (GLA_DK ** -0.5)
    k = k_ref[0].astype(F32)

    row = lax.broadcasted_iota(jnp.int32, (t_blk, hk), 0)
    b_last = b[t_blk - 1:t_blk, :]
    q_in = (q * jnp.exp2(b)).astype(BF16)
    k_out = (k * jnp.exp2(b_last - b)).astype(BF16)
    e_last = jnp.exp2(b_last)

    level = t_blk // 2
    levels = []
    while level >= GLA_DIAG:
        grp = 2 * level
        b3 = b.reshape(t_blk // grp, grp, hk)
        pivot = jnp.broadcast_to(b3[:, level - 1:level, :], b3.shape).reshape(t_blk, hk)
        upper = (row % grp) >= level
        e = jnp.exp2(-jnp.abs(b - pivot))
        qt = jnp.where(upper, q * e, 0.0).astype(BF16)
        kt = jnp.where(upper, 0.0, k * e).astype(BF16)
        levels.append((grp, qt, kt))
        level //= 2

    nd = t_blk // GLA_DIAG
    b3 = b.reshape(nd, GLA_DIAG, hk)
    q3 = q.reshape(nd, GLA_DIAG, hk)
    k3 = k.reshape(nd, GLA_DIAG, hk)
    sel_shape = (GLA_DIAG * GLA_DK, t_blk)
    sel = jnp.where(lax.broadcasted_iota(jnp.int32, sel_shape, 0) // GLA_DK
                    == lax.broadcasted_iota(jnp.int32, sel_shape, 1) % GLA_DIAG, 1.0, 0.0).astype(BF16)
    diag_mask = ((ri // GLA_DIAG) == (ci // GLA_DIAG)) & (ci <= ri)

    eye = (lax.broadcasted_iota(jnp.int32, (GLA_DK, GLA_DK), 0)
           == lax.broadcasted_iota(jnp.int32, (GLA_DK, GLA_DK), 1))

    diag_terms = []
    for j in range(GLA_DIAG):
        d = jnp.minimum(b3 - b3[:, j:j + 1, :], 0.0)
        pj = (jnp.exp2(d) * q3 * k3[:, j:j + 1, :]).reshape(t_blk, hk)
        diag_terms.append(pj)

    ng = ng_ref[...]
    for h in range(GLA_HEADS):
        ks = slice(h * GLA_DK, (h + 1) * GLA_DK)
        vs = slice(h * GLA_DV, (h + 1) * GLA_DV)
        a = jnp.zeros((t_blk, t_blk), F32)
        for grp, qt, kt in levels:
            s_l = lax.dot_general(qt[:, ks], kt[:, ks], (((1,), (1,)), ((), ())),
                                  preferred_element_type=F32)
            if grp == t_blk:
                a = a + s_l
            else:
                a = a + jnp.where((ri // grp) == (ci // grp), s_l, 0.0)
        stacked = jnp.concatenate([diag_terms[j][:, ks] for j in range(GLA_DIAG)], axis=1)
        a = jnp.where(diag_mask, jnp.dot(stacked.astype(BF16), sel, preferred_element_type=F32), a)

        v_h = v_ref[0, :, vs]
        state = state_ref[h]
        o = (jnp.dot(q_in[:, ks], state.astype(BF16), preferred_element_type=F32)
             + jnp.dot(a.astype(BF16), v_h, preferred_element_type=F32))

        e_col = jnp.sum(jnp.where(eye, jnp.broadcast_to(e_last[:, ks], (GLA_DK, GLA_DK)), 0.0),
                        axis=1, keepdims=True)
        upd = lax.dot_general(k_out[:, ks], v_h, (((0,), (0,)), ((), ())),
                              preferred_element_type=F32)
        state_ref[h] = e_col * state + upd

        r_h = r_ref[0, :, vs].astype(F32)
        gate = r_h / (1.0 + jnp.exp(-r_h))
        o_ref[0, :, vs] = (_rms(o, ng) * gate).astype(o_ref.dtype)


def _gla(proj, g, norm_gain):
    b, s, _ = proj.shape
    t = GLA_BLOCK
    hk = GLA_HEADS * GLA_DK
    hv = GLA_HEADS * GLA_DV
    return pl.pallas_call(
        _gla_kernel,
        out_shape=jax.ShapeDtypeStruct((b, s, hv), BF16),
        grid=(b, s // t),
        in_specs=[
            pl.BlockSpec((1, t, hk), lambda bi, ti: (bi, ti, 0)),
            pl.BlockSpec((1, t, hk), lambda bi, ti: (bi, ti, 1)),
            pl.BlockSpec((1, t, hv), lambda bi, ti: (bi, ti, 1)),
            pl.BlockSpec((1, t, hv), lambda bi, ti: (bi, ti, 2)),
            pl.BlockSpec((1, t, hk), lambda bi, ti: (bi, ti, 0)),
            pl.BlockSpec((1, GLA_DV), lambda bi, ti: (0, 0)),
        ],
        out_specs=pl.BlockSpec((1, t, hv), lambda bi, ti: (bi, ti, 0)),
        scratch_shapes=[pltpu.VMEM((GLA_HEADS, GLA_DK, GLA_DV), F32)],
        compiler_params=_cparams(("parallel", "arbitrary")),
        name="gla",
    )(proj, proj, proj, proj, g, norm_gain)


def _ffn_kernel(x_ref, halo_ref, g_ref, wg_ref, wu_ref, cwg_ref, cwu_ref, cbg_ref, cbu_ref, wd_ref,
                fg_ref, o_ref, xn_ref, acc_ref, slab_ref, *, tiles_per_seq, final_norm):
    i = pl.program_id(0)
    j = pl.program_id(1)
    nj = pl.num_programs(1)
    tm, d = x_ref.shape
    rows = tm // FFN_SUBTILES
    ext = rows + FFN_HALO
    seg = ext // 8
    nslab = d // LANES
    subs = range(FFN_SUBTILES)

    @pl.when(j == 0)
    def _():
        gain = g_ref[...]
        for s in subs:
            if s == 0:
                halo = jnp.where(i % tiles_per_seq == 0, 0.0, _rms(halo_ref[...], gain))
            else:
                halo = _rms(x_ref[s * rows - FFN_HALO:s * rows, :], gain)
            xn = _rms(x_ref[s * rows:(s + 1) * rows, :], gain)
            for c in range(nslab):
                slab_ref[c, 0:FFN_HALO, :] = halo[:, c * LANES:(c + 1) * LANES]
                slab_ref[c, FFN_HALO:ext, :] = xn[:, c * LANES:(c + 1) * LANES]
            for a2 in range(seg // 2):
                grp = [jnp.concatenate([slab_ref[c, pl.ds(2 * a2 + r, 8, stride=seg), :] for c in range(nslab)],
                                       axis=1) for r in range(2)]
                xn_ref[s, 16 * a2:16 * a2 + 16, :] = jnp.concatenate(grp, axis=0).astype(BF16)
        acc_ref[...] = jnp.zeros(acc_ref.shape, F32)

    def conv(u, cw_ref, cb_ref):
        cw = cw_ref[...]
        u3 = u.reshape(seg, 8, u.shape[-1])
        wrap1 = pltpu.roll(u3[seg - 1], 1, 0)[None]
        wrap2 = pltpu.roll(u3[seg - 2], 1, 0)[None]
        prev1 = jnp.concatenate([wrap1, u3[:seg - 1]], axis=0)
        prev2 = jnp.concatenate([wrap2, wrap1, u3[:seg - 2]], axis=0)
        return cw[2:3, :] * u3 + cw[1:2, :] * prev1 + cw[0:1, :] * prev2 + cb_ref[...]

    w_gate = wg_ref[...]
    w_up = wu_ref[...]
    w_down = wd_ref[...]
    ups = [(jnp.dot(xn_ref[s], w_gate, preferred_element_type=F32),
            jnp.dot(xn_ref[s], w_up, preferred_element_type=F32)) for s in subs]
    acts = []
    for s in subs:
        gate = conv(ups[s][0], cwg_ref, cbg_ref)
        up = conv(ups[s][1], cwu_ref, cbu_ref)
        acts.append(((gate / (1.0 + jnp.exp(-gate))) * up).reshape(ext, gate.shape[-1]).astype(BF16))
    for s in subs:
        acc_ref[s] += jnp.dot(acts[s], w_down, preferred_element_type=F32)

    @pl.when(j == nj - 1)
    def _():
        for s in subs:
            for a in range(seg):
                for c in range(nslab):
                    slab_ref[c, pl.ds(a, 8, stride=seg), :] = acc_ref[s, 8 * a:8 * a + 8, c * LANES:(c + 1) * LANES]
            y = (x_ref[s * rows:(s + 1) * rows, :]
                 + jnp.concatenate([slab_ref[c, FFN_HALO:ext, :] for c in range(nslab)], axis=1))
            if final_norm:
                y = _rms(y, fg_ref[...])
            o_ref[s * rows:(s + 1) * rows, :] = y


def _ffn(x, gain, w_up, conv_w, conv_b, w_down, layer, final_gain, *, tm, tf, seq, final_norm):
    m, d = x.shape
    f = w_down.shape[1]
    nf = f // tf
    tiles_per_seq = seq // tm
    halo_blocks = tm // FFN_HALO
    kern = functools.partial(_ffn_kernel, tiles_per_seq=tiles_per_seq, final_norm=final_norm)
    return pl.pallas_call(
        kern,
        out_shape=jax.ShapeDtypeStruct((m, d), F32),
        grid=(m // tm, nf),
        in_specs=[
            pl.BlockSpec((tm, d), lambda i, j: (i, 0)),
            pl.BlockSpec((FFN_HALO, d), lambda i, j: (jnp.maximum(i * halo_blocks - 1, 0), 0)),
            pl.BlockSpec((None, 1, d), lambda i, j: (layer, 0, 0)),
            pl.BlockSpec((None, d, tf), lambda i, j: (layer, 0, j)),
            pl.BlockSpec((None, d, tf), lambda i, j: (layer, 0, nf + j)),
            pl.BlockSpec((None, CONV_WIDTH, tf), lambda i, j: (layer, 0, j)),
            pl.BlockSpec((None, CONV_WIDTH, tf), lambda i, j: (layer, 0, nf + j)),
            pl.BlockSpec((None, 1, tf), lambda i, j: (layer, 0, j)),
            pl.BlockSpec((None, 1, tf), lambda i, j: (layer, 0, nf + j)),
            pl.BlockSpec((None, tf, d), lambda i, j: (layer, j, 0)),
            pl.BlockSpec((1, d), lambda i, j: (0, 0)),
        ],
        out_specs=pl.BlockSpec((tm, d), lambda i, j: (i, 0)),
        scratch_shapes=[pltpu.VMEM((FFN_SUBTILES, tm // FFN_SUBTILES + FFN_HALO, d), BF16),
                        pltpu.VMEM((FFN_SUBTILES, tm // FFN_SUBTILES + FFN_HALO, d), F32),
                        pltpu.VMEM((d // LANES, tm // FFN_SUBTILES + FFN_HALO, LANES), F32)],
        compiler_params=_cparams(("parallel", "arbitrary")),
        name="ffn",
    )(x, x, gain, w_up, w_up, conv_w, conv_w, conv_b, conv_b, w_down, final_gain)


def _rope_tables(seq, q_scale):
    half = DIFF_HEAD_DIM // 2
    inv = 1.0 / (ROPE_THETA ** (jnp.arange(0, DIFF_HEAD_DIM, 2, dtype=F32) / DIFF_HEAD_DIM))
    ang = jnp.arange(seq, dtype=F32)[:, None] * inv[None, :]
    cos, sin = jnp.cos(ang), jnp.sin(ang)
    cos_t = jnp.tile(cos, (1, LANES // half))
    sin_t = jnp.concatenate([-sin, -sin, sin, sin], axis=1)
    cos3 = jnp.stack([cos_t * q_scale, cos_t, jnp.ones_like(cos_t)])
    sin3 = jnp.stack([sin_t * q_scale, sin_t, jnp.zeros_like(sin_t)])
    return cos3, sin3


def _reorder_qk_columns(w_qkv):
    layers, d, _ = w_qkv.shape
    width = DIFF_HEADS * DIFF_V_DIM
    half = DIFF_HEAD_DIM // 2
    qk = w_qkv[:, :, :2 * width].reshape(layers, d, 2, DIFF_HEADS, 2, 2, half)
    qk = qk.transpose(0, 1, 2, 3, 5, 4, 6).reshape(layers, d, 2 * width)
    return jnp.concatenate([qk, w_qkv[:, :, 2 * width:]], axis=2).astype(BF16)


def _trunk(x, norm_mix, norm_ffn, norm_final, diff_w_qkv, diff_w_o, diff_lambda, diff_subln,
           gla_w_in, gla_w_a1, gla_w_a2, gla_b_a, gla_norm, gla_w_o,
           ffn_w_up, ffn_conv_w, ffn_conv_b, ffn_w_down, *, tm, tq, tf):
    bsz, seq, d = x.shape
    depth = norm_mix.shape[0]
    m = bsz * seq
    tables = _rope_tables(seq, DIFF_HEAD_DIM ** -0.5 * math.log2(math.e))
    w_qkv = _reorder_qk_columns(diff_w_qkv)
    w_up = ffn_w_up.astype(BF16)
    w_down = ffn_w_down.astype(BF16)
    h = x.reshape(m, d)
    rank = gla_w_a1.shape[-1]
    for layer in range(depth):
        jdx = layer // 2
        gain = norm_mix[layer].reshape(1, d)
        if layer % 2 == 0:
            lambda_init = 0.8 - 0.6 * math.exp(-0.3 * layer)
            qkv = _norm_proj(h, gain, w_qkv, jdx, tables, tm=tm, tn=d, seq=seq)
            o = _diff_attention(qkv.reshape(bsz, seq, 3 * d), diff_lambda[jdx],
                                diff_subln[jdx].reshape(DIFF_V_DIM, 1),
                                lambda_init=lambda_init, tq=tq)
            h = _proj_res(o.reshape(m, d), diff_w_o, jdx, h, tm=tm)
        else:
            proj = _norm_proj(h, gain, gla_w_in, jdx, None, tm=tm, tn=d, seq=seq)
            w1 = jnp.pad(gla_w_a1[jdx], ((0, 0), (0, LANES - rank))).astype(BF16)
            w2 = jnp.pad(gla_w_a2[jdx], ((0, LANES - rank), (0, 0)))
            g = _gla_gate(h, gain, w1, w2, gla_b_a[jdx].reshape(1, -1), tm=tm)
            o = _gla(proj.reshape(bsz, seq, 3 * d), g.reshape(bsz, seq, -1),
                     gla_norm[jdx].reshape(1, GLA_DV))
            h = _proj_res(o.reshape(m, d), gla_w_o, jdx, h, tm=tm)
        h = _ffn(h, norm_ffn.reshape(depth, 1, d), w_up, ffn_conv_w, ffn_conv_b.reshape(depth, 1, -1),
                 w_down, layer, norm_final.reshape(1, d), tm=tm, tf=tf, seq=seq,
                 final_norm=(layer == depth - 1))
    return h.reshape(bsz, seq, d)


def kernel(x, norm_mix, norm_ffn, norm_final, diff_w_qkv, diff_w_o, diff_lambda, diff_subln, gla_w_in, gla_w_a1, gla_w_a2, gla_b_a, gla_norm, gla_w_o, ffn_w_up, ffn_conv_w, ffn_conv_b, ffn_w_down):
    return _trunk(x, norm_mix, norm_ffn, norm_final, diff_w_qkv, diff_w_o, diff_lambda, diff_subln,
                  gla_w_in, gla_w_a1, gla_w_a2, gla_b_a, gla_norm, gla_w_o,
                  ffn_w_up, ffn_conv_w, ffn_conv_b, ffn_w_down, tm=1024, tq=512, tf=256)
```

```python
import functools
import math

import jax
import jax.numpy as jnp
from jax import lax
from jax.experimental import pallas as pl
from jax.experimental.pallas import tpu as pltpu

F32 = jnp.float32
BF16 = jnp.bfloat16

NORM_EPS = 1e-6
ROPE_THETA = 10000.0

DIFF_HEADS = 8
DIFF_HEAD_DIM = 64
DIFF_V_DIM = 2 * DIFF_HEAD_DIM

GLA_HEADS = 4
GLA_DK = 128
GLA_DV = 256
GLA_GATE_TEMP = 16.0
GLA_BLOCK = 256
GLA_DIAG = 8

CONV_WIDTH = 3
FFN_HALO = 16
FFN_SUBTILES = 4

LANES = 128
VMEM_LIMIT_BYTES = 56 * 1024 * 1024


def _cparams(semantics):
    return pltpu.CompilerParams(dimension_semantics=semantics, vmem_limit_bytes=VMEM_LIMIT_BYTES)


def _rms(x, gain):
    ms = jnp.mean(x * x, axis=-1, keepdims=True)
    return x * lax.rsqrt(ms + NORM_EPS) * gain


NORM_PROJ_COLS = 1024


def _norm_proj_kernel(x_ref, g_ref, w_ref, *rest, rope):
    if rope:
        cos_ref, sin_ref, o_ref = rest
    else:
        (o_ref,) = rest
    xn = _rms(x_ref[...], g_ref[...]).astype(BF16)
    for cb in range(o_ref.shape[1] // NORM_PROJ_COLS):
        cols = slice(cb * NORM_PROJ_COLS, (cb + 1) * NORM_PROJ_COLS)
        acc = jnp.dot(xn, w_ref[:, cols].astype(BF16), preferred_element_type=F32)
        if not rope:
            o_ref[:, cols] = acc.astype(o_ref.dtype)
            continue
        c = cos_ref[cb]
        s = sin_ref[cb]
        for hh in range(NORM_PROJ_COLS // LANES):
            t = acc[:, hh * LANES:(hh + 1) * LANES]
            lo = cb * NORM_PROJ_COLS + hh * LANES
            o_ref[:, lo:lo + LANES] = (t * c + pltpu.roll(t, LANES // 2, 1) * s).astype(o_ref.dtype)


def _norm_proj(x, gain, w, layer, tables, *, tm, seq):
    m, d = x.shape
    n = w.shape[2]
    nseq = seq // tm
    rope = tables is not None
    in_specs = [
        pl.BlockSpec((tm, d), lambda i: (i, 0)),
        pl.BlockSpec((1, d), lambda i: (0, 0)),
        pl.BlockSpec((None, d, n), lambda i: (layer, 0, 0)),
    ]
    if rope:
        nb = tables[0].shape[0]
        assert nb == n // NORM_PROJ_COLS
        in_specs += [pl.BlockSpec((nb, tm, LANES), lambda i: (0, i % nseq, 0))] * 2
    return pl.pallas_call(
        functools.partial(_norm_proj_kernel, rope=rope),
        out_shape=jax.ShapeDtypeStruct((m, n), BF16),
        grid=(m // tm,),
        in_specs=in_specs,
        out_specs=pl.BlockSpec((tm, n), lambda i: (i, 0)),
        compiler_params=_cparams(("parallel",)),
        name="norm_proj",
    )(x, gain, w, *(tables if rope else ()))


ATTN_CHUNK = 256
ATTN_HEADS_PER_STEP = 4
ATTN_V_ROWS = DIFF_V_DIM + 16


def _diff_attn_kernel(lam_ref, sg_ref, q_ref, k_ref, v_ref, o_ref, vt_ref, s_ref, acc_ref, *, tq, lambda_init):
    qi = pl.program_id(2)
    n_kv_blocks = vt_ref.shape[1]
    n2 = 2 * tq
    ch = min(ATTN_CHUNK, tq)
    nch = tq // ch
    heads = range(ATTN_HEADS_PER_STEP)
    lanes = [slice(hh * LANES, (hh + 1) * LANES) for hh in heads]

    @pl.when(qi == 0)
    def _():
        for hh in heads:
            for c in range(n_kv_blocks):
                vt_ref[hh, c, 0:DIFF_V_DIM, :] = v_ref[0, c * tq:(c + 1) * tq, lanes[hh]].astype(F32).T.astype(BF16)
                ones_row = lax.broadcasted_iota(jnp.int32, (ATTN_V_ROWS - DIFF_V_DIM, tq), 0) == 0
                vt_ref[hh, c, DIFF_V_DIM:ATTN_V_ROWS, :] = jnp.where(ones_row, 1.0, 0.0).astype(BF16)

    qqt = []
    for hh in heads:
        qt = q_ref[0, :, lanes[hh]].astype(F32).T
        feat = lax.broadcasted_iota(jnp.int32, qt.shape, 0)
        map1 = (feat % DIFF_HEAD_DIM) < (DIFF_HEAD_DIM // 2)
        qqt.append(jnp.concatenate([jnp.where(map1, qt, 0.0), jnp.where(map1, 0.0, qt)], axis=1).astype(BF16))
    acc_ref[...] = jnp.zeros(acc_ref.shape, F32)

    def scores(hh, blk, slot, masked):
        cmx8 = None
        for c in range(nch):
            start = pl.multiple_of(blk * tq + c * ch, ch)
            s = jnp.dot(k_ref[0, pl.ds(start, ch), lanes[hh]], qqt[hh], preferred_element_type=F32)
            if masked:
                kv_pos = lax.broadcasted_iota(jnp.int32, s.shape, 0) + c * ch
                col = lax.broadcasted_iota(jnp.int32, s.shape, 1)
                q_pos = jnp.where(col >= tq, col - tq, col)
                s = jnp.where(kv_pos <= q_pos, s, -jnp.inf)
            s_ref[hh, slot, c * ch:(c + 1) * ch, :] = s
            part = jnp.max(s.reshape(ch // 8, 8, n2), axis=0)
            cmx8 = part if cmx8 is None else jnp.maximum(cmx8, part)
        return cmx8

    def softmax_pv(hh, blk, slot, m, cmx8):
        m_new = jnp.maximum(m, jnp.max(cmx8, axis=0, keepdims=True))
        alpha = jnp.exp2(m - m_new)
        pv = None
        for c in range(nch):
            p = jnp.exp2((s_ref[hh, slot, c * ch:(c + 1) * ch, :] - m_new).astype(BF16))
            d = jnp.dot(vt_ref[hh, blk, :, c * ch:(c + 1) * ch], p, preferred_element_type=F32)
            pv = d if pv is None else pv + d
        acc_ref[hh] = alpha * acc_ref[hh] + pv
        return (m_new,)

    cmx8_0 = [scores(hh, qi, 0, True) for hh in heads]

    def body(t, carry):
        slot = lax.rem(t, 2)
        stats = [softmax_pv(hh, jnp.where(t == 0, qi, t - 1), slot, *carry[hh]) for hh in heads]
        return tuple(stats[hh] + (scores(hh, t, 1 - slot, False),) for hh in heads)

    init = tuple((jnp.full((1, n2), -jnp.inf, F32), cmx8_0[hh]) for hh in heads)
    carry = lax.fori_loop(0, qi, body, init)

    lv = lam_ref[...]
    lam = (jnp.exp(jnp.sum(lv[0:1] * lv[1:2], axis=1, keepdims=True))
           - jnp.exp(jnp.sum(lv[2:3] * lv[3:4], axis=1, keepdims=True)) + lambda_init)
    for hh in heads:
        softmax_pv(hh, jnp.maximum(qi - 1, 0), lax.rem(qi, 2), *carry[hh])
    for hh in heads:
        l = acc_ref[hh, DIFF_V_DIM:DIFF_V_DIM + 1, :]
        o1 = acc_ref[hh, 0:DIFF_V_DIM, 0:tq] / l[:, 0:tq]
        o2 = acc_ref[hh, 0:DIFF_V_DIM, tq:n2] / l[:, tq:n2]
        o = o1 - lam * o2
        ms = jnp.mean(o * o, axis=0, keepdims=True)
        o = o * lax.rsqrt(ms + NORM_EPS) * sg_ref[...] * (1.0 - lambda_init)
        o_ref[0, :, lanes[hh]] = o.T.astype(o_ref.dtype)


def _diff_attention(qkv, lam_vecs, subln_gain, *, lambda_init, tq):
    b, s, _ = qkv.shape
    hps = ATTN_HEADS_PER_STEP
    groups = DIFF_HEADS // hps
    width = hps * LANES
    kern = functools.partial(_diff_attn_kernel, tq=tq, lambda_init=lambda_init)
    return pl.pallas_call(
        kern,
        out_shape=jax.ShapeDtypeStruct((b, s, DIFF_HEADS * DIFF_V_DIM), BF16),
        grid=(b, groups, s // tq),
        in_specs=[
            pl.BlockSpec((4, DIFF_HEAD_DIM), lambda bi, gi, qi: (0, 0)),
            pl.BlockSpec((DIFF_V_DIM, 1), lambda bi, gi, qi: (0, 0)),
            pl.BlockSpec((1, tq, width), lambda bi, gi, qi: (bi, qi, gi)),
            pl.BlockSpec((1, s, width), lambda bi, gi, qi: (bi, 0, groups + gi)),
            pl.BlockSpec((1, s, width), lambda bi, gi, qi: (bi, 0, 2 * groups + gi)),
        ],
        out_specs=pl.BlockSpec((1, tq, width), lambda bi, gi, qi: (bi, qi, gi)),
        scratch_shapes=[
            pltpu.VMEM((hps, s // tq, ATTN_V_ROWS, tq), BF16),
            pltpu.VMEM((hps, 2, tq, 2 * tq), F32),
            pltpu.VMEM((hps, ATTN_V_ROWS, 2 * tq), F32),
        ],
        compiler_params=_cparams(("parallel", "parallel", "arbitrary")),
        name="diff_attn",
    )(lam_vecs, subln_gain, qkv, qkv, qkv)


def _proj_res_kernel(a_ref, w_ref, res_ref, o_ref):
    o_ref[...] = res_ref[...] + jnp.dot(a_ref[...], w_ref[...].astype(BF16), preferred_element_type=F32)


def _proj_res(a, w, layer, res, *, tm):
    m, k = a.shape
    n = w.shape[2]
    return pl.pallas_call(
        _proj_res_kernel,
        out_shape=jax.ShapeDtypeStruct((m, n), F32),
        grid=(m // tm,),
        in_specs=[
            pl.BlockSpec((tm, k), lambda i: (i, 0)),
            pl.BlockSpec((None, k, n), lambda i: (layer, 0, 0)),
            pl.BlockSpec((tm, n), lambda i: (i, 0)),
        ],
        out_specs=pl.BlockSpec((tm, n), lambda i: (i, 0)),
        compiler_params=_cparams(("parallel",)),
        name="proj_res",
    )(a, w, res)


def _gla_gate_kernel(x_ref, g_ref, w1_ref, w2_ref, b_ref, o_ref):
    xn = _rms(x_ref[...], g_ref[...]).astype(BF16)
    t = jnp.dot(xn, w1_ref[...], preferred_element_type=F32)
    z = jnp.dot(t, w2_ref[...], preferred_element_type=F32, precision=lax.Precision.HIGHEST) + b_ref[...]
    log_sig = jnp.minimum(z, 0.0) - jnp.log1p(jnp.exp(-jnp.abs(z)))
    o_ref[...] = log_sig * (1.0 / GLA_GATE_TEMP)


def _gla_gate(x, gain, w1, w2, bias, *, tm):
    m, d = x.shape
    r = w1.shape[1]
    n = w2.shape[1]
    return pl.pallas_call(
        _gla_gate_kernel,
        out_shape=jax.ShapeDtypeStruct((m, n), F32),
        grid=(m // tm,),
        in_specs=[
            pl.BlockSpec((tm, d), lambda i: (i, 0)),
            pl.BlockSpec((1, d), lambda i: (0, 0)),
            pl.BlockSpec((d, r), lambda i: (0, 0)),
            pl.BlockSpec((r, n), lambda i: (0, 0)),
            pl.BlockSpec((1, n), lambda i: (0, 0)),
        ],
        out_specs=pl.BlockSpec((tm, n), lambda i: (i, 0)),
        compiler_params=_cparams(("parallel",)),
        name="gla_gate",
    )(x, gain, w1, w2, bias)


def _split3(x):
    hi = x.astype(BF16)
    r1 = x - hi.astype(F32)
    mid = r1.astype(BF16)
    lo = (r1 - mid.astype(F32)).astype(BF16)
    return hi, mid, lo


def _gla_kernel(q_ref, k_ref, v_ref, r_ref, g_ref, ng_ref, o_ref, state_ref):
    t_blk = q_ref.shape[1]
    hk = GLA_HEADS * GLA_DK

    @pl.when(pl.program_id(1) == 0)
    def _():
        state_ref[...] = jnp.zeros(state_ref.shape, F32)

    ri = lax.broadcasted_iota(jnp.int32, (t_blk, t_blk), 0)
    ci = lax.broadcasted_iota(jnp.int32, (t_blk, t_blk), 1)

    tri = jnp.where(ci <= ri, 1.0, 0.0).astype(BF16)
    g_hi, g_mid, g_lo = _split3(g_ref[0])
    b = (jnp.dot(tri, g_hi, preferred_element_type=F32)
         + jnp.dot(tri, g_mid, preferred_element_type=F32)
         + jnp.dot(tri, g_lo, preferred_element_type=F32))
    b = b * math.log2(math.e)

    q = q_ref[0].astype(F32) * (GLA_DK ** -0.5)
    k = k_ref[0].astype(F32)

    row = lax.broadcasted_iota(jnp.int32, (t_blk, hk), 0)
    b_last = b[t_blk - 1:t_blk, :]
    q_in = (q * jnp.exp2(b)).astype(BF16)
    k_out = (k * jnp.exp2(b_last - b)).astype(BF16)
    e_last = jnp.exp2(b_last)

    level = t_blk // 2
    levels = []
    while level >= GLA_DIAG:
        grp = 2 * level
        b3 = b.reshape(t_blk // grp, grp, hk)
        pivot = jnp.broadcast_to(b3[:, level - 1:level, :], b3.shape).reshape(t_blk, hk)
        upper = (row % grp) >= level
        e = jnp.exp2(-jnp.abs(b - pivot))
        qt = jnp.where(upper, q * e, 0.0).astype(BF16)
        kt = jnp.where(upper, 0.0, k * e).astype(BF16)
        levels.append((grp, qt, kt))
        level //= 2

    nd = t_blk // GLA_DIAG
    b3 = b.reshape(nd, GLA_DIAG, hk)
    q3 = q.reshape(nd, GLA_DIAG, hk)
    k3 = k.reshape(nd, GLA_DIAG, hk)
    sel_shape = (GLA_DIAG * GLA_DK, t_blk)
    sel = jnp.where(lax.broadcasted_iota(jnp.int32, sel_shape, 0) // GLA_DK
                    == lax.broadcasted_iota(jnp.int32, sel_shape, 1) % GLA_DIAG, 1.0, 0.0).astype(BF16)
    diag_mask = ((ri // GLA_DIAG) == (ci // GLA_DIAG)) & (ci <= ri)

    eye = (lax.broadcasted_iota(jnp.int32, (GLA_DK, GLA_DK), 0)
           == lax.broadcasted_iota(jnp.int32, (GLA_DK, GLA_DK), 1))

    diag_terms = []
    for j in range(GLA_DIAG):
        d = jnp.minimum(b3 - b3[:, j:j + 1, :], 0.0)
        pj = (jnp.exp2(d) * q3 * k3[:, j:j + 1, :]).reshape(t_blk, hk)
        diag_terms.append(pj)

    ng = ng_ref[...]
    for h in range(GLA_HEADS):
        ks = slice(h * GLA_DK, (h + 1) * GLA_DK)
        vs = slice(h * GLA_DV, (h + 1) * GLA_DV)
        a = jnp.zeros((t_blk, t_blk), F32)
        for grp, qt, kt in levels:
            s_l = lax.dot_general(qt[:, ks], kt[:, ks], (((1,), (1,)), ((), ())),
                                  preferred_element_type=F32)
            if grp == t_blk:
                a = a + s_l
            else:
                a = a + jnp.where((ri // grp) == (ci // grp), s_l, 0.0)
        stacked = jnp.concatenate([diag_terms[j][:, ks] for j in range(GLA_DIAG)], axis=1)
        a = jnp.where(diag_mask, jnp.dot(stacked.astype(BF16), sel, preferred_element_type=F32), a)

        v_h = v_ref[0, :, vs]
        state = state_ref[h]
        o = (jnp.dot(q_in[:, ks], state.astype(BF16), preferred_element_type=F32)
             + jnp.dot(a.astype(BF16), v_h, preferred_element_type=F32))

        e_col = jnp.sum(jnp.where(eye, jnp.broadcast_to(e_last[:, ks], (GLA_DK, GLA_DK)), 0.0),
                        axis=1, keepdims=True)
        upd = lax.dot_general(k_out[:, ks], v_h, (((0,), (0,)), ((), ())),
                              preferred_element_type=F32)
        state_ref[h] = e_col * state + upd

        r_h = r_ref[0, :, vs].astype(F32)
        gate = r_h / (1.0 + jnp.exp(-r_h))
        o_ref[0, :, vs] = (_rms(o, ng) * gate).astype(o_ref.dtype)


def _gla(proj, g, norm_gain):
    b, s, _ = proj.shape
    t = GLA_BLOCK
    hk = GLA_HEADS * GLA_DK
    hv = GLA_HEADS * GLA_DV
    return pl.pallas_call(
        _gla_kernel,
        out_shape=jax.ShapeDtypeStruct((b, s, hv), BF16),
        grid=(b, s // t),
        in_specs=[
            pl.BlockSpec((1, t, hk), lambda bi, ti: (bi, ti, 0)),
            pl.BlockSpec((1, t, hk), lambda bi, ti: (bi, ti, 1)),
            pl.BlockSpec((1, t, hv), lambda bi, ti: (bi, ti, 1)),
            pl.BlockSpec((1, t, hv), lambda bi, ti: (bi, ti, 2)),
            pl.BlockSpec((1, t, hk), lambda bi, ti: (bi, ti, 0)),
            pl.BlockSpec((1, GLA_DV), lambda bi, ti: (0, 0)),
        ],
        out_specs=pl.BlockSpec((1, t, hv), lambda bi, ti: (bi, ti, 0)),
        scratch_shapes=[pltpu.VMEM((GLA_HEADS, GLA_DK, GLA_DV), F32)],
        compiler_params=_cparams(("parallel", "arbitrary")),
        name="gla",
    )(proj, proj, proj, proj, g, norm_gain)


def _ffn_kernel(x_ref, halo_ref, g_ref, wg_ref, wu_ref, cwg_ref, cwu_ref, cbg_ref, cbu_ref, wd_ref,
                fg_ref, o_ref, xn_ref, acc_ref, slab_ref, *, tiles_per_seq, final_norm):
    i = pl.program_id(0)
    j = pl.program_id(1)
    nj = pl.num_programs(1)
    tm, d = x_ref.shape
    rows = tm // FFN_SUBTILES
    ext = rows + FFN_HALO
    seg = ext // 8
    nslab = d // LANES
    subs = range(FFN_SUBTILES)

    @pl.when(j == 0)
    def _():
        gain = g_ref[...]
        for s in subs:
            if s == 0:
                halo = jnp.where(i % tiles_per_seq == 0, 0.0, _rms(halo_ref[...], gain))
            else:
                halo = _rms(x_ref[s * rows - FFN_HALO:s * rows, :], gain)
            xn = _rms(x_ref[s * rows:(s + 1) * rows, :], gain)
            for c in range(nslab):
                slab_ref[c, 0:FFN_HALO, :] = halo[:, c * LANES:(c + 1) * LANES]
                slab_ref[c, FFN_HALO:ext, :] = xn[:, c * LANES:(c + 1) * LANES]
            for a2 in range(seg // 2):
                grp = [jnp.concatenate([slab_ref[c, pl.ds(2 * a2 + r, 8, stride=seg), :] for c in range(nslab)],
                                       axis=1) for r in range(2)]
                xn_ref[s, 16 * a2:16 * a2 + 16, :] = jnp.concatenate(grp, axis=0).astype(BF16)
        acc_ref[...] = jnp.zeros(acc_ref.shape, F32)

    def conv(u, cw_ref, cb_ref):
        cw = cw_ref[...]
        u3 = u.reshape(seg, 8, u.shape[-1])
        wrap1 = pltpu.roll(u3[seg - 1], 1, 0)[None]
        wrap2 = pltpu.roll(u3[seg - 2], 1, 0)[None]
        prev1 = jnp.concatenate([wrap1, u3[:seg - 1]], axis=0)
        prev2 = jnp.concatenate([wrap2, wrap1, u3[:seg - 2]], axis=0)
        return cw[2:3, :] * u3 + cw[1:2, :] * prev1 + cw[0:1, :] * prev2 + cb_ref[...]

    w_gate = wg_ref[...].astype(BF16)
    w_up = wu_ref[...].astype(BF16)
    w_down = wd_ref[...].astype(BF16)
    ups = [(jnp.dot(xn_ref[s], w_gate, preferred_element_type=F32),
            jnp.dot(xn_ref[s], w_up, preferred_element_type=F32)) for s in subs]
    acts = []
    for s in subs:
        gate = conv(ups[s][0], cwg_ref, cbg_ref)
        up = conv(ups[s][1], cwu_ref, cbu_ref)
        acts.append(((gate / (1.0 + jnp.exp(-gate))) * up).reshape(ext, gate.shape[-1]).astype(BF16))
    for s in subs:
        acc_ref[s] += jnp.dot(acts[s], w_down, preferred_element_type=F32)

    @pl.when(j == nj - 1)
    def _():
        for s in subs:
            for a in range(seg):
                for c in range(nslab):
                    slab_ref[c, pl.ds(a, 8, stride=seg), :] = acc_ref[s, 8 * a:8 * a + 8, c * LANES:(c + 1) * LANES]
            y = (x_ref[s * rows:(s + 1) * rows, :]
                 + jnp.concatenate([slab_ref[c, FFN_HALO:ext, :] for c in range(nslab)], axis=1))
            if final_norm:
                y = _rms(y, fg_ref[...])
            o_ref[s * rows:(s + 1) * rows, :] = y


def _ffn(x, gain, w_up, conv_w, conv_b, w_down, layer, final_gain, *, tm, tf, seq, final_norm):
    m, d = x.shape
    f = w_down.shape[1]
    nf = f // tf
    tiles_per_seq = seq // tm
    halo_blocks = tm // FFN_HALO
    kern = functools.partial(_ffn_kernel, tiles_per_seq=tiles_per_seq, final_norm=final_norm)
    return pl.pallas_call(
        kern,
        out_shape=jax.ShapeDtypeStruct((m, d), F32),
        grid=(m // tm, nf),
        in_specs=[
            pl.BlockSpec((tm, d), lambda i, j: (i, 0)),
            pl.BlockSpec((FFN_HALO, d), lambda i, j: (jnp.maximum(i * halo_blocks - 1, 0), 0)),
            pl.BlockSpec((None, 1, d), lambda i, j: (layer, 0, 0)),
            pl.BlockSpec((None, d, tf), lambda i, j: (layer, 0, j)),
            pl.BlockSpec((None, d, tf), lambda i, j: (layer, 0, nf + j)),
            pl.BlockSpec((None, CONV_WIDTH, tf), lambda i, j: (layer, 0, j)),
            pl.BlockSpec((None, CONV_WIDTH, tf), lambda i, j: (layer, 0, nf + j)),
            pl.BlockSpec((None, 1, tf), lambda i, j: (layer, 0, j)),
            pl.BlockSpec((None, 1, tf), lambda i, j: (layer, 0, nf + j)),
            pl.BlockSpec((None, tf, d), lambda i, j: (layer, j, 0)),
            pl.BlockSpec((1, d), lambda i, j: (0, 0)),
        ],
        out_specs=pl.BlockSpec((tm, d), lambda i, j: (i, 0)),
        scratch_shapes=[pltpu.VMEM((FFN_SUBTILES, tm // FFN_SUBTILES + FFN_HALO, d), BF16),
                        pltpu.VMEM((FFN_SUBTILES, tm // FFN_SUBTILES + FFN_HALO, d), F32),
                        pltpu.VMEM((d // LANES, tm // FFN_SUBTILES + FFN_HALO, LANES), F32)],
        compiler_params=_cparams(("parallel", "arbitrary")),
        name="ffn",
    )(x, x, gain, w_up, w_up, conv_w, conv_w, conv_b, conv_b, w_down, final_gain)


def _rope_tables(seq, q_scale):
    half = DIFF_HEAD_DIM // 2
    inv = 1.0 / (ROPE_THETA ** (jnp.arange(0, DIFF_HEAD_DIM, 2, dtype=F32) / DIFF_HEAD_DIM))
    ang = jnp.arange(seq, dtype=F32)[:, None] * inv[None, :]
    cos, sin = jnp.cos(ang), jnp.sin(ang)
    cos_t = jnp.tile(cos, (1, LANES // half))
    sin_t = jnp.concatenate([-sin, -sin, sin, sin], axis=1)
    cos3 = jnp.stack([cos_t * q_scale, cos_t, jnp.ones_like(cos_t)])
    sin3 = jnp.stack([sin_t * q_scale, sin_t, jnp.zeros_like(sin_t)])
    return cos3, sin3


def _reorder_qk_columns(w_qkv):
    layers, d, _ = w_qkv.shape
    width = DIFF_HEADS * DIFF_V_DIM
    half = DIFF_HEAD_DIM // 2
    qk = w_qkv[:, :, :2 * width].reshape(layers, d, 2, DIFF_HEADS, 2, 2, half)
    qk = qk.transpose(0, 1, 2, 3, 5, 4, 6).reshape(layers, d, 2 * width)
    return jnp.concatenate([qk, w_qkv[:, :, 2 * width:]], axis=2).astype(BF16)


def _trunk(x, norm_mix, norm_ffn, norm_final, diff_w_qkv, diff_w_o, diff_lambda, diff_subln,
           gla_w_in, gla_w_a1, gla_w_a2, gla_b_a, gla_norm, gla_w_o,
           ffn_w_up, ffn_conv_w, ffn_conv_b, ffn_w_down, *, tm, tp, tq, tf):
    bsz, seq, d = x.shape
    depth = norm_mix.shape[0]
    m = bsz * seq
    tables = _rope_tables(seq, DIFF_HEAD_DIM ** -0.5 * math.log2(math.e))
    w_qkv = _reorder_qk_columns(diff_w_qkv)
    h = x.reshape(m, d)
    rank = gla_w_a1.shape[-1]
    for layer in range(depth):
        jdx = layer // 2
        gain = norm_mix[layer].reshape(1, d)
        if layer % 2 == 0:
            lambda_init = 0.8 - 0.6 * math.exp(-0.3 * layer)
            qkv = _norm_proj(h, gain, w_qkv, jdx, tables, tm=tp, seq=seq)
            o = _diff_attention(qkv.reshape(bsz, seq, 3 * d), diff_lambda[jdx],
                                diff_subln[jdx].reshape(DIFF_V_DIM, 1),
                                lambda_init=lambda_init, tq=tq)
            h = _proj_res(o.reshape(m, d), diff_w_o, jdx, h, tm=tm)
        else:
            proj = _norm_proj(h, gain, gla_w_in, jdx, None, tm=tp, seq=seq)
            w1 = jnp.pad(gla_w_a1[jdx], ((0, 0), (0, LANES - rank))).astype(BF16)
            w2 = jnp.pad(gla_w_a2[jdx], ((0, LANES - rank), (0, 0)))
            g = _gla_gate(h, gain, w1, w2, gla_b_a[jdx].reshape(1, -1), tm=tm)
            o = _gla(proj.reshape(bsz, seq, 3 * d), g.reshape(bsz, seq, -1),
                     gla_norm[jdx].reshape(1, GLA_DV))
            h = _proj_res(o.reshape(m, d), gla_w_o, jdx, h, tm=tm)
        h = _ffn(h, norm_ffn.reshape(depth, 1, d), ffn_w_up, ffn_conv_w, ffn_conv_b.reshape(depth, 1, -1),
                 ffn_w_down, layer, norm_final.reshape(1, d), tm=tm, tf=tf, seq=seq,
                 final_norm=(layer == depth - 1))
    return h.reshape(bsz, seq, d)


def kernel(x, norm_mix, norm_ffn, norm_final, diff_w_qkv, diff_w_o, diff_lambda, diff_subln, gla_w_in, gla_w_a1, gla_w_a2, gla_b_a, gla_norm, gla_w_o, ffn_w_up, ffn_conv_w, ffn_conv_b, ffn_w_down):
    return _trunk(x, norm_mix, norm_ffn, norm_final, diff_w_qkv, diff_w_o, diff_lambda, diff_subln,
                  gla_w_in, gla_w_a1, gla_w_a2, gla_b_a, gla_norm, gla_w_o,
                  ffn_w_up, ffn_conv_w, ffn_conv_b, ffn_w_down, tm=1024, tp=512, tq=512, tf=256)
```

```python
import functools
import math

import jax
import jax.numpy as jnp
from jax import lax
from jax.experimental import pallas as pl
from jax.experimental.pallas import tpu as pltpu

F32 = jnp.float32
BF16 = jnp.bfloat16

NORM_EPS = 1e-6
ROPE_THETA = 10000.0

DIFF_HEADS = 8
DIFF_HEAD_DIM = 64
DIFF_V_DIM = 2 * DIFF_HEAD_DIM

GLA_HEADS = 4
GLA_DK = 128
GLA_DV = 256
GLA_GATE_TEMP = 16.0
GLA_BLOCK = 256
GLA_DIAG = 8

CONV_WIDTH = 3
FFN_HALO = 16
FFN_SUBTILES = 2

LANES = 128
VMEM_LIMIT_BYTES = 56 * 1024 * 1024


def _cparams(semantics):
    return pltpu.CompilerParams(dimension_semantics=semantics, vmem_limit_bytes=VMEM_LIMIT_BYTES)


def _rms(x, gain):
    ms = jnp.mean(x * x, axis=-1, keepdims=True)
    return x * lax.rsqrt(ms + NORM_EPS) * gain


NORM_PROJ_COLS = 1024


def _norm_proj_kernel(x_ref, g_ref, w_ref, *rest, rope):
    if rope:
        cos_ref, sin_ref, o_ref = rest
    else:
        (o_ref,) = rest
    xn = _rms(x_ref[...], g_ref[...]).astype(BF16)
    for cb in range(o_ref.shape[1] // NORM_PROJ_COLS):
        cols = slice(cb * NORM_PROJ_COLS, (cb + 1) * NORM_PROJ_COLS)
        acc = jnp.dot(xn, w_ref[:, cols].astype(BF16), preferred_element_type=F32)
        if not rope:
            o_ref[:, cols] = acc.astype(o_ref.dtype)
            continue
        c = cos_ref[cb]
        s = sin_ref[cb]
        for hh in range(NORM_PROJ_COLS // LANES):
            t = acc[:, hh * LANES:(hh + 1) * LANES]
            lo = cb * NORM_PROJ_COLS + hh * LANES
            o_ref[:, lo:lo + LANES] = (t * c + pltpu.roll(t, LANES // 2, 1) * s).astype(o_ref.dtype)


def _norm_proj(x, gain, w, layer, tables, *, tm, seq):
    m, d = x.shape
    n = w.shape[2]
    nseq = seq // tm
    rope = tables is not None
    in_specs = [
        pl.BlockSpec((tm, d), lambda i: (i, 0)),
        pl.BlockSpec((1, d), lambda i: (0, 0)),
        pl.BlockSpec((None, d, n), lambda i: (layer, 0, 0)),
    ]
    if rope:
        nb = tables[0].shape[0]
        assert nb == n // NORM_PROJ_COLS
        in_specs += [pl.BlockSpec((nb, tm, LANES), lambda i: (0, i % nseq, 0))] * 2
    return pl.pallas_call(
        functools.partial(_norm_proj_kernel, rope=rope),
        out_shape=jax.ShapeDtypeStruct((m, n), BF16),
        grid=(m // tm,),
        in_specs=in_specs,
        out_specs=pl.BlockSpec((tm, n), lambda i: (i, 0)),
        compiler_params=_cparams(("parallel",)),
        name="norm_proj",
    )(x, gain, w, *(tables if rope else ()))


ATTN_CHUNK = 256
ATTN_HEADS_PER_STEP = 4
ATTN_V_ROWS = DIFF_V_DIM + 16


def _diff_attn_kernel(lam_ref, sg_ref, q_ref, k_ref, v_ref, o_ref, vt_ref, s_ref, acc_ref, *, tq, lambda_init):
    qi = pl.program_id(2)
    n_kv_blocks = vt_ref.shape[1]
    n2 = 2 * tq
    ch = min(ATTN_CHUNK, tq)
    nch = tq // ch
    heads = range(ATTN_HEADS_PER_STEP)
    lanes = [slice(hh * LANES, (hh + 1) * LANES) for hh in heads]

    @pl.when(qi == 0)
    def _():
        for hh in heads:
            for c in range(n_kv_blocks):
                vt_ref[hh, c, 0:DIFF_V_DIM, :] = v_ref[0, c * tq:(c + 1) * tq, lanes[hh]].astype(F32).T.astype(BF16)
                ones_row = lax.broadcasted_iota(jnp.int32, (ATTN_V_ROWS - DIFF_V_DIM, tq), 0) == 0
                vt_ref[hh, c, DIFF_V_DIM:ATTN_V_ROWS, :] = jnp.where(ones_row, 1.0, 0.0).astype(BF16)

    qqt = []
    for hh in heads:
        qt = q_ref[0, :, lanes[hh]].astype(F32).T
        feat = lax.broadcasted_iota(jnp.int32, qt.shape, 0)
        map1 = (feat % DIFF_HEAD_DIM) < (DIFF_HEAD_DIM // 2)
        qqt.append(jnp.concatenate([jnp.where(map1, qt, 0.0), jnp.where(map1, 0.0, qt)], axis=1).astype(BF16))
    acc_ref[...] = jnp.zeros(acc_ref.shape, F32)

    def scores(hh, blk, slot, masked):
        cmx8 = None
        for c in range(nch):
            start = pl.multiple_of(blk * tq + c * ch, ch)
            s = jnp.dot(k_ref[0, pl.ds(start, ch), lanes[hh]], qqt[hh], preferred_element_type=F32)
            if masked:
                kv_pos = lax.broadcasted_iota(jnp.int32, s.shape, 0) + c * ch
                col = lax.broadcasted_iota(jnp.int32, s.shape, 1)
                q_pos = jnp.where(col >= tq, col - tq, col)
                s = jnp.where(kv_pos <= q_pos, s, -jnp.inf)
            s_ref[hh, slot, c * ch:(c + 1) * ch, :] = s
            part = jnp.max(s.reshape(ch // 8, 8, n2), axis=0)
            cmx8 = part if cmx8 is None else jnp.maximum(cmx8, part)
        return cmx8

    def softmax_pv(hh, blk, slot, m, cmx8):
        m_new = jnp.maximum(m, jnp.max(cmx8, axis=0, keepdims=True))
        alpha = jnp.exp2(m - m_new)
        pv = None
        for c in range(nch):
            p = jnp.exp2((s_ref[hh, slot, c * ch:(c + 1) * ch, :] - m_new).astype(BF16))
            d = jnp.dot(vt_ref[hh, blk, :, c * ch:(c + 1) * ch], p, preferred_element_type=F32)
            pv = d if pv is None else pv + d
        acc_ref[hh] = alpha * acc_ref[hh] + pv
        return (m_new,)

    cmx8_0 = [scores(hh, qi, 0, True) for hh in heads]

    def body(t, carry):
        slot = lax.rem(t, 2)
        stats = [softmax_pv(hh, jnp.where(t == 0, qi, t - 1), slot, *carry[hh]) for hh in heads]
        return tuple(stats[hh] + (scores(hh, t, 1 - slot, False),) for hh in heads)

    init = tuple((jnp.full((1, n2), -jnp.inf, F32), cmx8_0[hh]) for hh in heads)
    carry = lax.fori_loop(0, qi, body, init)

    lv = lam_ref[...]
    lam = (jnp.exp(jnp.sum(lv[0:1] * lv[1:2], axis=1, keepdims=True))
           - jnp.exp(jnp.sum(lv[2:3] * lv[3:4], axis=1, keepdims=True)) + lambda_init)
    for hh in heads:
        softmax_pv(hh, jnp.maximum(qi - 1, 0), lax.rem(qi, 2), *carry[hh])
    for hh in heads:
        l = acc_ref[hh, DIFF_V_DIM:DIFF_V_DIM + 1, :]
        o1 = acc_ref[hh, 0:DIFF_V_DIM, 0:tq] / l[:, 0:tq]
        o2 = acc_ref[hh, 0:DIFF_V_DIM, tq:n2] / l[:, tq:n2]
        o = o1 - lam * o2
        ms = jnp.mean(o * o, axis=0, keepdims=True)
        o = o * lax.rsqrt(ms + NORM_EPS) * sg_ref[...] * (1.0 - lambda_init)
        o_ref[0, :, lanes[hh]] = o.T.astype(o_ref.dtype)


def _diff_attention(qkv, lam_vecs, subln_gain, *, lambda_init, tq):
    b, s, _ = qkv.shape
    hps = ATTN_HEADS_PER_STEP
    groups = DIFF_HEADS // hps
    width = hps * LANES
    kern = functools.partial(_diff_attn_kernel, tq=tq, lambda_init=lambda_init)
    return pl.pallas_call(
        kern,
        out_shape=jax.ShapeDtypeStruct((b, s, DIFF_HEADS * DIFF_V_DIM), BF16),
        grid=(b, groups, s // tq),
        in_specs=[
            pl.BlockSpec((4, DIFF_HEAD_DIM), lambda bi, gi, qi: (0, 0)),
            pl.BlockSpec((DIFF_V_DIM, 1), lambda bi, gi, qi: (0, 0)),
            pl.BlockSpec((1, tq, width), lambda bi, gi, qi: (bi, qi, gi)),
            pl.BlockSpec((1, s, width), lambda bi, gi, qi: (bi, 0, groups + gi)),
            pl.BlockSpec((1, s, width), lambda bi, gi, qi: (bi, 0, 2 * groups + gi)),
        ],
        out_specs=pl.BlockSpec((1, tq, width), lambda bi, gi, qi: (bi, qi, gi)),
        scratch_shapes=[
            pltpu.VMEM((hps, s // tq, ATTN_V_ROWS, tq), BF16),
            pltpu.VMEM((hps, 2, tq, 2 * tq), F32),
            pltpu.VMEM((hps, ATTN_V_ROWS, 2 * tq), F32),
        ],
        compiler_params=_cparams(("parallel", "parallel", "arbitrary")),
        name="diff_attn",
    )(lam_vecs, subln_gain, qkv, qkv, qkv)


def _proj_res_kernel(a_ref, w_ref, res_ref, o_ref):
    o_ref[...] = res_ref[...] + jnp.dot(a_ref[...], w_ref[...].astype(BF16), preferred_element_type=F32)


def _proj_res(a, w, layer, res, *, tm):
    m, k = a.shape
    n = w.shape[2]
    return pl.pallas_call(
        _proj_res_kernel,
        out_shape=jax.ShapeDtypeStruct((m, n), F32),
        grid=(m // tm,),
        in_specs=[
            pl.BlockSpec((tm, k), lambda i: (i, 0)),
            pl.BlockSpec((None, k, n), lambda i: (layer, 0, 0)),
            pl.BlockSpec((tm, n), lambda i: (i, 0)),
        ],
        out_specs=pl.BlockSpec((tm, n), lambda i: (i, 0)),
        compiler_params=_cparams(("parallel",)),
        name="proj_res",
    )(a, w, res)


def _gla_gate_kernel(x_ref, g_ref, w1_ref, w2_ref, b_ref, o_ref):
    xn = _rms(x_ref[...], g_ref[...]).astype(BF16)
    t = jnp.dot(xn, w1_ref[...], preferred_element_type=F32)
    z = jnp.dot(t, w2_ref[...], preferred_element_type=F32, precision=lax.Precision.HIGHEST) + b_ref[...]
    log_sig = jnp.minimum(z, 0.0) - jnp.log1p(jnp.exp(-jnp.abs(z)))
    o_ref[...] = log_sig * (1.0 / GLA_GATE_TEMP)


def _gla_gate(x, gain, w1, w2, bias, *, tm):
    m, d = x.shape
    r = w1.shape[1]
    n = w2.shape[1]
    return pl.pallas_call(
        _gla_gate_kernel,
        out_shape=jax.ShapeDtypeStruct((m, n), F32),
        grid=(m // tm,),
        in_specs=[
            pl.BlockSpec((tm, d), lambda i: (i, 0)),
            pl.BlockSpec((1, d), lambda i: (0, 0)),
            pl.BlockSpec((d, r), lambda i: (0, 0)),
            pl.BlockSpec((r, n), lambda i: (0, 0)),
            pl.BlockSpec((1, n), lambda i: (0, 0)),
        ],
        out_specs=pl.BlockSpec((tm, n), lambda i: (i, 0)),
        compiler_params=_cparams(("parallel",)),
        name="gla_gate",
    )(x, gain, w1, w2, bias)


def _split3(x):
    hi = x.astype(BF16)
    r1 = x - hi.astype(F32)
    mid = r1.astype(BF16)
    lo = (r1 - mid.astype(F32)).astype(BF16)
    return hi, mid, lo


def _gla_kernel(q_ref, k_ref, v_ref, r_ref, g_ref, ng_ref, o_ref, state_ref):
    t_blk = q_ref.shape[1]
    hk = GLA_HEADS * GLA_DK

    @pl.when(pl.program_id(1) == 0)
    def _():
        state_ref[...] = jnp.zeros(state_ref.shape, F32)

    ri = lax.broadcasted_iota(jnp.int32, (t_blk, t_blk), 0)
    ci = lax.broadcasted_iota(jnp.int32, (t_blk, t_blk), 1)

    tri = jnp.where(ci <= ri, 1.0, 0.0).astype(BF16)
    g_hi, g_mid, g_lo = _split3(g_ref[0])
    b = (jnp.dot(tri, g_hi, preferred_element_type=F32)
         + jnp.dot(tri, g_mid, preferred_element_type=F32)
         + jnp.dot(tri, g_lo, preferred_element_type=F32))
    b = b * math.log2(math.e)

    q = q_ref[0].astype(F32) * (GLA_DK ** -0.5)
    k = k_ref[0].astype(F32)

    row = lax.broadcasted_iota(jnp.int32, (t_blk, hk), 0)
    b_last = b[t_blk - 1:t_blk, :]
    q_in = (q * jnp.exp2(b)).astype(BF16)
    k_out = (k * jnp.exp2(b_last - b)).astype(BF16)
    e_last = jnp.exp2(b_last)

    level = t_blk // 2
    levels = []
    while level >= GLA_DIAG:
        grp = 2 * level
        b3 = b.reshape(t_blk // grp, grp, hk)
        pivot = jnp.broadcast_to(b3[:, level - 1:level, :], b3.shape).reshape(t_blk, hk)
        upper = (row % grp) >= level
        e = jnp.exp2(-jnp.abs(b - pivot))
        qt = jnp.where(upper, q * e, 0.0).astype(BF16)
        kt = jnp.where(upper, 0.0, k * e).astype(BF16)
        levels.append((grp, qt, kt))
        level //= 2

    nd = t_blk // GLA_DIAG
    b3 = b.reshape(nd, GLA_DIAG, hk)
    q3 = q.reshape(nd, GLA_DIAG, hk)
    k3 = k.reshape(nd, GLA_DIAG, hk)
    sel_shape = (GLA_DIAG * GLA_DK, t_blk)
    sel = jnp.where(lax.broadcasted_iota(jnp.int32, sel_shape, 0) // GLA_DK
                    == lax.broadcasted_iota(jnp.int32, sel_shape, 1) % GLA_DIAG, 1.0, 0.0).astype(BF16)
    diag_mask = ((ri // GLA_DIAG) == (ci // GLA_DIAG)) & (ci <= ri)

    eye = (lax.broadcasted_iota(jnp.int32, (GLA_DK, GLA_DK), 0)
           == lax.broadcasted_iota(jnp.int32, (GLA_DK, GLA_DK), 1))

    diag_terms = []
    for j in range(GLA_DIAG):
        d = jnp.minimum(b3 - b3[:, j:j + 1, :], 0.0)
        pj = (jnp.exp2(d) * q3 * k3[:, j:j + 1, :]).reshape(t_blk, hk)
        diag_terms.append(pj)

    ng = ng_ref[...]
    for h in range(GLA_HEADS):
        ks = slice(h * GLA_DK, (h + 1) * GLA_DK)
        vs = slice(h * GLA_DV, (h + 1) * GLA_DV)
        a = jnp.zeros((t_blk, t_blk), F32)
        for grp, qt, kt in levels:
            s_l = lax.dot_general(qt[:, ks], kt[:, ks], (((1,), (1,)), ((), ())),
                                  preferred_element_type=F32)
            if grp == t_blk:
                a = a + s_l
            else:
                a = a + jnp.where((ri // grp) == (ci // grp), s_l, 0.0)
        stacked = jnp.concatenate([diag_terms[j][:, ks] for j in range(GLA_DIAG)], axis=1)
        a = jnp.where(diag_mask, jnp.dot(stacked.astype(BF16), sel, preferred_element_type=F32), a)

        v_h = v_ref[0, :, vs]
        state = state_ref[h]
        o = (jnp.dot(q_in[:, ks], state.astype(BF16), preferred_element_type=F32)
             + jnp.dot(a.astype(BF16), v_h, preferred_element_type=F32))

        e_col = jnp.sum(jnp.where(eye, jnp.broadcast_to(e_last[:, ks], (GLA_DK, GLA_DK)), 0.0),
                        axis=1, keepdims=True)
        upd = lax.dot_general(k_out[:, ks], v_h, (((0,), (0,)), ((), ())),
                              preferred_element_type=F32)
        state_ref[h] = e_col * state + upd

        r_h = r_ref[0, :, vs].astype(F32)
        gate = r_h / (1.0 + jnp.exp(-r_h))
        o_ref[0, :, vs] = (_rms(o, ng) * gate).astype(o_ref.dtype)


def _gla(proj, g, norm_gain):
    b, s, _ = proj.shape
    t = GLA_BLOCK
    hk = GLA_HEADS * GLA_DK
    hv = GLA_HEADS * GLA_DV
    return pl.pallas_call(
        _gla_kernel,
        out_shape=jax.ShapeDtypeStruct((b, s, hv), BF16),
        grid=(b, s // t),
        in_specs=[
            pl.BlockSpec((1, t, hk), lambda bi, ti: (bi, ti, 0)),
            pl.BlockSpec((1, t, hk), lambda bi, ti: (bi, ti, 1)),
            pl.BlockSpec((1, t, hv), lambda bi, ti: (bi, ti, 1)),
            pl.BlockSpec((1, t, hv), lambda bi, ti: (bi, ti, 2)),
            pl.BlockSpec((1, t, hk), lambda bi, ti: (bi, ti, 0)),
            pl.BlockSpec((1, GLA_DV), lambda bi, ti: (0, 0)),
        ],
        out_specs=pl.BlockSpec((1, t, hv), lambda bi, ti: (bi, ti, 0)),
        scratch_shapes=[pltpu.VMEM((GLA_HEADS, GLA_DK, GLA_DV), F32)],
        compiler_params=_cparams(("parallel", "arbitrary")),
        name="gla",
    )(proj, proj, proj, proj, g, norm_gain)


def _ffn_kernel(x_ref, halo_ref, g_ref, wup_ref, cw_ref, cb_ref, wd_ref, fg_ref, o_ref, xn_ref, slab_ref,
                *, tiles_per_seq, final_norm, tf):
    i = pl.program_id(0)
    tm, d = x_ref.shape
    f = wd_ref.shape[0]
    rows = tm // FFN_SUBTILES
    ext = rows + FFN_HALO
    seg = ext // 8
    nslab = d // LANES
    subs = range(FFN_SUBTILES)

    gain = g_ref[...]
    for s in subs:
        if s == 0:
            halo = jnp.where(i % tiles_per_seq == 0, 0.0, _rms(halo_ref[...], gain))
        else:
            halo = _rms(x_ref[s * rows - FFN_HALO:s * rows, :], gain)
        xn = _rms(x_ref[s * rows:(s + 1) * rows, :], gain)
        for c in range(nslab):
            slab_ref[s, c, 0:FFN_HALO, :] = halo[:, c * LANES:(c + 1) * LANES]
            slab_ref[s, c, FFN_HALO:ext, :] = xn[:, c * LANES:(c + 1) * LANES]
        for a2 in range(seg // 2):
            grp = [jnp.concatenate([slab_ref[s, c, pl.ds(2 * a2 + r, 8, stride=seg), :] for c in range(nslab)],
                                   axis=1) for r in range(2)]
            xn_ref[s, 16 * a2:16 * a2 + 16, :] = jnp.concatenate(grp, axis=0).astype(BF16)

    def conv(u, cols):
        u3 = u.reshape(seg, 8, u.shape[-1])
        wrap1 = pltpu.roll(u3[seg - 1], 1, 0)[None]
        wrap2 = pltpu.roll(u3[seg - 2], 1, 0)[None]
        prev1 = jnp.concatenate([wrap1, u3[:seg - 1]], axis=0)
        prev2 = jnp.concatenate([wrap2, wrap1, u3[:seg - 2]], axis=0)
        return (cw_ref[2:3, cols] * u3 + cw_ref[1:2, cols] * prev1 + cw_ref[0:1, cols] * prev2
                + cb_ref[:, cols])

    accs = [None for _ in subs]
    for c in range(f // tf):
        gcols = slice(c * tf, (c + 1) * tf)
        ucols = slice(f + c * tf, f + (c + 1) * tf)
        w_gate = wup_ref[:, gcols].astype(BF16)
        w_up = wup_ref[:, ucols].astype(BF16)
        w_down = wd_ref[c * tf:(c + 1) * tf, :].astype(BF16)
        ups = [(jnp.dot(xn_ref[s], w_gate, preferred_element_type=F32),
                jnp.dot(xn_ref[s], w_up, preferred_element_type=F32)) for s in subs]
        acts = []
        for s in subs:
            gate = conv(ups[s][0], gcols)
            up = conv(ups[s][1], ucols)
            acts.append(((gate / (1.0 + jnp.exp(-gate))) * up).reshape(ext, tf).astype(BF16))
        for s in subs:
            part = jnp.dot(acts[s], w_down, preferred_element_type=F32)
            accs[s] = part if accs[s] is None else accs[s] + part

    for s in subs:
        for a in range(seg):
            for c in range(nslab):
                slab_ref[s, c, pl.ds(a, 8, stride=seg), :] = accs[s][8 * a:8 * a + 8, c * LANES:(c + 1) * LANES]
        y = (x_ref[s * rows:(s + 1) * rows, :]
             + jnp.concatenate([slab_ref[s, c, FFN_HALO:ext, :] for c in range(nslab)], axis=1))
        if final_norm:
            y = _rms(y, fg_ref[...])
        o_ref[s * rows:(s + 1) * rows, :] = y


def _ffn(x, gain, w_up, conv_w, conv_b, w_down, layer, final_gain, *, tm, tf, seq, final_norm):
    m, d = x.shape
    f = w_down.shape[1]
    tiles_per_seq = seq // tm
    halo_blocks = tm // FFN_HALO
    ext = tm // FFN_SUBTILES + FFN_HALO
    kern = functools.partial(_ffn_kernel, tiles_per_seq=tiles_per_seq, final_norm=final_norm, tf=tf)
    return pl.pallas_call(
        kern,
        out_shape=jax.ShapeDtypeStruct((m, d), F32),
        grid=(m // tm,),
        in_specs=[
            pl.BlockSpec((tm, d), lambda i: (i, 0)),
            pl.BlockSpec((FFN_HALO, d), lambda i: (jnp.maximum(i * halo_blocks - 1, 0), 0)),
            pl.BlockSpec((None, 1, d), lambda i: (layer, 0, 0)),
            pl.BlockSpec((None, d, 2 * f), lambda i: (layer, 0, 0)),
            pl.BlockSpec((None, CONV_WIDTH, 2 * f), lambda i: (layer, 0, 0)),
            pl.BlockSpec((None, 1, 2 * f), lambda i: (layer, 0, 0)),
            pl.BlockSpec((None, f, d), lambda i: (layer, 0, 0)),
            pl.BlockSpec((1, d), lambda i: (0, 0)),
        ],
        out_specs=pl.BlockSpec((tm, d), lambda i: (i, 0)),
        scratch_shapes=[pltpu.VMEM((FFN_SUBTILES, ext, d), BF16),
                        pltpu.VMEM((FFN_SUBTILES, d // LANES, ext, LANES), F32)],
        compiler_params=_cparams(("parallel",)),
        name="ffn",
    )(x, x, gain, w_up, conv_w, conv_b, w_down, final_gain)


def _rope_tables(seq, q_scale):
    half = DIFF_HEAD_DIM // 2
    inv = 1.0 / (ROPE_THETA ** (jnp.arange(0, DIFF_HEAD_DIM, 2, dtype=F32) / DIFF_HEAD_DIM))
    ang = jnp.arange(seq, dtype=F32)[:, None] * inv[None, :]
    cos, sin = jnp.cos(ang), jnp.sin(ang)
    cos_t = jnp.tile(cos, (1, LANES // half))
    sin_t = jnp.concatenate([-sin, -sin, sin, sin], axis=1)
    cos3 = jnp.stack([cos_t * q_scale, cos_t, jnp.ones_like(cos_t)])
    sin3 = jnp.stack([sin_t * q_scale, sin_t, jnp.zeros_like(sin_t)])
    return cos3, sin3


def _reorder_qk_columns(w_qkv):
    layers, d, _ = w_qkv.shape
    width = DIFF_HEADS * DIFF_V_DIM
    half = DIFF_HEAD_DIM // 2
    qk = w_qkv[:, :, :2 * width].reshape(layers, d, 2, DIFF_HEADS, 2, 2, half)
    qk = qk.transpose(0, 1, 2, 3, 5, 4, 6).reshape(layers, d, 2 * width)
    return jnp.concatenate([qk, w_qkv[:, :, 2 * width:]], axis=2).astype(BF16)


def _trunk(x, norm_mix, norm_ffn, norm_final, diff_w_qkv, diff_w_o, diff_lambda, diff_subln,
           gla_w_in, gla_w_a1, gla_w_a2, gla_b_a, gla_norm, gla_w_o,
           ffn_w_up, ffn_conv_w, ffn_conv_b, ffn_w_down, *, tm, tp, tq, tf):
    bsz, seq, d = x.shape
    depth = norm_mix.shape[0]
    m = bsz * seq
    tables = _rope_tables(seq, DIFF_HEAD_DIM ** -0.5 * math.log2(math.e))
    w_qkv = _reorder_qk_columns(diff_w_qkv)
    w_up = ffn_w_up.astype(BF16)
    w_down = ffn_w_down.astype(BF16)
    h = x.reshape(m, d)
    rank = gla_w_a1.shape[-1]
    for layer in range(depth):
        jdx = layer // 2
        gain = norm_mix[layer].reshape(1, d)
        if layer % 2 == 0:
            lambda_init = 0.8 - 0.6 * math.exp(-0.3 * layer)
            qkv = _norm_proj(h, gain, w_qkv, jdx, tables, tm=tp, seq=seq)
            o = _diff_attention(qkv.reshape(bsz, seq, 3 * d), diff_lambda[jdx],
                                diff_subln[jdx].reshape(DIFF_V_DIM, 1),
                                lambda_init=lambda_init, tq=tq)
            h = _proj_res(o.reshape(m, d), diff_w_o, jdx, h, tm=tm)
        else:
            proj = _norm_proj(h, gain, gla_w_in, jdx, None, tm=tp, seq=seq)
            w1 = jnp.pad(gla_w_a1[jdx], ((0, 0), (0, LANES - rank))).astype(BF16)
            w2 = jnp.pad(gla_w_a2[jdx], ((0, LANES - rank), (0, 0)))
            g = _gla_gate(h, gain, w1, w2, gla_b_a[jdx].reshape(1, -1), tm=tm)
            o = _gla(proj.reshape(bsz, seq, 3 * d), g.reshape(bsz, seq, -1),
                     gla_norm[jdx].reshape(1, GLA_DV))
            h = _proj_res(o.reshape(m, d), gla_w_o, jdx, h, tm=tm)
        h = _ffn(h, norm_ffn.reshape(depth, 1, d), w_up, ffn_conv_w, ffn_conv_b.reshape(depth, 1, -1),
                 w_down, layer, norm_final.reshape(1, d), tm=tp, tf=tf, seq=seq,
                 final_norm=(layer == depth - 1))
    return h.reshape(bsz, seq, d)


def kernel(x, norm_mix, norm_ffn, norm_final, diff_w_qkv, diff_w_o, diff_lambda, diff_subln, gla_w_in, gla_w_a1, gla_w_a2, gla_b_a, gla_norm, gla_w_o, ffn_w_up, ffn_conv_w, ffn_conv_b, ffn_w_down):
    return _trunk(x, norm_mix, norm_ffn, norm_final, diff_w_qkv, diff_w_o, diff_lambda, diff_subln,
                  gla_w_in, gla_w_a1, gla_w_a2, gla_b_a, gla_norm, gla_w_o,
                  ffn_w_up, ffn_conv_w, ffn_conv_b, ffn_w_down, tm=1024, tp=512, tq=512, tf=256)
```

```python
import functools
import math

import jax
import jax.numpy as jnp
from jax import lax
from jax.experimental import pallas as pl
from jax.experimental.pallas import tpu as pltpu

F32 = jnp.float32
BF16 = jnp.bfloat16

NORM_EPS = 1e-6
ROPE_THETA = 10000.0

DIFF_HEADS = 8
DIFF_HEAD_DIM = 64
DIFF_V_DIM = 2 * DIFF_HEAD_DIM

GLA_HEADS = 4
GLA_DK = 128
GLA_DV = 256
GLA_GATE_TEMP = 16.0
GLA_BLOCK = 256
GLA_DIAG = 8

CONV_WIDTH = 3
FFN_HALO = 16
FFN_SUBTILES = 2

LANES = 128
VMEM_LIMIT_BYTES = 56 * 1024 * 1024


def _cparams(semantics):
    return pltpu.CompilerParams(dimension_semantics=semantics, vmem_limit_bytes=VMEM_LIMIT_BYTES)


def _rms(x, gain):
    ms = jnp.mean(x * x, axis=-1, keepdims=True)
    return x * lax.rsqrt(ms + NORM_EPS) * gain


NORM_PROJ_COLS = 1024


def _norm_proj_kernel(x_ref, g_ref, w_ref, *rest, rope):
    if rope:
        cos_ref, sin_ref, o_ref = rest
    else:
        (o_ref,) = rest
    xn = _rms(x_ref[...], g_ref[...]).astype(BF16)
    for cb in range(o_ref.shape[1] // NORM_PROJ_COLS):
        cols = slice(cb * NORM_PROJ_COLS, (cb + 1) * NORM_PROJ_COLS)
        acc = jnp.dot(xn, w_ref[:, cols].astype(BF16), preferred_element_type=F32)
        if not rope:
            o_ref[:, cols] = acc.astype(o_ref.dtype)
            continue
        c = cos_ref[cb]
        s = sin_ref[cb]
        for hh in range(NORM_PROJ_COLS // LANES):
            t = acc[:, hh * LANES:(hh + 1) * LANES]
            lo = cb * NORM_PROJ_COLS + hh * LANES
            o_ref[:, lo:lo + LANES] = (t * c + pltpu.roll(t, LANES // 2, 1) * s).astype(o_ref.dtype)


def _norm_proj(x, gain, w, layer, tables, *, tm, seq):
    m, d = x.shape
    n = w.shape[2]
    nseq = seq // tm
    rope = tables is not None
    in_specs = [
        pl.BlockSpec((tm, d), lambda i: (i, 0)),
        pl.BlockSpec((1, d), lambda i: (0, 0)),
        pl.BlockSpec((None, d, n), lambda i: (layer, 0, 0)),
    ]
    if rope:
        nb = tables[0].shape[0]
        assert nb == n // NORM_PROJ_COLS
        in_specs += [pl.BlockSpec((nb, tm, LANES), lambda i: (0, i % nseq, 0))] * 2
    return pl.pallas_call(
        functools.partial(_norm_proj_kernel, rope=rope),
        out_shape=jax.ShapeDtypeStruct((m, n), BF16),
        grid=(m // tm,),
        in_specs=in_specs,
        out_specs=pl.BlockSpec((tm, n), lambda i: (i, 0)),
        compiler_params=_cparams(("parallel",)),
        name="norm_proj",
    )(x, gain, w, *(tables if rope else ()))


ATTN_CHUNK = 256
ATTN_HEADS_PER_STEP = 4
ATTN_V_ROWS = DIFF_V_DIM + 16


def _diff_attn_kernel(lam_ref, sg_ref, q_ref, k_ref, v_ref, o_ref, vt_ref, s_ref, acc_ref, *, tq, lambda_init):
    qi = pl.program_id(2)
    n_kv_blocks = vt_ref.shape[1]
    n2 = 2 * tq
    ch = min(ATTN_CHUNK, tq)
    nch = tq // ch
    heads = range(ATTN_HEADS_PER_STEP)
    lanes = [slice(hh * LANES, (hh + 1) * LANES) for hh in heads]

    @pl.when(qi == 0)
    def _():
        for hh in heads:
            for c in range(n_kv_blocks):
                vt_ref[hh, c, 0:DIFF_V_DIM, :] = v_ref[0, c * tq:(c + 1) * tq, lanes[hh]].astype(F32).T.astype(BF16)
                ones_row = lax.broadcasted_iota(jnp.int32, (ATTN_V_ROWS - DIFF_V_DIM, tq), 0) == 0
                vt_ref[hh, c, DIFF_V_DIM:ATTN_V_ROWS, :] = jnp.where(ones_row, 1.0, 0.0).astype(BF16)

    qqt = []
    for hh in heads:
        qt = q_ref[0, :, lanes[hh]].astype(F32).T
        feat = lax.broadcasted_iota(jnp.int32, qt.shape, 0)
        map1 = (feat % DIFF_HEAD_DIM) < (DIFF_HEAD_DIM // 2)
        qqt.append(jnp.concatenate([jnp.where(map1, qt, 0.0), jnp.where(map1, 0.0, qt)], axis=1).astype(BF16))
    acc_ref[...] = jnp.zeros(acc_ref.shape, F32)

    def scores(hh, blk, slot, masked):
        cmx8 = None
        for c in range(nch):
            start = pl.multiple_of(blk * tq + c * ch, ch)
            s = jnp.dot(k_ref[0, pl.ds(start, ch), lanes[hh]], qqt[hh], preferred_element_type=F32)
            if masked:
                kv_pos = lax.broadcasted_iota(jnp.int32, s.shape, 0) + c * ch
                col = lax.broadcasted_iota(jnp.int32, s.shape, 1)
                q_pos = jnp.where(col >= tq, col - tq, col)
                s = jnp.where(kv_pos <= q_pos, s, -jnp.inf)
            s_ref[hh, slot, c * ch:(c + 1) * ch, :] = s
            part = jnp.max(s.reshape(ch // 8, 8, n2), axis=0)
            cmx8 = part if cmx8 is None else jnp.maximum(cmx8, part)
        return cmx8

    def softmax_pv(hh, blk, slot, m, cmx8):
        m_new = jnp.maximum(m, jnp.max(cmx8, axis=0, keepdims=True))
        alpha = jnp.exp2(m - m_new)
        pv = None
        for c in range(nch):
            p = jnp.exp2((s_ref[hh, slot, c * ch:(c + 1) * ch, :] - m_new).astype(BF16))
            d = jnp.dot(vt_ref[hh, blk, :, c * ch:(c + 1) * ch], p, preferred_element_type=F32)
            pv = d if pv is None else pv + d
        acc_ref[hh] = alpha * acc_ref[hh] + pv
        return (m_new,)

    cmx8_0 = [scores(hh, qi, 0, True) for hh in heads]

    def body(t, carry):
        slot = lax.rem(t, 2)
        stats = [softmax_pv(hh, jnp.where(t == 0, qi, t - 1), slot, *carry[hh]) for hh in heads]
        return tuple(stats[hh] + (scores(hh, t, 1 - slot, False),) for hh in heads)

    init = tuple((jnp.full((1, n2), -jnp.inf, F32), cmx8_0[hh]) for hh in heads)
    carry = lax.fori_loop(0, qi, body, init)

    lv = lam_ref[...]
    lam = (jnp.exp(jnp.sum(lv[0:1] * lv[1:2], axis=1, keepdims=True))
           - jnp.exp(jnp.sum(lv[2:3] * lv[3:4], axis=1, keepdims=True)) + lambda_init)
    for hh in heads:
        softmax_pv(hh, jnp.maximum(qi - 1, 0), lax.rem(qi, 2), *carry[hh])
    for hh in heads:
        l = acc_ref[hh, DIFF_V_DIM:DIFF_V_DIM + 1, :]
        o1 = acc_ref[hh, 0:DIFF_V_DIM, 0:tq] / l[:, 0:tq]
        o2 = acc_ref[hh, 0:DIFF_V_DIM, tq:n2] / l[:, tq:n2]
        o = o1 - lam * o2
        ms = jnp.mean(o * o, axis=0, keepdims=True)
        o = o * lax.rsqrt(ms + NORM_EPS) * sg_ref[...] * (1.0 - lambda_init)
        o_ref[0, :, lanes[hh]] = o.T.astype(o_ref.dtype)


def _diff_attention(qkv, lam_vecs, subln_gain, *, lambda_init, tq):
    b, s, _ = qkv.shape
    hps = ATTN_HEADS_PER_STEP
    groups = DIFF_HEADS // hps
    width = hps * LANES
    kern = functools.partial(_diff_attn_kernel, tq=tq, lambda_init=lambda_init)
    return pl.pallas_call(
        kern,
        out_shape=jax.ShapeDtypeStruct((b, s, DIFF_HEADS * DIFF_V_DIM), BF16),
        grid=(b, groups, s // tq),
        in_specs=[
            pl.BlockSpec((4, DIFF_HEAD_DIM), lambda bi, gi, qi: (0, 0)),
            pl.BlockSpec((DIFF_V_DIM, 1), lambda bi, gi, qi: (0, 0)),
            pl.BlockSpec((1, tq, width), lambda bi, gi, qi: (bi, qi, gi)),
            pl.BlockSpec((1, s, width), lambda bi, gi, qi: (bi, 0, groups + gi)),
            pl.BlockSpec((1, s, width), lambda bi, gi, qi: (bi, 0, 2 * groups + gi)),
        ],
        out_specs=pl.BlockSpec((1, tq, width), lambda bi, gi, qi: (bi, qi, gi)),
        scratch_shapes=[
            pltpu.VMEM((hps, s // tq, ATTN_V_ROWS, tq), BF16),
            pltpu.VMEM((hps, 2, tq, 2 * tq), F32),
            pltpu.VMEM((hps, ATTN_V_ROWS, 2 * tq), F32),
        ],
        compiler_params=_cparams(("parallel", "parallel", "arbitrary")),
        name="diff_attn",
    )(lam_vecs, subln_gain, qkv, qkv, qkv)


def _proj_res_kernel(a_ref, w_ref, res_ref, o_ref):
    o_ref[...] = res_ref[...] + jnp.dot(a_ref[...], w_ref[...].astype(BF16), preferred_element_type=F32)


def _proj_res(a, w, layer, res, *, tm):
    m, k = a.shape
    n = w.shape[2]
    return pl.pallas_call(
        _proj_res_kernel,
        out_shape=jax.ShapeDtypeStruct((m, n), F32),
        grid=(m // tm,),
        in_specs=[
            pl.BlockSpec((tm, k), lambda i: (i, 0)),
            pl.BlockSpec((None, k, n), lambda i: (layer, 0, 0)),
            pl.BlockSpec((tm, n), lambda i: (i, 0)),
        ],
        out_specs=pl.BlockSpec((tm, n), lambda i: (i, 0)),
        compiler_params=_cparams(("parallel",)),
        name="proj_res",
    )(a, w, res)


def _gla_gate_kernel(x_ref, g_ref, w1_ref, w2_ref, b_ref, o_ref):
    xn = _rms(x_ref[...], g_ref[...]).astype(BF16)
    t = jnp.dot(xn, w1_ref[...], preferred_element_type=F32)
    z = jnp.dot(t, w2_ref[...], preferred_element_type=F32, precision=lax.Precision.HIGHEST) + b_ref[...]
    log_sig = jnp.minimum(z, 0.0) - jnp.log1p(jnp.exp(-jnp.abs(z)))
    o_ref[...] = log_sig * (1.0 / GLA_GATE_TEMP)


def _gla_gate(x, gain, w1, w2, bias, *, tm):
    m, d = x.shape
    r = w1.shape[1]
    n = w2.shape[1]
    return pl.pallas_call(
        _gla_gate_kernel,
        out_shape=jax.ShapeDtypeStruct((m, n), F32),
        grid=(m // tm,),
        in_specs=[
            pl.BlockSpec((tm, d), lambda i: (i, 0)),
            pl.BlockSpec((1, d), lambda i: (0, 0)),
            pl.BlockSpec((d, r), lambda i: (0, 0)),
            pl.BlockSpec((r, n), lambda i: (0, 0)),
            pl.BlockSpec((1, n), lambda i: (0, 0)),
        ],
        out_specs=pl.BlockSpec((tm, n), lambda i: (i, 0)),
        compiler_params=_cparams(("parallel",)),
        name="gla_gate",
    )(x, gain, w1, w2, bias)


def _split3(x):
    hi = x.astype(BF16)
    r1 = x - hi.astype(F32)
    mid = r1.astype(BF16)
    lo = (r1 - mid.astype(F32)).astype(BF16)
    return hi, mid, lo


def _gla_kernel(q_ref, k_ref, v_ref, r_ref, g_ref, ng_ref, o_ref, state_ref):
    t_blk = q_ref.shape[1]
    hk = GLA_HEADS * GLA_DK

    @pl.when(pl.program_id(1) == 0)
    def _():
        state_ref[...] = jnp.zeros(state_ref.shape, F32)

    ri = lax.broadcasted_iota(jnp.int32, (t_blk, t_blk), 0)
    ci = lax.broadcasted_iota(jnp.int32, (t_blk, t_blk), 1)

    tri = jnp.where(ci <= ri, 1.0, 0.0).astype(BF16)
    g_hi, g_mid, g_lo = _split3(g_ref[0])
    b = (jnp.dot(tri, g_hi, preferred_element_type=F32)
         + jnp.dot(tri, g_mid, preferred_element_type=F32)
         + jnp.dot(tri, g_lo, preferred_element_type=F32))
    b = b * math.log2(math.e)

    q = q_ref[0].astype(F32) * (GLA_DK ** -0.5)
    k = k_ref[0].astype(F32)

    row = lax.broadcasted_iota(jnp.int32, (t_blk, hk), 0)
    b_last = b[t_blk - 1:t_blk, :]
    q_in = (q * jnp.exp2(b)).astype(BF16)
    k_out = (k * jnp.exp2(b_last - b)).astype(BF16)
    e_last = jnp.exp2(b_last)

    level = t_blk // 2
    levels = []
    while level >= GLA_DIAG:
        grp = 2 * level
        b3 = b.reshape(t_blk // grp, grp, hk)
        pivot = jnp.broadcast_to(b3[:, level - 1:level, :], b3.shape).reshape(t_blk, hk)
        upper = (row % grp) >= level
        e = jnp.exp2(-jnp.abs(b - pivot))
        qt = jnp.where(upper, q * e, 0.0).astype(BF16)
        kt = jnp.where(upper, 0.0, k * e).astype(BF16)
        levels.append((grp, qt, kt))
        level //= 2

    nd = t_blk // GLA_DIAG
    b3 = b.reshape(nd, GLA_DIAG, hk)
    q3 = q.reshape(nd, GLA_DIAG, hk)
    k3 = k.reshape(nd, GLA_DIAG, hk)
    sel_shape = (GLA_DIAG * GLA_DK, t_blk)
    sel = jnp.where(lax.broadcasted_iota(jnp.int32, sel_shape, 0) // GLA_DK
                    == lax.broadcasted_iota(jnp.int32, sel_shape, 1) % GLA_DIAG, 1.0, 0.0).astype(BF16)
    diag_mask = ((ri // GLA_DIAG) == (ci // GLA_DIAG)) & (ci <= ri)

    eye = (lax.broadcasted_iota(jnp.int32, (GLA_DK, GLA_DK), 0)
           == lax.broadcasted_iota(jnp.int32, (GLA_DK, GLA_DK), 1))

    diag_terms = []
    for j in range(GLA_DIAG):
        d = jnp.minimum(b3 - b3[:, j:j + 1, :], 0.0)
        pj = (jnp.exp2(d) * q3 * k3[:, j:j + 1, :]).reshape(t_blk, hk)
        diag_terms.append(pj)

    ng = ng_ref[...]
    for h in range(GLA_HEADS):
        ks = slice(h * GLA_DK, (h + 1) * GLA_DK)
        vs = slice(h * GLA_DV, (h + 1) * GLA_DV)
        a = jnp.zeros((t_blk, t_blk), F32)
        for grp, qt, kt in levels:
            s_l = lax.dot_general(qt[:, ks], kt[:, ks], (((1,), (1,)), ((), ())),
                                  preferred_element_type=F32)
            if grp == t_blk:
                a = a + s_l
            else:
                a = a + jnp.where((ri // grp) == (ci // grp), s_l, 0.0)
        stacked = jnp.concatenate([diag_terms[j][:, ks] for j in range(GLA_DIAG)], axis=1)
        a = jnp.where(diag_mask, jnp.dot(stacked.astype(BF16), sel, preferred_element_type=F32), a)

        v_h = v_ref[0, :, vs]
        state = state_ref[h]
        o = (jnp.dot(q_in[:, ks], state.astype(BF16), preferred_element_type=F32)
             + jnp.dot(a.astype(BF16), v_h, preferred_element_type=F32))

        e_col = jnp.sum(jnp.where(eye, jnp.broadcast_to(e_last[:, ks], (GLA_DK, GLA_DK)), 0.0),
                        axis=1, keepdims=True)
        upd = lax.dot_general(k_out[:, ks], v_h, (((0,), (0,)), ((), ())),
                              preferred_element_type=F32)
        state_ref[h] = e_col * state + upd

        r_h = r_ref[0, :, vs].astype(F32)
        gate = r_h / (1.0 + jnp.exp(-r_h))
        o_ref[0, :, vs] = (_rms(o, ng) * gate).astype(o_ref.dtype)


def _gla(proj, g, norm_gain):
    b, s, _ = proj.shape
    t = GLA_BLOCK
    hk = GLA_HEADS * GLA_DK
    hv = GLA_HEADS * GLA_DV
    return pl.pallas_call(
        _gla_kernel,
        out_shape=jax.ShapeDtypeStruct((b, s, hv), BF16),
        grid=(b, s // t),
        in_specs=[
            pl.BlockSpec((1, t, hk), lambda bi, ti: (bi, ti, 0)),
            pl.BlockSpec((1, t, hk), lambda bi, ti: (bi, ti, 1)),
            pl.BlockSpec((1, t, hv), lambda bi, ti: (bi, ti, 1)),
            pl.BlockSpec((1, t, hv), lambda bi, ti: (bi, ti, 2)),
            pl.BlockSpec((1, t, hk), lambda bi, ti: (bi, ti, 0)),
            pl.BlockSpec((1, GLA_DV), lambda bi, ti: (0, 0)),
        ],
        out_specs=pl.BlockSpec((1, t, hv), lambda bi, ti: (bi, ti, 0)),
        scratch_shapes=[pltpu.VMEM((GLA_HEADS, GLA_DK, GLA_DV), F32)],
        compiler_params=_cparams(("parallel", "arbitrary")),
        name="gla",
    )(proj, proj, proj, proj, g, norm_gain)


def _ffn_kernel(x_ref, halo_ref, g_ref, wup_ref, cw_ref, cb_ref, wd_ref, fg_ref, o_ref, xn_ref, slab_ref,
                *, tiles_per_seq, final_norm, tf):
    i = pl.program_id(0)
    tm, d = x_ref.shape
    f = wd_ref.shape[0]
    rows = tm // FFN_SUBTILES
    ext = rows + FFN_HALO
    seg = ext // 8
    nslab = d // LANES
    subs = range(FFN_SUBTILES)

    gain = g_ref[...]
    for s in subs:
        if s == 0:
            halo = jnp.where(i % tiles_per_seq == 0, 0.0, _rms(halo_ref[...], gain))
        else:
            halo = _rms(x_ref[s * rows - FFN_HALO:s * rows, :], gain)
        xn = _rms(x_ref[s * rows:(s + 1) * rows, :], gain)
        for c in range(nslab):
            slab_ref[s, c, 0:FFN_HALO, :] = halo[:, c * LANES:(c + 1) * LANES]
            slab_ref[s, c, FFN_HALO:ext, :] = xn[:, c * LANES:(c + 1) * LANES]
        for a2 in range(seg // 2):
            grp = [jnp.concatenate([slab_ref[s, c, pl.ds(2 * a2 + r, 8, stride=seg), :] for c in range(nslab)],
                                   axis=1) for r in range(2)]
            xn_ref[s, 16 * a2:16 * a2 + 16, :] = jnp.concatenate(grp, axis=0).astype(BF16)

    def conv(u, cols):
        u3 = u.reshape(seg, 8, u.shape[-1])
        wrap1 = pltpu.roll(u3[seg - 1], 1, 0)[None]
        wrap2 = pltpu.roll(u3[seg - 2], 1, 0)[None]
        prev1 = jnp.concatenate([wrap1, u3[:seg - 1]], axis=0)
        prev2 = jnp.concatenate([wrap2, wrap1, u3[:seg - 2]], axis=0)
        return (cw_ref[2:3, cols] * u3 + cw_ref[1:2, cols] * prev1 + cw_ref[0:1, cols] * prev2
                + cb_ref[:, cols])

    accs = [None for _ in subs]
    for c in range(f // tf):
        gcols = slice(c * tf, (c + 1) * tf)
        ucols = slice(f + c * tf, f + (c + 1) * tf)
        w_gate = wup_ref[:, gcols].astype(BF16)
        w_up = wup_ref[:, ucols].astype(BF16)
        w_down = wd_ref[c * tf:(c + 1) * tf, :].astype(BF16)
        ups = [(jnp.dot(xn_ref[s], w_gate, preferred_element_type=F32),
                jnp.dot(xn_ref[s], w_up, preferred_element_type=F32)) for s in subs]
        acts = []
        for s in subs:
            gate = conv(ups[s][0], gcols)
            up = conv(ups[s][1], ucols)
            acts.append(((gate / (1.0 + jnp.exp(-gate))) * up).reshape(ext, tf).astype(BF16))
        for s in subs:
            part = jnp.dot(acts[s], w_down, preferred_element_type=F32)
            accs[s] = part if accs[s] is None else accs[s] + part

    for s in subs:
        for a in range(seg):
            for c in range(nslab):
                slab_ref[s, c, pl.ds(a, 8, stride=seg), :] = accs[s][8 * a:8 * a + 8, c * LANES:(c + 1) * LANES]
        y = (x_ref[s * rows:(s + 1) * rows, :]
             + jnp.concatenate([slab_ref[s, c, FFN_HALO:ext, :] for c in range(nslab)], axis=1))
        if final_norm:
            y = _rms(y, fg_ref[...])
        o_ref[s * rows:(s + 1) * rows, :] = y


def _ffn(x, gain, w_up, conv_w, conv_b, w_down, layer, final_gain, *, tm, tf, seq, final_norm):
    m, d = x.shape
    f = w_down.shape[1]
    tiles_per_seq = seq // tm
    halo_blocks = tm // FFN_HALO
    ext = tm // FFN_SUBTILES + FFN_HALO
    kern = functools.partial(_ffn_kernel, tiles_per_seq=tiles_per_seq, final_norm=final_norm, tf=tf)
    return pl.pallas_call(
        kern,
        out_shape=jax.ShapeDtypeStruct((m, d), F32),
        grid=(m // tm,),
        in_specs=[
            pl.BlockSpec((tm, d), lambda i: (i, 0)),
            pl.BlockSpec((FFN_HALO, d), lambda i: (jnp.maximum(i * halo_blocks - 1, 0), 0)),
            pl.BlockSpec((None, 1, d), lambda i: (layer, 0, 0)),
            pl.BlockSpec((None, d, 2 * f), lambda i: (layer, 0, 0), pipeline_mode=pl.Buffered(1)),
            pl.BlockSpec((None, CONV_WIDTH, 2 * f), lambda i: (layer, 0, 0)),
            pl.BlockSpec((None, 1, 2 * f), lambda i: (layer, 0, 0)),
            pl.BlockSpec((None, f, d), lambda i: (layer, 0, 0), pipeline_mode=pl.Buffered(1)),
            pl.BlockSpec((1, d), lambda i: (0, 0)),
        ],
        out_specs=pl.BlockSpec((tm, d), lambda i: (i, 0)),
        scratch_shapes=[pltpu.VMEM((FFN_SUBTILES, ext, d), BF16),
                        pltpu.VMEM((FFN_SUBTILES, d // LANES, ext, LANES), F32)],
        compiler_params=_cparams(("parallel",)),
        name="ffn",
    )(x, x, gain, w_up, conv_w, conv_b, w_down, final_gain)


def _rope_tables(seq, q_scale):
    half = DIFF_HEAD_DIM // 2
    inv = 1.0 / (ROPE_THETA ** (jnp.arange(0, DIFF_HEAD_DIM, 2, dtype=F32) / DIFF_HEAD_DIM))
    ang = jnp.arange(seq, dtype=F32)[:, None] * inv[None, :]
    cos, sin = jnp.cos(ang), jnp.sin(ang)
    cos_t = jnp.tile(cos, (1, LANES // half))
    sin_t = jnp.concatenate([-sin, -sin, sin, sin], axis=1)
    cos3 = jnp.stack([cos_t * q_scale, cos_t, jnp.ones_like(cos_t)])
    sin3 = jnp.stack([sin_t * q_scale, sin_t, jnp.zeros_like(sin_t)])
    return cos3, sin3


def _reorder_qk_columns(w_qkv):
    layers, d, _ = w_qkv.shape
    width = DIFF_HEADS * DIFF_V_DIM
    half = DIFF_HEAD_DIM // 2
    qk = w_qkv[:, :, :2 * width].reshape(layers, d, 2, DIFF_HEADS, 2, 2, half)
    qk = qk.transpose(0, 1, 2, 3, 5, 4, 6).reshape(layers, d, 2 * width)
    return jnp.concatenate([qk, w_qkv[:, :, 2 * width:]], axis=2).astype(BF16)


def _trunk(x, norm_mix, norm_ffn, norm_final, diff_w_qkv, diff_w_o, diff_lambda, diff_subln,
           gla_w_in, gla_w_a1, gla_w_a2, gla_b_a, gla_norm, gla_w_o,
           ffn_w_up, ffn_conv_w, ffn_conv_b, ffn_w_down, *, tm, tp, tq, tf):
    bsz, seq, d = x.shape
    depth = norm_mix.shape[0]
    m = bsz * seq
    tables = _rope_tables(seq, DIFF_HEAD_DIM ** -0.5 * math.log2(math.e))
    w_qkv = _reorder_qk_columns(diff_w_qkv)
    h = x.reshape(m, d)
    rank = gla_w_a1.shape[-1]
    for layer in range(depth):
        jdx = layer // 2
        gain = norm_mix[layer].reshape(1, d)
        if layer % 2 == 0:
            lambda_init = 0.8 - 0.6 * math.exp(-0.3 * layer)
            qkv = _norm_proj(h, gain, w_qkv, jdx, tables, tm=tp, seq=seq)
            o = _diff_attention(qkv.reshape(bsz, seq, 3 * d), diff_lambda[jdx],
                                diff_subln[jdx].reshape(DIFF_V_DIM, 1),
                                lambda_init=lambda_init, tq=tq)
            h = _proj_res(o.reshape(m, d), diff_w_o, jdx, h, tm=tm)
        else:
            proj = _norm_proj(h, gain, gla_w_in, jdx, None, tm=tp, seq=seq)
            w1 = jnp.pad(gla_w_a1[jdx], ((0, 0), (0, LANES - rank))).astype(BF16)
            w2 = jnp.pad(gla_w_a2[jdx], ((0, LANES - rank), (0, 0)))
            g = _gla_gate(h, gain, w1, w2, gla_b_a[jdx].reshape(1, -1), tm=tm)
            o = _gla(proj.reshape(bsz, seq, 3 * d), g.reshape(bsz, seq, -1),
                     gla_norm[jdx].reshape(1, GLA_DV))
            h = _proj_res(o.reshape(m, d), gla_w_o, jdx, h, tm=tm)
        h = _ffn(h, norm_ffn.reshape(depth, 1, d), ffn_w_up, ffn_conv_w, ffn_conv_b.reshape(depth, 1, -1),
                 ffn_w_down, layer, norm_final.reshape(1, d), tm=tp, tf=tf, seq=seq,
                 final_norm=(layer == depth - 1))
    return h.reshape(bsz, seq, d)


def kernel(x, norm_mix, norm_ffn, norm_final, diff_w_qkv, diff_w_o, diff_lambda, diff_subln, gla_w_in, gla_w_a1, gla_w_a2, gla_b_a, gla_norm, gla_w_o, ffn_w_up, ffn_conv_w, ffn_conv_b, ffn_w_down):
    return _trunk(x, norm_mix, norm_ffn, norm_final, diff_w_qkv, diff_w_o, diff_lambda, diff_subln,
                  gla_w_in, gla_w_a1, gla_w_a2, gla_b_a, gla_norm, gla_w_o,
                  ffn_w_up, ffn_conv_w, ffn_conv_b, ffn_w_down, tm=1024, tp=512, tq=512, tf=256)
```

```python
import functools
import math

import jax
import jax.numpy as jnp
from jax import lax
from jax.experimental import pallas as pl
from jax.experimental.pallas import tpu as pltpu

F32 = jnp.float32
BF16 = jnp.bfloat16

NORM_EPS = 1e-6
ROPE_THETA = 10000.0

DIFF_HEADS = 8
DIFF_HEAD_DIM = 64
DIFF_V_DIM = 2 * DIFF_HEAD_DIM

GLA_HEADS = 4
GLA_DK = 128
GLA_DV = 256
GLA_GATE_TEMP = 16.0
GLA_BLOCK = 256
GLA_DIAG = 8

CONV_WIDTH = 3
FFN_HALO = 16
FFN_SUBTILES = 2

LANES = 128
VMEM_LIMIT_BYTES = 56 * 1024 * 1024


def _cparams(semantics):
    return pltpu.CompilerParams(dimension_semantics=semantics, vmem_limit_bytes=VMEM_LIMIT_BYTES)


def _rms(x, gain):
    ms = jnp.mean(x * x, axis=-1, keepdims=True)
    return x * lax.rsqrt(ms + NORM_EPS) * gain


NORM_PROJ_COLS = 1024


def _norm_proj_kernel(x_ref, g_ref, w_ref, *rest, rope, gate):
    if rope:
        cos_ref, sin_ref, o_ref = rest
    elif gate:
        w1_ref, w2_ref, b_ref, o_ref, gate_ref = rest
    else:
        (o_ref,) = rest
    xn = _rms(x_ref[...], g_ref[...]).astype(BF16)
    if gate:
        t = jnp.dot(xn, w1_ref[...], preferred_element_type=F32)
        w2 = w2_ref[...]
        t_hi, w_hi = t.astype(BF16), w2.astype(BF16)
        t_lo, w_lo = (t - t_hi.astype(F32)).astype(BF16), (w2 - w_hi.astype(F32)).astype(BF16)
        z = (jnp.dot(t_hi, w_hi, preferred_element_type=F32) + jnp.dot(t_hi, w_lo, preferred_element_type=F32)
             + jnp.dot(t_lo, w_hi, preferred_element_type=F32) + b_ref[...])
        gate_ref[...] = (jnp.minimum(z, 0.0) - jnp.log1p(jnp.exp(-jnp.abs(z)))) * (1.0 / GLA_GATE_TEMP)
    for cb in range(o_ref.shape[1] // NORM_PROJ_COLS):
        cols = slice(cb * NORM_PROJ_COLS, (cb + 1) * NORM_PROJ_COLS)
        acc = jnp.dot(xn, w_ref[:, cols].astype(BF16), preferred_element_type=F32)
        if not rope:
            o_ref[:, cols] = acc.astype(o_ref.dtype)
            continue
        c = cos_ref[cb]
        s = sin_ref[cb]
        for hh in range(NORM_PROJ_COLS // LANES):
            t = acc[:, hh * LANES:(hh + 1) * LANES]
            lo = cb * NORM_PROJ_COLS + hh * LANES
            o_ref[:, lo:lo + LANES] = (t * c + pltpu.roll(t, LANES // 2, 1) * s).astype(o_ref.dtype)


def _norm_proj(x, gain, w, layer, tables=None, gate_params=None, *, tm, seq):
    m, d = x.shape
    n = w.shape[2]
    nseq = seq // tm
    rope = tables is not None
    gate = gate_params is not None
    in_specs = [
        pl.BlockSpec((tm, d), lambda i: (i, 0)),
        pl.BlockSpec((1, d), lambda i: (0, 0)),
        pl.BlockSpec((None, d, n), lambda i: (layer, 0, 0)),
    ]
    out_shape = jax.ShapeDtypeStruct((m, n), BF16)
    out_specs = pl.BlockSpec((tm, n), lambda i: (i, 0))
    extra = ()
    if rope:
        nb = tables[0].shape[0]
        assert nb == n // NORM_PROJ_COLS
        in_specs += [pl.BlockSpec((nb, tm, LANES), lambda i: (0, i % nseq, 0))] * 2
        extra = tuple(tables)
    elif gate:
        w1, w2, bias = gate_params
        in_specs += [pl.BlockSpec(w1.shape, lambda i: (0, 0)), pl.BlockSpec(w2.shape, lambda i: (0, 0)),
                     pl.BlockSpec(bias.shape, lambda i: (0, 0))]
        extra = (w1, w2, bias)
        out_shape = (out_shape, jax.ShapeDtypeStruct((m, w2.shape[1]), F32))
        out_specs = (out_specs, pl.BlockSpec((tm, w2.shape[1]), lambda i: (i, 0)))
    return pl.pallas_call(
        functools.partial(_norm_proj_kernel, rope=rope, gate=gate),
        out_shape=out_shape,
        grid=(m // tm,),
        in_specs=in_specs,
        out_specs=out_specs,
        compiler_params=_cparams(("parallel",)),
        name="norm_proj",
    )(x, gain, w, *extra)


ATTN_CHUNK = 256
ATTN_HEADS_PER_STEP = 4
ATTN_V_ROWS = DIFF_V_DIM + 16


def _diff_attn_kernel(lam_ref, sg_ref, q_ref, k_ref, v_ref, o_ref, vt_ref, s_ref, acc_ref, *, tq, lambda_init):
    qi = pl.program_id(2)
    n_kv_blocks = vt_ref.shape[1]
    n2 = 2 * tq
    ch = min(ATTN_CHUNK, tq)
    nch = tq // ch
    heads = range(ATTN_HEADS_PER_STEP)
    lanes = [slice(hh * LANES, (hh + 1) * LANES) for hh in heads]

    @pl.when(qi == 0)
    def _():
        for hh in heads:
            for c in range(n_kv_blocks):
                vt_ref[hh, c, 0:DIFF_V_DIM, :] = v_ref[0, c * tq:(c + 1) * tq, lanes[hh]].astype(F32).T.astype(BF16)
                ones_row = lax.broadcasted_iota(jnp.int32, (ATTN_V_ROWS - DIFF_V_DIM, tq), 0) == 0
                vt_ref[hh, c, DIFF_V_DIM:ATTN_V_ROWS, :] = jnp.where(ones_row, 1.0, 0.0).astype(BF16)

    qqt = []
    for hh in heads:
        qt = q_ref[0, :, lanes[hh]].astype(F32).T
        feat = lax.broadcasted_iota(jnp.int32, qt.shape, 0)
        map1 = (feat % DIFF_HEAD_DIM) < (DIFF_HEAD_DIM // 2)
        qqt.append(jnp.concatenate([jnp.where(map1, qt, 0.0), jnp.where(map1, 0.0, qt)], axis=1).astype(BF16))
    acc_ref[...] = jnp.zeros(acc_ref.shape, F32)

    def scores(hh, blk, slot, masked):
        cmx8 = None
        for c in range(nch):
            start = pl.multiple_of(blk * tq + c * ch, ch)
            s = jnp.dot(k_ref[0, pl.ds(start, ch), lanes[hh]], qqt[hh], preferred_element_type=F32)
            if masked:
                kv_pos = lax.broadcasted_iota(jnp.int32, s.shape, 0) + c * ch
                col = lax.broadcasted_iota(jnp.int32, s.shape, 1)
                q_pos = jnp.where(col >= tq, col - tq, col)
                s = jnp.where(kv_pos <= q_pos, s, -jnp.inf)
            s_ref[hh, slot, c * ch:(c + 1) * ch, :] = s
            part = jnp.max(s.reshape(ch // 8, 8, n2), axis=0)
            cmx8 = part if cmx8 is None else jnp.maximum(cmx8, part)
        return cmx8

    def softmax_pv(hh, blk, slot, m, cmx8):
        m_new = jnp.maximum(m, jnp.max(cmx8, axis=0, keepdims=True))
        alpha = jnp.exp2(m - m_new)
        pv = None
        for c in range(nch):
            p = jnp.exp2((s_ref[hh, slot, c * ch:(c + 1) * ch, :] - m_new).astype(BF16))
            d = jnp.dot(vt_ref[hh, blk, :, c * ch:(c + 1) * ch], p, preferred_element_type=F32)
            pv = d if pv is None else pv + d
        acc_ref[hh] = alpha * acc_ref[hh] + pv
        return (m_new,)

    cmx8_0 = [scores(hh, qi, 0, True) for hh in heads]

    def body(t, carry):
        slot = lax.rem(t, 2)
        stats = [softmax_pv(hh, jnp.where(t == 0, qi, t - 1), slot, *carry[hh]) for hh in heads]
        return tuple(stats[hh] + (scores(hh, t, 1 - slot, False),) for hh in heads)

    init = tuple((jnp.full((1, n2), -jnp.inf, F32), cmx8_0[hh]) for hh in heads)
    carry = lax.fori_loop(0, qi, body, init)

    lv = lam_ref[...]
    lam = (jnp.exp(jnp.sum(lv[0:1] * lv[1:2], axis=1, keepdims=True))
           - jnp.exp(jnp.sum(lv[2:3] * lv[3:4], axis=1, keepdims=True)) + lambda_init)
    for hh in heads:
        softmax_pv(hh, jnp.maximum(qi - 1, 0), lax.rem(qi, 2), *carry[hh])
    for hh in heads:
        l = acc_ref[hh, DIFF_V_DIM:DIFF_V_DIM + 1, :]
        o1 = acc_ref[hh, 0:DIFF_V_DIM, 0:tq] / l[:, 0:tq]
        o2 = acc_ref[hh, 0:DIFF_V_DIM, tq:n2] / l[:, tq:n2]
        o = o1 - lam * o2
        ms = jnp.mean(o * o, axis=0, keepdims=True)
        o = o * lax.rsqrt(ms + NORM_EPS) * sg_ref[...] * (1.0 - lambda_init)
        o_ref[0, :, lanes[hh]] = o.T.astype(o_ref.dtype)


def _diff_attention(qkv, lam_vecs, subln_gain, *, lambda_init, tq):
    b, s, _ = qkv.shape
    hps = ATTN_HEADS_PER_STEP
    groups = DIFF_HEADS // hps
    width = hps * LANES
    kern = functools.partial(_diff_attn_kernel, tq=tq, lambda_init=lambda_init)
    return pl.pallas_call(
        kern,
        out_shape=jax.ShapeDtypeStruct((b, s, DIFF_HEADS * DIFF_V_DIM), BF16),
        grid=(b, groups, s // tq),
        in_specs=[
            pl.BlockSpec((4, DIFF_HEAD_DIM), lambda bi, gi, qi: (0, 0)),
            pl.BlockSpec((DIFF_V_DIM, 1), lambda bi, gi, qi: (0, 0)),
            pl.BlockSpec((1, tq, width), lambda bi, gi, qi: (bi, qi, gi)),
            pl.BlockSpec((1, s, width), lambda bi, gi, qi: (bi, 0, groups + gi)),
            pl.BlockSpec((1, s, width), lambda bi, gi, qi: (bi, 0, 2 * groups + gi)),
        ],
        out_specs=pl.BlockSpec((1, tq, width), lambda bi, gi, qi: (bi, qi, gi)),
        scratch_shapes=[
            pltpu.VMEM((hps, s // tq, ATTN_V_ROWS, tq), BF16),
            pltpu.VMEM((hps, 2, tq, 2 * tq), F32),
            pltpu.VMEM((hps, ATTN_V_ROWS, 2 * tq), F32),
        ],
        compiler_params=_cparams(("parallel", "parallel", "arbitrary")),
        name="diff_attn",
    )(lam_vecs, subln_gain, qkv, qkv, qkv)


def _split3(x):
    hi = x.astype(BF16)
    r1 = x - hi.astype(F32)
    mid = r1.astype(BF16)
    lo = (r1 - mid.astype(F32)).astype(BF16)
    return hi, mid, lo


def _gla_kernel(q_ref, k_ref, v_ref, r_ref, g_ref, ng_ref, o_ref, state_ref):
    t_blk = q_ref.shape[1]
    hk = GLA_HEADS * GLA_DK

    @pl.when(pl.program_id(1) == 0)
    def _():
        state_ref[...] = jnp.zeros(state_ref.shape, F32)

    ri = lax.broadcasted_iota(jnp.int32, (t_blk, t_blk), 0)
    ci = lax.broadcasted_iota(jnp.int32, (t_blk, t_blk), 1)

    tri = jnp.where(ci <= ri, 1.0, 0.0).astype(BF16)
    g_hi, g_mid, g_lo = _split3(g_ref[0])
    b = (jnp.dot(tri, g_hi, preferred_element_type=F32)
         + jnp.dot(tri, g_mid, preferred_element_type=F32)
         + jnp.dot(tri, g_lo, preferred_element_type=F32))
    b = b * math.log2(math.e)

    q = q_ref[0].astype(F32) * (GLA_DK ** -0.5)
    k = k_ref[0].astype(F32)

    row = lax.broadcasted_iota(jnp.int32, (t_blk, hk), 0)
    b_last = b[t_blk - 1:t_blk, :]
    q_in = (q * jnp.exp2(b)).astype(BF16)
    k_out = (k * jnp.exp2(b_last - b)).astype(BF16)
    e_last = jnp.exp2(b_last)

    level = t_blk // 2
    levels = []
    while level >= GLA_DIAG:
        grp = 2 * level
        b3 = b.reshape(t_blk // grp, grp, hk)
        pivot = jnp.broadcast_to(b3[:, level - 1:level, :], b3.shape).reshape(t_blk, hk)
        upper = (row % grp) >= level
        e = jnp.exp2(-jnp.abs(b - pivot))
        qt = jnp.where(upper, q * e, 0.0).astype(BF16)
        kt = jnp.where(upper, 0.0, k * e).astype(BF16)
        levels.append((grp, qt, kt))
        level //= 2

    nd = t_blk // GLA_DIAG
    b3 = b.reshape(nd, GLA_DIAG, hk)
    q3 = q.reshape(nd, GLA_DIAG, hk)
    k3 = k.reshape(nd, GLA_DIAG, hk)
    sel_shape = (GLA_DIAG * GLA_DK, t_blk)
    sel = jnp.where(lax.broadcasted_iota(jnp.int32, sel_shape, 0) // GLA_DK
                    == lax.broadcasted_iota(jnp.int32, sel_shape, 1) % GLA_DIAG, 1.0, 0.0).astype(BF16)
    diag_mask = ((ri // GLA_DIAG) == (ci // GLA_DIAG)) & (ci <= ri)

    eye = (lax.broadcasted_iota(jnp.int32, (GLA_DK, GLA_DK), 0)
           == lax.broadcasted_iota(jnp.int32, (GLA_DK, GLA_DK), 1))

    diag_terms = []
    for j in range(GLA_DIAG):
        d = jnp.minimum(b3 - b3[:, j:j + 1, :], 0.0)
        pj = (jnp.exp2(d) * q3 * k3[:, j:j + 1, :]).reshape(t_blk, hk)
        diag_terms.append(pj)

    ng = ng_ref[...]
    for h in range(GLA_HEADS):
        ks = slice(h * GLA_DK, (h + 1) * GLA_DK)
        vs = slice(h * GLA_DV, (h + 1) * GLA_DV)
        a = jnp.zeros((t_blk, t_blk), F32)
        for grp, qt, kt in levels:
            s_l = lax.dot_general(qt[:, ks], kt[:, ks], (((1,), (1,)), ((), ())),
                                  preferred_element_type=F32)
            if grp == t_blk:
                a = a + s_l
            else:
                a = a + jnp.where((ri // grp) == (ci // grp), s_l, 0.0)
        stacked = jnp.concatenate([diag_terms[j][:, ks] for j in range(GLA_DIAG)], axis=1)
        a = jnp.where(diag_mask, jnp.dot(stacked.astype(BF16), sel, preferred_element_type=F32), a)

        v_h = v_ref[0, :, vs]
        state = state_ref[h]
        o = (jnp.dot(q_in[:, ks], state.astype(BF16), preferred_element_type=F32)
             + jnp.dot(a.astype(BF16), v_h, preferred_element_type=F32))

        e_col = jnp.sum(jnp.where(eye, jnp.broadcast_to(e_last[:, ks], (GLA_DK, GLA_DK)), 0.0),
                        axis=1, keepdims=True)
        upd = lax.dot_general(k_out[:, ks], v_h, (((0,), (0,)), ((), ())),
                              preferred_element_type=F32)
        state_ref[h] = e_col * state + upd

        r_h = r_ref[0, :, vs].astype(F32)
        gate = r_h / (1.0 + jnp.exp(-r_h))
        o_ref[0, :, vs] = (_rms(o, ng) * gate).astype(o_ref.dtype)


def _gla(proj, g, norm_gain):
    b, s, _ = proj.shape
    t = GLA_BLOCK
    hk = GLA_HEADS * GLA_DK
    hv = GLA_HEADS * GLA_DV
    return pl.pallas_call(
        _gla_kernel,
        out_shape=jax.ShapeDtypeStruct((b, s, hv), BF16),
        grid=(b, s // t),
        in_specs=[
            pl.BlockSpec((1, t, hk), lambda bi, ti: (bi, ti, 0)),
            pl.BlockSpec((1, t, hk), lambda bi, ti: (bi, ti, 1)),
            pl.BlockSpec((1, t, hv), lambda bi, ti: (bi, ti, 1)),
            pl.BlockSpec((1, t, hv), lambda bi, ti: (bi, ti, 2)),
            pl.BlockSpec((1, t, hk), lambda bi, ti: (bi, ti, 0)),
            pl.BlockSpec((1, GLA_DV), lambda bi, ti: (0, 0)),
        ],
        out_specs=pl.BlockSpec((1, t, hv), lambda bi, ti: (bi, ti, 0)),
        scratch_shapes=[pltpu.VMEM((GLA_HEADS, GLA_DK, GLA_DV), F32)],
        compiler_params=_cparams(("parallel", "arbitrary")),
        name="gla",
    )(proj, proj, proj, proj, g, norm_gain)


def _ffn_kernel(x_ref, halo_ref, a_ref, ahalo_ref, wo_ref, g_ref, wup_ref, cw_ref, cb_ref, wd_ref, fg_ref,
                o_ref, xn_ref, slab_ref, *, tiles_per_seq, final_norm, tf):
    i = pl.program_id(0)
    tm, d = x_ref.shape
    f = wd_ref.shape[0]
    rows = tm // FFN_SUBTILES
    ext = rows + FFN_HALO
    seg = ext // 8
    nslab = d // LANES
    subs = range(FFN_SUBTILES)

    gain = g_ref[...]
    w_o = wo_ref[...].astype(BF16)
    resid = []
    for s in subs:
        if s == 0:
            x_ext = jnp.concatenate([halo_ref[...], x_ref[0:rows, :]], axis=0)
            a_ext = jnp.concatenate([ahalo_ref[...], a_ref[0:rows, :]], axis=0)
        else:
            x_ext = x_ref[s * rows - FFN_HALO:(s + 1) * rows, :]
            a_ext = a_ref[s * rows - FFN_HALO:(s + 1) * rows, :]
        h_ext = x_ext + jnp.dot(a_ext, w_o, preferred_element_type=F32)
        resid.append(h_ext[FFN_HALO:ext, :])
        halo = _rms(h_ext[0:FFN_HALO, :], gain)
        if s == 0:
            halo = jnp.where(i % tiles_per_seq == 0, 0.0, halo)
        xn = _rms(resid[s], gain)
        for c in range(nslab):
            slab_ref[s, c, 0:FFN_HALO, :] = halo[:, c * LANES:(c + 1) * LANES]
            slab_ref[s, c, FFN_HALO:ext, :] = xn[:, c * LANES:(c + 1) * LANES]
        for a2 in range(seg // 2):
            grp = [jnp.concatenate([slab_ref[s, c, pl.ds(2 * a2 + r, 8, stride=seg), :] for c in range(nslab)],
                                   axis=1) for r in range(2)]
            xn_ref[s, 16 * a2:16 * a2 + 16, :] = jnp.concatenate(grp, axis=0).astype(BF16)

    def conv(u, cols):
        u3 = u.reshape(seg, 8, u.shape[-1])
        wrap1 = pltpu.roll(u3[seg - 1], 1, 0)[None]
        wrap2 = pltpu.roll(u3[seg - 2], 1, 0)[None]
        prev1 = jnp.concatenate([wrap1, u3[:seg - 1]], axis=0)
        prev2 = jnp.concatenate([wrap2, wrap1, u3[:seg - 2]], axis=0)
        return (cw_ref[2:3, cols] * u3 + cw_ref[1:2, cols] * prev1 + cw_ref[0:1, cols] * prev2
                + cb_ref[:, cols])

    accs = [None for _ in subs]
    for c in range(f // tf):
        gcols = slice(c * tf, (c + 1) * tf)
        ucols = slice(f + c * tf, f + (c + 1) * tf)
        w_gate = wup_ref[:, gcols].astype(BF16)
        w_up = wup_ref[:, ucols].astype(BF16)
        w_down = wd_ref[c * tf:(c + 1) * tf, :].astype(BF16)
        ups = [(jnp.dot(xn_ref[s], w_gate, preferred_element_type=F32),
                jnp.dot(xn_ref[s], w_up, preferred_element_type=F32)) for s in subs]
        acts = []
        for s in subs:
            gate = conv(ups[s][0], gcols)
            up = conv(ups[s][1], ucols)
            acts.append(((gate / (1.0 + jnp.exp(-gate))) * up).reshape(ext, tf).astype(BF16))
        for s in subs:
            part = jnp.dot(acts[s], w_down, preferred_element_type=F32)
            accs[s] = part if accs[s] is None else accs[s] + part

    for s in subs:
        for a in range(seg):
            for c in range(nslab):
                slab_ref[s, c, pl.ds(a, 8, stride=seg), :] = accs[s][8 * a:8 * a + 8, c * LANES:(c + 1) * LANES]
        y = resid[s] + jnp.concatenate([slab_ref[s, c, FFN_HALO:ext, :] for c in range(nslab)], axis=1)
        if final_norm:
            y = _rms(y, fg_ref[...])
        o_ref[s * rows:(s + 1) * rows, :] = y


def _ffn(x, a, w_o, mixer_layer, gain, w_up, conv_w, conv_b, w_down, layer, final_gain, *, tm, tf, seq,
         final_norm):
    m, d = x.shape
    f = w_down.shape[1]
    tiles_per_seq = seq // tm
    halo_blocks = tm // FFN_HALO
    ext = tm // FFN_SUBTILES + FFN_HALO
    kern = functools.partial(_ffn_kernel, tiles_per_seq=tiles_per_seq, final_norm=final_norm, tf=tf)
    return pl.pallas_call(
        kern,
        out_shape=jax.ShapeDtypeStruct((m, d), F32),
        grid=(m // tm,),
        in_specs=[
            pl.BlockSpec((tm, d), lambda i: (i, 0)),
            pl.BlockSpec((FFN_HALO, d), lambda i: (jnp.maximum(i * halo_blocks - 1, 0), 0)),
            pl.BlockSpec((tm, d), lambda i: (i, 0)),
            pl.BlockSpec((FFN_HALO, d), lambda i: (jnp.maximum(i * halo_blocks - 1, 0), 0)),
            pl.BlockSpec((None, d, d), lambda i: (mixer_layer, 0, 0)),
            pl.BlockSpec((None, 1, d), lambda i: (layer, 0, 0)),
            pl.BlockSpec((None, d, 2 * f), lambda i: (layer, 0, 0)),
            pl.BlockSpec((None, CONV_WIDTH, 2 * f), lambda i: (layer, 0, 0)),
            pl.BlockSpec((None, 1, 2 * f), lambda i: (layer, 0, 0)),
            pl.BlockSpec((None, f, d), lambda i: (layer, 0, 0)),
            pl.BlockSpec((1, d), lambda i: (0, 0)),
        ],
        out_specs=pl.BlockSpec((tm, d), lambda i: (i, 0)),
        scratch_shapes=[pltpu.VMEM((FFN_SUBTILES, ext, d), BF16),
                        pltpu.VMEM((FFN_SUBTILES, d // LANES, ext, LANES), F32)],
        compiler_params=_cparams(("parallel",)),
        name="ffn",
    )(x, x, a, a, w_o, gain, w_up, conv_w, conv_b, w_down, final_gain)


def _rope_tables(seq, q_scale):
    half = DIFF_HEAD_DIM // 2
    inv = 1.0 / (ROPE_THETA ** (jnp.arange(0, DIFF_HEAD_DIM, 2, dtype=F32) / DIFF_HEAD_DIM))
    ang = jnp.arange(seq, dtype=F32)[:, None] * inv[None, :]
    cos, sin = jnp.cos(ang), jnp.sin(ang)
    cos_t = jnp.tile(cos, (1, LANES // half))
    sin_t = jnp.concatenate([-sin, -sin, sin, sin], axis=1)
    cos3 = jnp.stack([cos_t * q_scale, cos_t, jnp.ones_like(cos_t)])
    sin3 = jnp.stack([sin_t * q_scale, sin_t, jnp.zeros_like(sin_t)])
    return cos3, sin3


def _reorder_qk_columns(w_qkv):
    layers, d, _ = w_qkv.shape
    width = DIFF_HEADS * DIFF_V_DIM
    half = DIFF_HEAD_DIM // 2
    qk = w_qkv[:, :, :2 * width].reshape(layers, d, 2, DIFF_HEADS, 2, 2, half)
    qk = qk.transpose(0, 1, 2, 3, 5, 4, 6).reshape(layers, d, 2 * width)
    return jnp.concatenate([qk, w_qkv[:, :, 2 * width:]], axis=2).astype(BF16)


def _trunk(x, norm_mix, norm_ffn, norm_final, diff_w_qkv, diff_w_o, diff_lambda, diff_subln,
           gla_w_in, gla_w_a1, gla_w_a2, gla_b_a, gla_norm, gla_w_o,
           ffn_w_up, ffn_conv_w, ffn_conv_b, ffn_w_down, *, tm, tp, tq, tf):
    bsz, seq, d = x.shape
    depth = norm_mix.shape[0]
    m = bsz * seq
    tables = _rope_tables(seq, DIFF_HEAD_DIM ** -0.5 * math.log2(math.e))
    w_qkv = _reorder_qk_columns(diff_w_qkv)
    w_up = ffn_w_up.astype(BF16)
    w_down = ffn_w_down.astype(BF16)
    h = x.reshape(m, d)
    rank = gla_w_a1.shape[-1]
    for layer in range(depth):
        jdx = layer // 2
        gain = norm_mix[layer].reshape(1, d)
        if layer % 2 == 0:
            lambda_init = 0.8 - 0.6 * math.exp(-0.3 * layer)
            qkv = _norm_proj(h, gain, w_qkv, jdx, tables=tables, tm=tp, seq=seq)
            o = _diff_attention(qkv.reshape(bsz, seq, 3 * d), diff_lambda[jdx],
                                diff_subln[jdx].reshape(DIFF_V_DIM, 1),
                                lambda_init=lambda_init, tq=tq)
            w_o = diff_w_o
        else:
            w1 = jnp.pad(gla_w_a1[jdx], ((0, 0), (0, LANES - rank))).astype(BF16)
            w2 = jnp.pad(gla_w_a2[jdx], ((0, LANES - rank), (0, 0)))
            proj, g = _norm_proj(h, gain, gla_w_in, jdx, gate_params=(w1, w2, gla_b_a[jdx].reshape(1, -1)),
                                 tm=tp, seq=seq)
            o = _gla(proj.reshape(bsz, seq, 3 * d), g.reshape(bsz, seq, -1),
                     gla_norm[jdx].reshape(1, GLA_DV))
            w_o = gla_w_o
        h = _ffn(h, o.reshape(m, d), w_o, jdx, norm_ffn.reshape(depth, 1, d), w_up, ffn_conv_w, ffn_conv_b.reshape(depth, 1, -1),
                 w_down, layer, norm_final.reshape(1, d), tm=tp, tf=tf, seq=seq,
                 final_norm=(layer == depth - 1))
    return h.reshape(bsz, seq, d)


def kernel(x, norm_mix, norm_ffn, norm_final, diff_w_qkv, diff_w_o, diff_lambda, diff_subln, gla_w_in, gla_w_a1, gla_w_a2, gla_b_a, gla_norm, gla_w_o, ffn_w_up, ffn_conv_w, ffn_conv_b, ffn_w_down):
    return _trunk(x, norm_mix, norm_ffn, norm_final, diff_w_qkv, diff_w_o, diff_lambda, diff_subln,
                  gla_w_in, gla_w_a1, gla_w_a2, gla_b_a, gla_norm, gla_w_o,
                  ffn_w_up, ffn_conv_w, ffn_conv_b, ffn_w_down, tm=1024, tp=512, tq=512, tf=256)
```

```python
import functools
import math

import jax
import jax.numpy as jnp
from jax import lax
from jax.experimental import pallas as pl
from jax.experimental.pallas import tpu as pltpu

F32 = jnp.float32
BF16 = jnp.bfloat16

NORM_EPS = 1e-6
ROPE_THETA = 10000.0

DIFF_HEADS = 8
DIFF_HEAD_DIM = 64
DIFF_V_DIM = 2 * DIFF_HEAD_DIM

GLA_HEADS = 4
GLA_DK = 128
GLA_DV = 256
GLA_GATE_TEMP = 16.0
GLA_BLOCK = 256
GLA_DIAG = 8

CONV_WIDTH = 3
FFN_HALO = 16
FFN_SUBTILES = 2

LANES = 128
VMEM_LIMIT_BYTES = 56 * 1024 * 1024


def _cparams(semantics):
    return pltpu.CompilerParams(dimension_semantics=semantics, vmem_limit_bytes=VMEM_LIMIT_BYTES)


def _rms(x, gain):
    ms = jnp.mean(x * x, axis=-1, keepdims=True)
    return x * lax.rsqrt(ms + NORM_EPS) * gain


NORM_PROJ_COLS = 1024


def _norm_proj_kernel(x_ref, g_ref, w_ref, wup_ref, wdn_ref, *rest, rope, gate):
    if rope:
        cos_ref, sin_ref, o_ref, wup_out_ref, wdn_out_ref = rest
    elif gate:
        w1_ref, w2_ref, b_ref, o_ref, gate_ref, wup_out_ref, wdn_out_ref = rest
    else:
        o_ref, wup_out_ref, wdn_out_ref = rest
    wup_out_ref[...] = wup_ref[...].astype(BF16)
    wdn_out_ref[...] = wdn_ref[...].astype(BF16)
    xn = _rms(x_ref[...], g_ref[...]).astype(BF16)
    if gate:
        t = jnp.dot(xn, w1_ref[...], preferred_element_type=F32)
        w2 = w2_ref[...]
        t_hi, w_hi = t.astype(BF16), w2.astype(BF16)
        t_lo, w_lo = (t - t_hi.astype(F32)).astype(BF16), (w2 - w_hi.astype(F32)).astype(BF16)
        z = (jnp.dot(t_hi, w_hi, preferred_element_type=F32) + jnp.dot(t_hi, w_lo, preferred_element_type=F32)
             + jnp.dot(t_lo, w_hi, preferred_element_type=F32) + b_ref[...])
        gate_ref[...] = (jnp.minimum(z, 0.0) - jnp.log1p(jnp.exp(-jnp.abs(z)))) * (1.0 / GLA_GATE_TEMP)
    for cb in range(o_ref.shape[1] // NORM_PROJ_COLS):
        cols = slice(cb * NORM_PROJ_COLS, (cb + 1) * NORM_PROJ_COLS)
        acc = jnp.dot(xn, w_ref[:, cols].astype(BF16), preferred_element_type=F32)
        if not rope:
            o_ref[:, cols] = acc.astype(o_ref.dtype)
            continue
        c = cos_ref[cb]
        s = sin_ref[cb]
        for hh in range(NORM_PROJ_COLS // LANES):
            t = acc[:, hh * LANES:(hh + 1) * LANES]
            lo = cb * NORM_PROJ_COLS + hh * LANES
            o_ref[:, lo:lo + LANES] = (t * c + pltpu.roll(t, LANES // 2, 1) * s).astype(o_ref.dtype)


def _norm_proj(x, gain, w, layer, ffn_w_up, ffn_w_down, ffn_layer, tables=None, gate_params=None, *, tm, seq):
    m, d = x.shape
    n = w.shape[2]
    nseq = seq // tm
    steps = m // tm
    rope = tables is not None
    gate = gate_params is not None
    up_rows, up_cols = ffn_w_up.shape[1:]
    dn_rows, dn_cols = ffn_w_down.shape[1:]
    up_slab = up_rows // steps
    dn_steps = steps // 2
    dn_slab = dn_rows // dn_steps
    assert up_slab * steps == up_rows and up_slab % 16 == 0 and dn_slab * dn_steps == dn_rows and dn_slab % 16 == 0
    in_specs = [
        pl.BlockSpec((tm, d), lambda i: (i, 0)),
        pl.BlockSpec((1, d), lambda i: (0, 0)),
        pl.BlockSpec((None, d, n), lambda i: (layer, 0, 0)),
        pl.BlockSpec((None, up_slab, up_cols), lambda i: (ffn_layer, i, 0)),
        pl.BlockSpec((None, dn_slab, dn_cols), lambda i: (ffn_layer, i // 2, 0)),
    ]
    out_shape = [jax.ShapeDtypeStruct((m, n), BF16)]
    out_specs = [pl.BlockSpec((tm, n), lambda i: (i, 0))]
    cast_shapes = [jax.ShapeDtypeStruct((up_rows, up_cols), BF16), jax.ShapeDtypeStruct((dn_rows, dn_cols), BF16)]
    cast_specs = [pl.BlockSpec((up_slab, up_cols), lambda i: (i, 0)),
                  pl.BlockSpec((dn_slab, dn_cols), lambda i: (i // 2, 0))]
    extra = ()
    if rope:
        nb = tables[0].shape[0]
        assert nb == n // NORM_PROJ_COLS
        in_specs += [pl.BlockSpec((nb, tm, LANES), lambda i: (0, i % nseq, 0))] * 2
        extra = tuple(tables)
    elif gate:
        w1, w2, bias = gate_params
        in_specs += [pl.BlockSpec(w1.shape, lambda i: (0, 0)), pl.BlockSpec(w2.shape, lambda i: (0, 0)),
                     pl.BlockSpec(bias.shape, lambda i: (0, 0))]
        extra = (w1, w2, bias)
        out_shape.append(jax.ShapeDtypeStruct((m, w2.shape[1]), F32))
        out_specs.append(pl.BlockSpec((tm, w2.shape[1]), lambda i: (i, 0)))
    return pl.pallas_call(
        functools.partial(_norm_proj_kernel, rope=rope, gate=gate),
        out_shape=tuple(out_shape + cast_shapes),
        grid=(steps,),
        in_specs=in_specs,
        out_specs=tuple(out_specs + cast_specs),
        compiler_params=_cparams(("arbitrary",)),
        name="norm_proj",
    )(x, gain, w, ffn_w_up, ffn_w_down, *extra)


ATTN_CHUNK = 256
ATTN_HEADS_PER_STEP = 4
ATTN_V_ROWS = DIFF_V_DIM + 16


def _diff_attn_kernel(lam_ref, sg_ref, q_ref, k_ref, v_ref, o_ref, vt_ref, s_ref, acc_ref, *, tq, lambda_init):
    qi = pl.program_id(2)
    n_kv_blocks = vt_ref.shape[1]
    n2 = 2 * tq
    ch = min(ATTN_CHUNK, tq)
    nch = tq // ch
    heads = range(ATTN_HEADS_PER_STEP)
    lanes = [slice(hh * LANES, (hh + 1) * LANES) for hh in heads]

    @pl.when(qi == 0)
    def _():
        for hh in heads:
            for c in range(n_kv_blocks):
                vt_ref[hh, c, 0:DIFF_V_DIM, :] = v_ref[0, c * tq:(c + 1) * tq, lanes[hh]].astype(F32).T.astype(BF16)
                ones_row = lax.broadcasted_iota(jnp.int32, (ATTN_V_ROWS - DIFF_V_DIM, tq), 0) == 0
                vt_ref[hh, c, DIFF_V_DIM:ATTN_V_ROWS, :] = jnp.where(ones_row, 1.0, 0.0).astype(BF16)

    qqt = []
    for hh in heads:
        qt = q_ref[0, :, lanes[hh]].astype(F32).T
        feat = lax.broadcasted_iota(jnp.int32, qt.shape, 0)
        map1 = (feat % DIFF_HEAD_DIM) < (DIFF_HEAD_DIM // 2)
        qqt.append(jnp.concatenate([jnp.where(map1, qt, 0.0), jnp.where(map1, 0.0, qt)], axis=1).astype(BF16))
    acc_ref[...] = jnp.zeros(acc_ref.shape, F32)

    def scores(hh, blk, slot, masked):
        cmx8 = None
        for c in range(nch):
            start = pl.multiple_of(blk * tq + c * ch, ch)
            s = jnp.dot(k_ref[0, pl.ds(start, ch), lanes[hh]], qqt[hh], preferred_element_type=F32)
            if masked:
                kv_pos = lax.broadcasted_iota(jnp.int32, s.shape, 0) + c * ch
                col = lax.broadcasted_iota(jnp.int32, s.shape, 1)
                q_pos = jnp.where(col >= tq, col - tq, col)
                s = jnp.where(kv_pos <= q_pos, s, -jnp.inf)
            s_ref[hh, slot, c * ch:(c + 1) * ch, :] = s
            part = jnp.max(s.reshape(ch // 8, 8, n2), axis=0)
            cmx8 = part if cmx8 is None else jnp.maximum(cmx8, part)
        return cmx8

    def softmax_pv(hh, blk, slot, m, cmx8):
        m_new = jnp.maximum(m, jnp.max(cmx8, axis=0, keepdims=True))
        alpha = jnp.exp2(m - m_new)
        pv = None
        for c in range(nch):
            p = jnp.exp2((s_ref[hh, slot, c * ch:(c + 1) * ch, :] - m_new).astype(BF16))
            d = jnp.dot(vt_ref[hh, blk, :, c * ch:(c + 1) * ch], p, preferred_element_type=F32)
            pv = d if pv is None else pv + d
        acc_ref[hh] = alpha * acc_ref[hh] + pv
        return (m_new,)

    cmx8_0 = [scores(hh, qi, 0, True) for hh in heads]

    def body(t, carry):
        slot = lax.rem(t, 2)
        stats = [softmax_pv(hh, jnp.where(t == 0, qi, t - 1), slot, *carry[hh]) for hh in heads]
        return tuple(stats[hh] + (scores(hh, t, 1 - slot, False),) for hh in heads)

    init = tuple((jnp.full((1, n2), -jnp.inf, F32), cmx8_0[hh]) for hh in heads)
    carry = lax.fori_loop(0, qi, body, init)

    lv = lam_ref[...]
    lam = (jnp.exp(jnp.sum(lv[0:1] * lv[1:2], axis=1, keepdims=True))
           - jnp.exp(jnp.sum(lv[2:3] * lv[3:4], axis=1, keepdims=True)) + lambda_init)
    for hh in heads:
        softmax_pv(hh, jnp.maximum(qi - 1, 0), lax.rem(qi, 2), *carry[hh])
    for hh in heads:
        l = acc_ref[hh, DIFF_V_DIM:DIFF_V_DIM + 1, :]
        o1 = acc_ref[hh, 0:DIFF_V_DIM, 0:tq] / l[:, 0:tq]
        o2 = acc_ref[hh, 0:DIFF_V_DIM, tq:n2] / l[:, tq:n2]
        o = o1 - lam * o2
        ms = jnp.mean(o * o, axis=0, keepdims=True)
        o = o * lax.rsqrt(ms + NORM_EPS) * sg_ref[...] * (1.0 - lambda_init)
        o_ref[0, :, lanes[hh]] = o.T.astype(o_ref.dtype)


def _diff_attention(qkv, lam_vecs, subln_gain, *, lambda_init, tq):
    b, s, _ = qkv.shape
    hps = ATTN_HEADS_PER_STEP
    groups = DIFF_HEADS // hps
    width = hps * LANES
    kern = functools.partial(_diff_attn_kernel, tq=tq, lambda_init=lambda_init)
    return pl.pallas_call(
        kern,
        out_shape=jax.ShapeDtypeStruct((b, s, DIFF_HEADS * DIFF_V_DIM), BF16),
        grid=(b, groups, s // tq),
        in_specs=[
            pl.BlockSpec((4, DIFF_HEAD_DIM), lambda bi, gi, qi: (0, 0)),
            pl.BlockSpec((DIFF_V_DIM, 1), lambda bi, gi, qi: (0, 0)),
            pl.BlockSpec((1, tq, width), lambda bi, gi, qi: (bi, qi, gi)),
            pl.BlockSpec((1, s, width), lambda bi, gi, qi: (bi, 0, groups + gi)),
            pl.BlockSpec((1, s, width), lambda bi, gi, qi: (bi, 0, 2 * groups + gi)),
        ],
        out_specs=pl.BlockSpec((1, tq, width), lambda bi, gi, qi: (bi, qi, gi)),
        scratch_shapes=[
            pltpu.VMEM((hps, s // tq, ATTN_V_ROWS, tq), BF16),
            pltpu.VMEM((hps, 2, tq, 2 * tq), F32),
            pltpu.VMEM((hps, ATTN_V_ROWS, 2 * tq), F32),
        ],
        compiler_params=_cparams(("parallel", "parallel", "arbitrary")),
        name="diff_attn",
    )(lam_vecs, subln_gain, qkv, qkv, qkv)


def _split3(x):
    hi = x.astype(BF16)
    r1 = x - hi.astype(F32)
    mid = r1.astype(BF16)
    lo = (r1 - mid.astype(F32)).astype(BF16)
    return hi, mid, lo


def _gla_kernel(q_ref, k_ref, v_ref, r_ref, g_ref, ng_ref, o_ref, state_ref):
    t_blk = q_ref.shape[1]
    hk = GLA_HEADS * GLA_DK

    @pl.when(pl.program_id(1) == 0)
    def _():
        state_ref[...] = jnp.zeros(state_ref.shape, F32)

    ri = lax.broadcasted_iota(jnp.int32, (t_blk, t_blk), 0)
    ci = lax.broadcasted_iota(jnp.int32, (t_blk, t_blk), 1)

    tri = jnp.where(ci <= ri, 1.0, 0.0).astype(BF16)
    g_hi, g_mid, g_lo = _split3(g_ref[0])
    b = (jnp.dot(tri, g_hi, preferred_element_type=F32)
         + jnp.dot(tri, g_mid, preferred_element_type=F32)
         + jnp.dot(tri, g_lo, preferred_element_type=F32))
    b = b * math.log2(math.e)

    q = q_ref[0].astype(F32) * (GLA_DK ** -0.5)
    k = k_ref[0].astype(F32)

    row = lax.broadcasted_iota(jnp.int32, (t_blk, hk), 0)
    b_last = b[t_blk - 1:t_blk, :]
    q_in = (q * jnp.exp2(b)).astype(BF16)
    k_out = (k * jnp.exp2(b_last - b)).astype(BF16)
    e_last = jnp.exp2(b_last)

    level = t_blk // 2
    levels = []
    while level >= GLA_DIAG:
        grp = 2 * level
        b3 = b.reshape(t_blk // grp, grp, hk)
        pivot = jnp.broadcast_to(b3[:, level - 1:level, :], b3.shape).reshape(t_blk, hk)
        upper = (row % grp) >= level
        e = jnp.exp2(-jnp.abs(b - pivot))
        qt = jnp.where(upper, q * e, 0.0).astype(BF16)
        kt = jnp.where(upper, 0.0, k * e).astype(BF16)
        levels.append((grp, qt, kt))
        level //= 2

    nd = t_blk // GLA_DIAG
    b3 = b.reshape(nd, GLA_DIAG, hk)
    q3 = q.reshape(nd, GLA_DIAG, hk)
    k3 = k.reshape(nd, GLA_DIAG, hk)
    sel_shape = (GLA_DIAG * GLA_DK, t_blk)
    sel = jnp.where(lax.broadcasted_iota(jnp.int32, sel_shape, 0) // GLA_DK
                    == lax.broadcasted_iota(jnp.int32, sel_shape, 1) % GLA_DIAG, 1.0, 0.0).astype(BF16)
    diag_mask = ((ri // GLA_DIAG) == (ci // GLA_DIAG)) & (ci <= ri)

    eye = (lax.broadcasted_iota(jnp.int32, (GLA_DK, GLA_DK), 0)
           == lax.broadcasted_iota(jnp.int32, (GLA_DK, GLA_DK), 1))

    diag_terms = []
    for j in range(GLA_DIAG):
        d = jnp.minimum(b3 - b3[:, j:j + 1, :], 0.0)
        pj = (jnp.exp2(d) * q3 * k3[:, j:j + 1, :]).reshape(t_blk, hk)
        diag_terms.append(pj)

    ng = ng_ref[...]
    for h in range(GLA_HEADS):
        ks = slice(h * GLA_DK, (h + 1) * GLA_DK)
        vs = slice(h * GLA_DV, (h + 1) * GLA_DV)
        a = jnp.zeros((t_blk, t_blk), F32)
        for grp, qt, kt in levels:
            s_l = lax.dot_general(qt[:, ks], kt[:, ks], (((1,), (1,)), ((), ())),
                                  preferred_element_type=F32)
            if grp == t_blk:
                a = a + s_l
            else:
                a = a + jnp.where((ri // grp) == (ci // grp), s_l, 0.0)
        stacked = jnp.concatenate([diag_terms[j][:, ks] for j in range(GLA_DIAG)], axis=1)
        a = jnp.where(diag_mask, jnp.dot(stacked.astype(BF16), sel, preferred_element_type=F32), a)

        v_h = v_ref[0, :, vs]
        state = state_ref[h]
        o = (jnp.dot(q_in[:, ks], state.astype(BF16), preferred_element_type=F32)
             + jnp.dot(a.astype(BF16), v_h, preferred_element_type=F32))

        e_col = jnp.sum(jnp.where(eye, jnp.broadcast_to(e_last[:, ks], (GLA_DK, GLA_DK)), 0.0),
                        axis=1, keepdims=True)
        upd = lax.dot_general(k_out[:, ks], v_h, (((0,), (0,)), ((), ())),
                              preferred_element_type=F32)
        state_ref[h] = e_col * state + upd

        r_h = r_ref[0, :, vs].astype(F32)
        gate = r_h / (1.0 + jnp.exp(-r_h))
        o_ref[0, :, vs] = (_rms(o, ng) * gate).astype(o_ref.dtype)


def _gla(proj, g, norm_gain):
    b, s, _ = proj.shape
    t = GLA_BLOCK
    hk = GLA_HEADS * GLA_DK
    hv = GLA_HEADS * GLA_DV
    return pl.pallas_call(
        _gla_kernel,
        out_shape=jax.ShapeDtypeStruct((b, s, hv), BF16),
        grid=(b, s // t),
        in_specs=[
            pl.BlockSpec((1, t, hk), lambda bi, ti: (bi, ti, 0)),
            pl.BlockSpec((1, t, hk), lambda bi, ti: (bi, ti, 1)),
            pl.BlockSpec((1, t, hv), lambda bi, ti: (bi, ti, 1)),
            pl.BlockSpec((1, t, hv), lambda bi, ti: (bi, ti, 2)),
            pl.BlockSpec((1, t, hk), lambda bi, ti: (bi, ti, 0)),
            pl.BlockSpec((1, GLA_DV), lambda bi, ti: (0, 0)),
        ],
        out_specs=pl.BlockSpec((1, t, hv), lambda bi, ti: (bi, ti, 0)),
        scratch_shapes=[pltpu.VMEM((GLA_HEADS, GLA_DK, GLA_DV), F32)],
        compiler_params=_cparams(("parallel", "arbitrary")),
        name="gla",
    )(proj, proj, proj, proj, g, norm_gain)


def _ffn_kernel(x_ref, halo_ref, a_ref, ahalo_ref, wo_ref, g_ref, wup_ref, cw_ref, cb_ref, wd_ref, fg_ref,
                o_ref, xn_ref, slab_ref, *, tiles_per_seq, final_norm, tf):
    i = pl.program_id(0)
    tm, d = x_ref.shape
    f = wd_ref.shape[0]
    rows = tm // FFN_SUBTILES
    ext = rows + FFN_HALO
    seg = ext // 8
    nslab = d // LANES
    subs = range(FFN_SUBTILES)

    gain = g_ref[...]
    w_o = wo_ref[...].astype(BF16)
    resid = []
    for s in subs:
        if s == 0:
            x_ext = jnp.concatenate([halo_ref[...], x_ref[0:rows, :]], axis=0)
            a_ext = jnp.concatenate([ahalo_ref[...], a_ref[0:rows, :]], axis=0)
        else:
            x_ext = x_ref[s * rows - FFN_HALO:(s + 1) * rows, :]
            a_ext = a_ref[s * rows - FFN_HALO:(s + 1) * rows, :]
        h_ext = x_ext + jnp.dot(a_ext, w_o, preferred_element_type=F32)
        resid.append(h_ext[FFN_HALO:ext, :])
        halo = _rms(h_ext[0:FFN_HALO, :], gain)
        if s == 0:
            halo = jnp.where(i % tiles_per_seq == 0, 0.0, halo)
        xn = _rms(resid[s], gain)
        for c in range(nslab):
            slab_ref[s, c, 0:FFN_HALO, :] = halo[:, c * LANES:(c + 1) * LANES]
            slab_ref[s, c, FFN_HALO:ext, :] = xn[:, c * LANES:(c + 1) * LANES]
        for a2 in range(seg // 2):
            grp = [jnp.concatenate([slab_ref[s, c, pl.ds(2 * a2 + r, 8, stride=seg), :] for c in range(nslab)],
                                   axis=1) for r in range(2)]
            xn_ref[s, 16 * a2:16 * a2 + 16, :] = jnp.concatenate(grp, axis=0).astype(BF16)

    def conv(u, cols):
        u3 = u.reshape(seg, 8, u.shape[-1])
        wrap1 = pltpu.roll(u3[seg - 1], 1, 0)[None]
        wrap2 = pltpu.roll(u3[seg - 2], 1, 0)[None]
        prev1 = jnp.concatenate([wrap1, u3[:seg - 1]], axis=0)
        prev2 = jnp.concatenate([wrap2, wrap1, u3[:seg - 2]], axis=0)
        return (cw_ref[2:3, cols] * u3 + cw_ref[1:2, cols] * prev1 + cw_ref[0:1, cols] * prev2
                + cb_ref[:, cols])

    accs = [None for _ in subs]
    for c in range(f // tf):
        gcols = slice(c * tf, (c + 1) * tf)
        ucols = slice(f + c * tf, f + (c + 1) * tf)
        w_gate = wup_ref[:, gcols].astype(BF16)
        w_up = wup_ref[:, ucols].astype(BF16)
        w_down = wd_ref[c * tf:(c + 1) * tf, :].astype(BF16)
        ups = [(jnp.dot(xn_ref[s], w_gate, preferred_element_type=F32),
                jnp.dot(xn_ref[s], w_up, preferred_element_type=F32)) for s in subs]
        acts = []
        for s in subs:
            gate = conv(ups[s][0], gcols)
            up = conv(ups[s][1], ucols)
            acts.append(((gate / (1.0 + jnp.exp(-gate))) * up).reshape(ext, tf).astype(BF16))
        for s in subs:
            part = jnp.dot(acts[s], w_down, preferred_element_type=F32)
            accs[s] = part if accs[s] is None else accs[s] + part

    for s in subs:
        for a in range(seg):
            for c in range(nslab):
                slab_ref[s, c, pl.ds(a, 8, stride=seg), :] = accs[s][8 * a:8 * a + 8, c * LANES:(c + 1) * LANES]
        y = resid[s] + jnp.concatenate([slab_ref[s, c, FFN_HALO:ext, :] for c in range(nslab)], axis=1)
        if final_norm:
            y = _rms(y, fg_ref[...])
        o_ref[s * rows:(s + 1) * rows, :] = y


def _ffn(x, a, w_o, mixer_layer, gain, w_up, conv_w, conv_b, w_down, layer, final_gain, *, tm, tf, seq,
         final_norm):
    m, d = x.shape
    f = w_down.shape[0]
    tiles_per_seq = seq // tm
    halo_blocks = tm // FFN_HALO
    ext = tm // FFN_SUBTILES + FFN_HALO
    kern = functools.partial(_ffn_kernel, tiles_per_seq=tiles_per_seq, final_norm=final_norm, tf=tf)
    return pl.pallas_call(
        kern,
        out_shape=jax.ShapeDtypeStruct((m, d), F32),
        grid=(m // tm,),
        in_specs=[
            pl.BlockSpec((tm, d), lambda i: (i, 0)),
            pl.BlockSpec((FFN_HALO, d), lambda i: (jnp.maximum(i * halo_blocks - 1, 0), 0)),
            pl.BlockSpec((tm, d), lambda i: (i, 0)),
            pl.BlockSpec((FFN_HALO, d), lambda i: (jnp.maximum(i * halo_blocks - 1, 0), 0)),
            pl.BlockSpec((None, d, d), lambda i: (mixer_layer, 0, 0)),
            pl.BlockSpec((None, 1, d), lambda i: (layer, 0, 0)),
            pl.BlockSpec((d, 2 * f), lambda i: (0, 0)),
            pl.BlockSpec((None, CONV_WIDTH, 2 * f), lambda i: (layer, 0, 0)),
            pl.BlockSpec((None, 1, 2 * f), lambda i: (layer, 0, 0)),
            pl.BlockSpec((f, d), lambda i: (0, 0)),
            pl.BlockSpec((1, d), lambda i: (0, 0)),
        ],
        out_specs=pl.BlockSpec((tm, d), lambda i: (i, 0)),
        scratch_shapes=[pltpu.VMEM((FFN_SUBTILES, ext, d), BF16),
                        pltpu.VMEM((FFN_SUBTILES, d // LANES, ext, LANES), F32)],
        compiler_params=_cparams(("parallel",)),
        name="ffn",
    )(x, x, a, a, w_o, gain, w_up, conv_w, conv_b, w_down, final_gain)


def _rope_tables(seq, q_scale):
    half = DIFF_HEAD_DIM // 2
    inv = 1.0 / (ROPE_THETA ** (jnp.arange(0, DIFF_HEAD_DIM, 2, dtype=F32) / DIFF_HEAD_DIM))
    ang = jnp.arange(seq, dtype=F32)[:, None] * inv[None, :]
    cos, sin = jnp.cos(ang), jnp.sin(ang)
    cos_t = jnp.tile(cos, (1, LANES // half))
    sin_t = jnp.concatenate([-sin, -sin, sin, sin], axis=1)
    cos3 = jnp.stack([cos_t * q_scale, cos_t, jnp.ones_like(cos_t)])
    sin3 = jnp.stack([sin_t * q_scale, sin_t, jnp.zeros_like(sin_t)])
    return cos3, sin3


def _reorder_qk_columns(w_qkv):
    layers, d, _ = w_qkv.shape
    width = DIFF_HEADS * DIFF_V_DIM
    half = DIFF_HEAD_DIM // 2
    qk = w_qkv[:, :, :2 * width].reshape(layers, d, 2, DIFF_HEADS, 2, 2, half)
    qk = qk.transpose(0, 1, 2, 3, 5, 4, 6).reshape(layers, d, 2 * width)
    return jnp.concatenate([qk, w_qkv[:, :, 2 * width:]], axis=2).astype(BF16)


def _trunk(x, norm_mix, norm_ffn, norm_final, diff_w_qkv, diff_w_o, diff_lambda, diff_subln,
           gla_w_in, gla_w_a1, gla_w_a2, gla_b_a, gla_norm, gla_w_o,
           ffn_w_up, ffn_conv_w, ffn_conv_b, ffn_w_down, *, tm, tp, tq, tf):
    bsz, seq, d = x.shape
    depth = norm_mix.shape[0]
    m = bsz * seq
    tables = _rope_tables(seq, DIFF_HEAD_DIM ** -0.5 * math.log2(math.e))
    w_qkv = _reorder_qk_columns(diff_w_qkv)
    h = x.reshape(m, d)
    rank = gla_w_a1.shape[-1]
    for layer in range(depth):
        jdx = layer // 2
        gain = norm_mix[layer].reshape(1, d)
        if layer % 2 == 0:
            lambda_init = 0.8 - 0.6 * math.exp(-0.3 * layer)
            qkv, w_up, w_down = _norm_proj(h, gain, w_qkv, jdx, ffn_w_up, ffn_w_down, layer, tables=tables,
                                           tm=tp, seq=seq)
            o = _diff_attention(qkv.reshape(bsz, seq, 3 * d), diff_lambda[jdx],
                                diff_subln[jdx].reshape(DIFF_V_DIM, 1),
                                lambda_init=lambda_init, tq=tq)
            w_o = diff_w_o
        else:
            w1 = jnp.pad(gla_w_a1[jdx], ((0, 0), (0, LANES - rank))).astype(BF16)
            w2 = jnp.pad(gla_w_a2[jdx], ((0, LANES - rank), (0, 0)))
            proj, g, w_up, w_down = _norm_proj(h, gain, gla_w_in, jdx, ffn_w_up, ffn_w_down, layer,
                                               gate_params=(w1, w2, gla_b_a[jdx].reshape(1, -1)), tm=tp, seq=seq)
            o = _gla(proj.reshape(bsz, seq, 3 * d), g.reshape(bsz, seq, -1),
                     gla_norm[jdx].reshape(1, GLA_DV))
            w_o = gla_w_o
        h = _ffn(h, o.reshape(m, d), w_o, jdx, norm_ffn.reshape(depth, 1, d), w_up, ffn_conv_w, ffn_conv_b.reshape(depth, 1, -1),
                 w_down, layer, norm_final.reshape(1, d), tm=tp, tf=tf, seq=seq,
                 final_norm=(layer == depth - 1))
    return h.reshape(bsz, seq, d)


def kernel(x, norm_mix, norm_ffn, norm_final, diff_w_qkv, diff_w_o, diff_lambda, diff_subln, gla_w_in, gla_w_a1, gla_w_a2, gla_b_a, gla_norm, gla_w_o, ffn_w_up, ffn_conv_w, ffn_conv_b, ffn_w_down):
    return _trunk(x, norm_mix, norm_ffn, norm_final, diff_w_qkv, diff_w_o, diff_lambda, diff_subln,
                  gla_w_in, gla_w_a1, gla_w_a2, gla_b_a, gla_norm, gla_w_o,
                  ffn_w_up, ffn_conv_w, ffn_conv_b, ffn_w_down, tm=1024, tp=512, tq=512, tf=256)
```

```python
import functools
import math

import jax
import jax.numpy as jnp
from jax import lax
from jax.experimental import pallas as pl
from jax.experimental.pallas import tpu as pltpu

F32 = jnp.float32
BF16 = jnp.bfloat16

NORM_EPS = 1e-6
ROPE_THETA = 10000.0

DIFF_HEADS = 8
DIFF_HEAD_DIM = 64
DIFF_V_DIM = 2 * DIFF_HEAD_DIM

GLA_HEADS = 4
GLA_DK = 128
GLA_DV = 256
GLA_GATE_TEMP = 16.0
GLA_BLOCK = 256
GLA_DIAG = 8

CONV_WIDTH = 3
FFN_HALO = 16
FFN_SUBTILES = 2

LANES = 128
VMEM_LIMIT_BYTES = 56 * 1024 * 1024


def _cparams(semantics):
    return pltpu.CompilerParams(dimension_semantics=semantics, vmem_limit_bytes=VMEM_LIMIT_BYTES)


def _rms(x, gain):
    ms = jnp.mean(x * x, axis=-1, keepdims=True)
    return x * lax.rsqrt(ms + NORM_EPS) * gain


NORM_PROJ_COLS = 1024


def _norm_proj_kernel(x_ref, g_ref, w_ref, wup_ref, wdn_ref, *rest, rope, gate):
    if rope:
        cos_ref, sin_ref, o_ref, wup_out_ref, wdn_out_ref = rest
    elif gate:
        w1_ref, w2_ref, b_ref, o_ref, gate_ref, wup_out_ref, wdn_out_ref = rest
    else:
        o_ref, wup_out_ref, wdn_out_ref = rest
    wup_out_ref[...] = wup_ref[...].astype(BF16)
    wdn_out_ref[...] = wdn_ref[...].astype(BF16)
    xn = _rms(x_ref[...], g_ref[...]).astype(BF16)
    if gate:
        t = jnp.dot(xn, w1_ref[...], preferred_element_type=F32)
        w2 = w2_ref[...]
        t_hi, w_hi = t.astype(BF16), w2.astype(BF16)
        t_lo, w_lo = (t - t_hi.astype(F32)).astype(BF16), (w2 - w_hi.astype(F32)).astype(BF16)
        z = (jnp.dot(t_hi, w_hi, preferred_element_type=F32) + jnp.dot(t_hi, w_lo, preferred_element_type=F32)
             + jnp.dot(t_lo, w_hi, preferred_element_type=F32) + b_ref[...])
        gate_ref[...] = (jnp.minimum(z, 0.0) - jnp.log1p(jnp.exp(-jnp.abs(z)))) * (1.0 / GLA_GATE_TEMP)
    for cb in range(o_ref.shape[1] // NORM_PROJ_COLS):
        cols = slice(cb * NORM_PROJ_COLS, (cb + 1) * NORM_PROJ_COLS)
        acc = jnp.dot(xn, w_ref[:, cols].astype(BF16), preferred_element_type=F32)
        if not rope:
            o_ref[:, cols] = acc.astype(o_ref.dtype)
            continue
        c = cos_ref[cb]
        s = sin_ref[cb]
        for hh in range(NORM_PROJ_COLS // LANES):
            t = acc[:, hh * LANES:(hh + 1) * LANES]
            lo = cb * NORM_PROJ_COLS + hh * LANES
            o_ref[:, lo:lo + LANES] = (t * c + pltpu.roll(t, LANES // 2, 1) * s).astype(o_ref.dtype)


def _norm_proj(x, gain, w, layer, ffn_w_up, ffn_w_down, ffn_layer, tables=None, gate_params=None, *, tm, seq):
    m, d = x.shape
    n = w.shape[2]
    nseq = seq // tm
    steps = m // tm
    rope = tables is not None
    gate = gate_params is not None
    up_rows, up_cols = ffn_w_up.shape[1:]
    dn_rows, dn_cols = ffn_w_down.shape[1:]
    up_slab = up_rows // steps
    dn_steps = steps // 2
    dn_slab = dn_rows // dn_steps
    assert up_slab * steps == up_rows and up_slab % 16 == 0 and dn_slab * dn_steps == dn_rows and dn_slab % 16 == 0
    in_specs = [
        pl.BlockSpec((tm, d), lambda i: (i, 0)),
        pl.BlockSpec((1, d), lambda i: (0, 0)),
        pl.BlockSpec((None, d, n), lambda i: (layer, 0, 0)),
        pl.BlockSpec((None, up_slab, up_cols), lambda i: (ffn_layer, i, 0)),
        pl.BlockSpec((None, dn_slab, dn_cols), lambda i: (ffn_layer, i // 2, 0)),
    ]
    out_shape = [jax.ShapeDtypeStruct((m, n), BF16)]
    out_specs = [pl.BlockSpec((tm, n), lambda i: (i, 0))]
    cast_shapes = [jax.ShapeDtypeStruct((up_rows, up_cols), BF16), jax.ShapeDtypeStruct((dn_rows, dn_cols), BF16)]
    cast_specs = [pl.BlockSpec((up_slab, up_cols), lambda i: (i, 0)),
                  pl.BlockSpec((dn_slab, dn_cols), lambda i: (i // 2, 0))]
    extra = ()
    if rope:
        nb = tables[0].shape[0]
        assert nb == n // NORM_PROJ_COLS
        in_specs += [pl.BlockSpec((nb, tm, LANES), lambda i: (0, i % nseq, 0))] * 2
        extra = tuple(tables)
    elif gate:
        w1, w2, bias = gate_params
        in_specs += [pl.BlockSpec(w1.shape, lambda i: (0, 0)), pl.BlockSpec(w2.shape, lambda i: (0, 0)),
                     pl.BlockSpec(bias.shape, lambda i: (0, 0))]
        extra = (w1, w2, bias)
        out_shape.append(jax.ShapeDtypeStruct((m, w2.shape[1]), F32))
        out_specs.append(pl.BlockSpec((tm, w2.shape[1]), lambda i: (i, 0)))
    return pl.pallas_call(
        functools.partial(_norm_proj_kernel, rope=rope, gate=gate),
        out_shape=tuple(out_shape + cast_shapes),
        grid=(steps,),
        in_specs=in_specs,
        out_specs=tuple(out_specs + cast_specs),
        compiler_params=_cparams(("arbitrary",)),
        name="norm_proj",
    )(x, gain, w, ffn_w_up, ffn_w_down, *extra)


ATTN_CHUNK = 256
ATTN_HEADS_PER_STEP = 4
ATTN_V_ROWS = DIFF_V_DIM + 16


def _diff_attn_kernel(lam_ref, sg_ref, q_ref, k_ref, v_ref, o_ref, vt_ref, s_ref, acc_ref, *, tq, lambda_init):
    qi = pl.program_id(2)
    n_kv_blocks = vt_ref.shape[1]
    n2 = 2 * tq
    ch = min(ATTN_CHUNK, tq)
    nch = tq // ch
    heads = range(ATTN_HEADS_PER_STEP)
    lanes = [slice(hh * LANES, (hh + 1) * LANES) for hh in heads]

    @pl.when(qi == 0)
    def _():
        for hh in heads:
            for c in range(n_kv_blocks):
                vt_ref[hh, c, 0:DIFF_V_DIM, :] = v_ref[0, c * tq:(c + 1) * tq, lanes[hh]].astype(F32).T.astype(BF16)
                ones_row = lax.broadcasted_iota(jnp.int32, (ATTN_V_ROWS - DIFF_V_DIM, tq), 0) == 0
                vt_ref[hh, c, DIFF_V_DIM:ATTN_V_ROWS, :] = jnp.where(ones_row, 1.0, 0.0).astype(BF16)

    qqt = []
    for hh in heads:
        qt = q_ref[0, :, lanes[hh]].astype(F32).T
        feat = lax.broadcasted_iota(jnp.int32, qt.shape, 0)
        map1 = (feat % DIFF_HEAD_DIM) < (DIFF_HEAD_DIM // 2)
        qqt.append(jnp.concatenate([jnp.where(map1, qt, 0.0), jnp.where(map1, 0.0, qt)], axis=1).astype(BF16))
    acc_ref[...] = jnp.zeros(acc_ref.shape, F32)

    def scores(hh, blk, slot, masked):
        cmx8 = None
        for c in range(nch):
            start = pl.multiple_of(blk * tq + c * ch, ch)
            s = jnp.dot(k_ref[0, pl.ds(start, ch), lanes[hh]], qqt[hh], preferred_element_type=F32)
            if masked:
                kv_pos = lax.broadcasted_iota(jnp.int32, s.shape, 0) + c * ch
                col = lax.broadcasted_iota(jnp.int32, s.shape, 1)
                q_pos = jnp.where(col >= tq, col - tq, col)
                s = jnp.where(kv_pos <= q_pos, s, -jnp.inf)
            s_ref[hh, slot, c * ch:(c + 1) * ch, :] = s
            part = jnp.max(s.reshape(ch // 8, 8, n2), axis=0)
            cmx8 = part if cmx8 is None else jnp.maximum(cmx8, part)
        return cmx8

    def softmax_pv(hh, blk, slot, m, cmx8):
        m_new = jnp.maximum(m, jnp.max(cmx8, axis=0, keepdims=True))
        alpha = jnp.exp2(m - m_new)
        pv = None
        for c in range(nch):
            p = jnp.exp2((s_ref[hh, slot, c * ch:(c + 1) * ch, :] - m_new).astype(BF16))
            d = jnp.dot(vt_ref[hh, blk, :, c * ch:(c + 1) * ch], p, preferred_element_type=F32)
            pv = d if pv is None else pv + d
        acc_ref[hh] = alpha * acc_ref[hh] + pv
        return (m_new,)

    cmx8_0 = [scores(hh, qi, 0, True) for hh in heads]

    def body(t, carry):
        slot = lax.rem(t, 2)
        stats = [softmax_pv(hh, jnp.where(t == 0, qi, t - 1), slot, *carry[hh]) for hh in heads]
        return tuple(stats[hh] + (scores(hh, t, 1 - slot, False),) for hh in heads)

    init = tuple((jnp.full((1, n2), -jnp.inf, F32), cmx8_0[hh]) for hh in heads)
    carry = lax.fori_loop(0, qi, body, init)

    lv = lam_ref[...]
    lam = (jnp.exp(jnp.sum(lv[0:1] * lv[1:2], axis=1, keepdims=True))
           - jnp.exp(jnp.sum(lv[2:3] * lv[3:4], axis=1, keepdims=True)) + lambda_init)
    for hh in heads:
        softmax_pv(hh, jnp.maximum(qi - 1, 0), lax.rem(qi, 2), *carry[hh])
    for hh in heads:
        l = acc_ref[hh, DIFF_V_DIM:DIFF_V_DIM + 1, :]
        o1 = acc_ref[hh, 0:DIFF_V_DIM, 0:tq] / l[:, 0:tq]
        o2 = acc_ref[hh, 0:DIFF_V_DIM, tq:n2] / l[:, tq:n2]
        o = o1 - lam * o2
        ms = jnp.mean(o * o, axis=0, keepdims=True)
        o = o * lax.rsqrt(ms + NORM_EPS) * sg_ref[...] * (1.0 - lambda_init)
        o_ref[0, :, lanes[hh]] = o.T.astype(o_ref.dtype)


def _diff_attention(qkv, lam_vecs, subln_gain, *, lambda_init, tq):
    b, s, _ = qkv.shape
    hps = ATTN_HEADS_PER_STEP
    groups = DIFF_HEADS // hps
    width = hps * LANES
    kern = functools.partial(_diff_attn_kernel, tq=tq, lambda_init=lambda_init)
    return pl.pallas_call(
        kern,
        out_shape=jax.ShapeDtypeStruct((b, s, DIFF_HEADS * DIFF_V_DIM), BF16),
        grid=(b, groups, s // tq),
        in_specs=[
            pl.BlockSpec((4, DIFF_HEAD_DIM), lambda bi, gi, qi: (0, 0)),
            pl.BlockSpec((DIFF_V_DIM, 1), lambda bi, gi, qi: (0, 0)),
            pl.BlockSpec((1, tq, width), lambda bi, gi, qi: (bi, qi, gi)),
            pl.BlockSpec((1, s, width), lambda bi, gi, qi: (bi, 0, groups + gi)),
            pl.BlockSpec((1, s, width), lambda bi, gi, qi: (bi, 0, 2 * groups + gi)),
        ],
        out_specs=pl.BlockSpec((1, tq, width), lambda bi, gi, qi: (bi, qi, gi)),
        scratch_shapes=[
            pltpu.VMEM((hps, s // tq, ATTN_V_ROWS, tq), BF16),
            pltpu.VMEM((hps, 2, tq, 2 * tq), F32),
            pltpu.VMEM((hps, ATTN_V_ROWS, 2 * tq), F32),
        ],
        compiler_params=_cparams(("parallel", "parallel", "arbitrary")),
        name="diff_attn",
    )(lam_vecs, subln_gain, qkv, qkv, qkv)


def _split3(x):
    hi = x.astype(BF16)
    r1 = x - hi.astype(F32)
    mid = r1.astype(BF16)
    lo = (r1 - mid.astype(F32)).astype(BF16)
    return hi, mid, lo


def _gla_kernel(q_ref, k_ref, v_ref, r_ref, g_ref, ng_ref, o_ref, state_ref):
    t_blk = q_ref.shape[1]
    hk = GLA_HEADS * GLA_DK

    @pl.when(pl.program_id(1) == 0)
    def _():
        state_ref[...] = jnp.zeros(state_ref.shape, F32)

    ri = lax.broadcasted_iota(jnp.int32, (t_blk, t_blk), 0)
    ci = lax.broadcasted_iota(jnp.int32, (t_blk, t_blk), 1)

    tri = jnp.where(ci <= ri, 1.0, 0.0).astype(BF16)
    g_hi, g_mid, g_lo = _split3(g_ref[0])
    b = (jnp.dot(tri, g_hi, preferred_element_type=F32)
         + jnp.dot(tri, g_mid, preferred_element_type=F32)
         + jnp.dot(tri, g_lo, preferred_element_type=F32))
    b = b * math.log2(math.e)

    q = q_ref[0].astype(F32) * (GLA_DK ** -0.5)
    k = k_ref[0].astype(F32)

    row = lax.broadcasted_iota(jnp.int32, (t_blk, hk), 0)
    b_last = b[t_blk - 1:t_blk, :]
    q_in = (q * jnp.exp2(b)).astype(BF16)
    k_out = (k * jnp.exp2(b_last - b)).astype(BF16)
    e_last = jnp.exp2(b_last)

    level = t_blk // 2
    levels = []
    while level >= GLA_DIAG:
        grp = 2 * level
        b3 = b.reshape(t_blk // grp, grp, hk)
        pivot = jnp.broadcast_to(b3[:, level - 1:level, :], b3.shape).reshape(t_blk, hk)
        upper = (row % grp) >= level
        e = jnp.exp2(-jnp.abs(b - pivot))
        qt = jnp.where(upper, q * e, 0.0).astype(BF16)
        kt = jnp.where(upper, 0.0, k * e).astype(BF16)
        levels.append((grp, qt, kt))
        level //= 2

    nd = t_blk // GLA_DIAG
    b3 = b.reshape(nd, GLA_DIAG, hk)
    q3 = q.reshape(nd, GLA_DIAG, hk)
    k3 = k.reshape(nd, GLA_DIAG, hk)
    sel_shape = (GLA_DIAG * GLA_DK, t_blk)
    sel = jnp.where(lax.broadcasted_iota(jnp.int32, sel_shape, 0) // GLA_DK
                    == lax.broadcasted_iota(jnp.int32, sel_shape, 1) % GLA_DIAG, 1.0, 0.0).astype(BF16)
    diag_mask = ((ri // GLA_DIAG) == (ci // GLA_DIAG)) & (ci <= ri)

    eye = (lax.broadcasted_iota(jnp.int32, (GLA_DK, GLA_DK), 0)
           == lax.broadcasted_iota(jnp.int32, (GLA_DK, GLA_DK), 1))

    diag_terms = []
    for j in range(GLA_DIAG):
        d = jnp.minimum(b3 - b3[:, j:j + 1, :], 0.0)
        pj = (jnp.exp2(d) * q3 * k3[:, j:j + 1, :]).reshape(t_blk, hk)
        diag_terms.append(pj)

    ng = ng_ref[...]
    for h in range(GLA_HEADS):
        ks = slice(h * GLA_DK, (h + 1) * GLA_DK)
        vs = slice(h * GLA_DV, (h + 1) * GLA_DV)
        a = jnp.zeros((t_blk, t_blk), F32)
        for grp, qt, kt in levels:
            s_l = lax.dot_general(qt[:, ks], kt[:, ks], (((1,), (1,)), ((), ())),
                                  preferred_element_type=F32)
            if grp == t_blk:
                a = a + s_l
            else:
                a = a + jnp.where((ri // grp) == (ci // grp), s_l, 0.0)
        stacked = jnp.concatenate([diag_terms[j][:, ks] for j in range(GLA_DIAG)], axis=1)
        a = jnp.where(diag_mask, jnp.dot(stacked.astype(BF16), sel, preferred_element_type=F32), a)

        v_h = v_ref[0, :, vs]
        state = state_ref[h]
        o = (jnp.dot(q_in[:, ks], state.astype(BF16), preferred_element_type=F32)
             + jnp.dot(a.astype(BF16), v_h, preferred_element_type=F32))

        e_col = jnp.sum(jnp.where(eye, jnp.broadcast_to(e_last[:, ks], (GLA_DK, GLA_DK)), 0.0),
                        axis=1, keepdims=True)
        upd = lax.dot_general(k_out[:, ks], v_h, (((0,), (0,)), ((), ())),
                              preferred_element_type=F32)
        state_ref[h] = e_col * state + upd

        r_h = r_ref[0, :, vs].astype(F32)
        gate = r_h / (1.0 + jnp.exp(-r_h))
        o_ref[0, :, vs] = (_rms(o, ng) * gate).astype(o_ref.dtype)


def _gla(proj, g, norm_gain):
    b, s, _ = proj.shape
    t = GLA_BLOCK
    hk = GLA_HEADS * GLA_DK
    hv = GLA_HEADS * GLA_DV
    return pl.pallas_call(
        _gla_kernel,
        out_shape=jax.ShapeDtypeStruct((b, s, hv), BF16),
        grid=(b, s // t),
        in_specs=[
            pl.BlockSpec((1, t, hk), lambda bi, ti: (bi, ti, 0)),
            pl.BlockSpec((1, t, hk), lambda bi, ti: (bi, ti, 1)),
            pl.BlockSpec((1, t, hv), lambda bi, ti: (bi, ti, 1)),
            pl.BlockSpec((1, t, hv), lambda bi, ti: (bi, ti, 2)),
            pl.BlockSpec((1, t, hk), lambda bi, ti: (bi, ti, 0)),
            pl.BlockSpec((1, GLA_DV), lambda bi, ti: (0, 0)),
        ],
        out_specs=pl.BlockSpec((1, t, hv), lambda bi, ti: (bi, ti, 0)),
        scratch_shapes=[pltpu.VMEM((GLA_HEADS, GLA_DK, GLA_DV), F32)],
        compiler_params=_cparams(("parallel", "arbitrary")),
        name="gla",
    )(proj, proj, proj, proj, g, norm_gain)


def _ffn_kernel(x_ref, halo_ref, a_ref, ahalo_ref, wo_ref, g_ref, wup_ref, cw_ref, cb_ref, wd_ref, fg_ref,
                o_ref, xn_ref, slab_ref, *, tiles_per_seq, final_norm, tf):
    i = pl.program_id(0)
    tm, d = x_ref.shape
    f = wd_ref.shape[0]
    rows = tm // FFN_SUBTILES
    ext = rows + FFN_HALO
    seg = ext // 8
    nslab = d // LANES
    subs = range(FFN_SUBTILES)

    gain = g_ref[...]
    w_o = wo_ref[...].astype(BF16)
    resid = []
    for s in subs:
        if s == 0:
            x_ext = jnp.concatenate([halo_ref[...], x_ref[0:rows, :]], axis=0)
            a_ext = jnp.concatenate([ahalo_ref[...], a_ref[0:rows, :]], axis=0)
        else:
            x_ext = x_ref[s * rows - FFN_HALO:(s + 1) * rows, :]
            a_ext = a_ref[s * rows - FFN_HALO:(s + 1) * rows, :]
        h_ext = x_ext + jnp.dot(a_ext, w_o, preferred_element_type=F32)
        resid.append(h_ext[FFN_HALO:ext, :])
        halo = _rms(h_ext[0:FFN_HALO, :], gain)
        if s == 0:
            halo = jnp.where(i % tiles_per_seq == 0, 0.0, halo)
        xn = _rms(resid[s], gain)
        for c in range(nslab):
            slab_ref[s, c, 0:FFN_HALO, :] = halo[:, c * LANES:(c + 1) * LANES]
            slab_ref[s, c, FFN_HALO:ext, :] = xn[:, c * LANES:(c + 1) * LANES]
        for a2 in range(seg // 2):
            grp = [jnp.concatenate([slab_ref[s, c, pl.ds(2 * a2 + r, 8, stride=seg), :] for c in range(nslab)],
                                   axis=1) for r in range(2)]
            xn_ref[s, 16 * a2:16 * a2 + 16, :] = jnp.concatenate(grp, axis=0).astype(BF16)

    def conv(u, cols):
        u3 = u.reshape(seg, 8, u.shape[-1])
        wrap1 = pltpu.roll(u3[seg - 1], 1, 0)[None]
        wrap2 = pltpu.roll(u3[seg - 2], 1, 0)[None]
        prev1 = jnp.concatenate([wrap1, u3[:seg - 1]], axis=0)
        prev2 = jnp.concatenate([wrap2, wrap1, u3[:seg - 2]], axis=0)
        return (cw_ref[2:3, cols] * u3 + cw_ref[1:2, cols] * prev1 + cw_ref[0:1, cols] * prev2
                + cb_ref[:, cols])

    accs = [None for _ in subs]
    for c in range(f // tf):
        gcols = slice(c * tf, (c + 1) * tf)
        ucols = slice(f + c * tf, f + (c + 1) * tf)
        w_gate = wup_ref[:, gcols].astype(BF16)
        w_up = wup_ref[:, ucols].astype(BF16)
        w_down = wd_ref[c * tf:(c + 1) * tf, :].astype(BF16)
        ups = [(jnp.dot(xn_ref[s], w_gate, preferred_element_type=F32),
                jnp.dot(xn_ref[s], w_up, preferred_element_type=F32)) for s in subs]
        acts = []
        for s in subs:
            gate = conv(ups[s][0], gcols)
            up = conv(ups[s][1], ucols)
            acts.append(((gate / (1.0 + jnp.exp(-gate))) * up).reshape(ext, tf).astype(BF16))
        for s in subs:
            part = jnp.dot(acts[s], w_down, preferred_element_type=F32)
            accs[s] = part if accs[s] is None else accs[s] + part

    for s in subs:
        for a in range(seg):
            for c in range(nslab):
                slab_ref[s, c, pl.ds(a, 8, stride=seg), :] = accs[s][8 * a:8 * a + 8, c * LANES:(c + 1) * LANES]
        y = resid[s] + jnp.concatenate([slab_ref[s, c, FFN_HALO:ext, :] for c in range(nslab)], axis=1)
        if final_norm:
            y = _rms(y, fg_ref[...])
        o_ref[s * rows:(s + 1) * rows, :] = y


def _ffn(x, a, w_o, mixer_layer, gain, w_up, conv_w, conv_b, w_down, layer, final_gain, *, tm, tf, seq,
         final_norm):
    m, d = x.shape
    f = w_down.shape[0]
    tiles_per_seq = seq // tm
    halo_blocks = tm // FFN_HALO
    ext = tm // FFN_SUBTILES + FFN_HALO
    kern = functools.partial(_ffn_kernel, tiles_per_seq=tiles_per_seq, final_norm=final_norm, tf=tf)
    return pl.pallas_call(
        kern,
        out_shape=jax.ShapeDtypeStruct((m, d), F32),
        grid=(m // tm,),
        in_specs=[
            pl.BlockSpec((tm, d), lambda i: (i, 0)),
            pl.BlockSpec((FFN_HALO, d), lambda i: (jnp.maximum(i * halo_blocks - 1, 0), 0)),
            pl.BlockSpec((tm, d), lambda i: (i, 0)),
            pl.BlockSpec((FFN_HALO, d), lambda i: (jnp.maximum(i * halo_blocks - 1, 0), 0)),
            pl.BlockSpec((None, d, d), lambda i: (mixer_layer, 0, 0)),
            pl.BlockSpec((None, 1, d), lambda i: (layer, 0, 0)),
            pl.BlockSpec((d, 2 * f), lambda i: (0, 0)),
            pl.BlockSpec((None, CONV_WIDTH, 2 * f), lambda i: (layer, 0, 0)),
            pl.BlockSpec((None, 1, 2 * f), lambda i: (layer, 0, 0)),
            pl.BlockSpec((f, d), lambda i: (0, 0)),
            pl.BlockSpec((1, d), lambda i: (0, 0)),
        ],
        out_specs=pl.BlockSpec((tm, d), lambda i: (i, 0)),
        scratch_shapes=[pltpu.VMEM((FFN_SUBTILES, ext, d), BF16),
                        pltpu.VMEM((FFN_SUBTILES, d // LANES, ext, LANES), F32)],
        compiler_params=_cparams(("parallel",)),
        name="ffn",
    )(x, x, a, a, w_o, gain, w_up, conv_w, conv_b, w_down, final_gain)


def _rope_tables(seq, q_scale):
    half = DIFF_HEAD_DIM // 2
    inv = 1.0 / (ROPE_THETA ** (jnp.arange(0, DIFF_HEAD_DIM, 2, dtype=F32) / DIFF_HEAD_DIM))
    ang = jnp.arange(seq, dtype=F32)[:, None] * inv[None, :]
    cos, sin = jnp.cos(ang), jnp.sin(ang)
    cos_t = jnp.tile(cos, (1, LANES // half))
    sin_t = jnp.concatenate([-sin, -sin, sin, sin], axis=1)
    cos3 = jnp.stack([cos_t * q_scale, cos_t, jnp.ones_like(cos_t)])
    sin3 = jnp.stack([sin_t * q_scale, sin_t, jnp.zeros_like(sin_t)])
    return cos3, sin3


def _reorder_qk_columns(w_qkv):
    layers, d, _ = w_qkv.shape
    width = DIFF_HEADS * DIFF_V_DIM
    half = DIFF_HEAD_DIM // 2
    qk = w_qkv[:, :, :2 * width].reshape(layers, d, 2, DIFF_HEADS, 2, 2, half)
    qk = qk.transpose(0, 1, 2, 3, 5, 4, 6).reshape(layers, d, 2 * width)
    return jnp.concatenate([qk, w_qkv[:, :, 2 * width:]], axis=2).astype(BF16)


def _trunk(x, norm_mix, norm_ffn, norm_final, diff_w_qkv, diff_w_o, diff_lambda, diff_subln,
           gla_w_in, gla_w_a1, gla_w_a2, gla_b_a, gla_norm, gla_w_o,
           ffn_w_up, ffn_conv_w, ffn_conv_b, ffn_w_down, *, tm, tp, tq, tf):
    bsz, seq, d = x.shape
    depth = norm_mix.shape[0]
    m = bsz * seq
    tables = _rope_tables(seq, DIFF_HEAD_DIM ** -0.5 * math.log2(math.e))
    w_qkv = _reorder_qk_columns(diff_w_qkv)
    h = x.reshape(m, d)
    rank = gla_w_a1.shape[-1]
    for layer in range(depth):
        jdx = layer // 2
        gain = norm_mix[layer].reshape(1, d)
        if layer % 2 == 0:
            lambda_init = 0.8 - 0.6 * math.exp(-0.3 * layer)
            qkv, w_up, w_down = _norm_proj(h, gain, w_qkv, jdx, ffn_w_up, ffn_w_down, layer, tables=tables,
                                           tm=tm, seq=seq)
            o = _diff_attention(qkv.reshape(bsz, seq, 3 * d), diff_lambda[jdx],
                                diff_subln[jdx].reshape(DIFF_V_DIM, 1),
                                lambda_init=lambda_init, tq=tq)
            w_o = diff_w_o
        else:
            w1 = jnp.pad(gla_w_a1[jdx], ((0, 0), (0, LANES - rank))).astype(BF16)
            w2 = jnp.pad(gla_w_a2[jdx], ((0, LANES - rank), (0, 0)))
            proj, g, w_up, w_down = _norm_proj(h, gain, gla_w_in, jdx, ffn_w_up, ffn_w_down, layer,
                                               gate_params=(w1, w2, gla_b_a[jdx].reshape(1, -1)), tm=tm, seq=seq)
            o = _gla(proj.reshape(bsz, seq, 3 * d), g.reshape(bsz, seq, -1),
                     gla_norm[jdx].reshape(1, GLA_DV))
            w_o = gla_w_o
        h = _ffn(h, o.reshape(m, d), w_o, jdx, norm_ffn.reshape(depth, 1, d), w_up, ffn_conv_w, ffn_conv_b.reshape(depth, 1, -1),
                 w_down, layer, norm_final.reshape(1, d), tm=tp, tf=tf, seq=seq,
                 final_norm=(layer == depth - 1))
    return h.reshape(bsz, seq, d)


def kernel(x, norm_mix, norm_ffn, norm_final, diff_w_qkv, diff_w_o, diff_lambda, diff_subln, gla_w_in, gla_w_a1, gla_w_a2, gla_b_a, gla_norm, gla_w_o, ffn_w_up, ffn_conv_w, ffn_conv_b, ffn_w_down):
    return _trunk(x, norm_mix, norm_ffn, norm_final, diff_w_qkv, diff_w_o, diff_lambda, diff_subln,
                  gla_w_in, gla_w_a1, gla_w_a2, gla_b_a, gla_norm, gla_w_o,
                  ffn_w_up, ffn_conv_w, ffn_conv_b, ffn_w_down, tm=1024, tp=512, tq=512, tf=256)
```
